```python
import math
import jax, jax.numpy as jnp
from jax import lax
import numpy as np


D_MODEL = 2048
BATCH = 2
SEQ = 4096
DEPTH = 4
DEC_BATCH = 32
DEC_SEQ = 8
PAST_LEN = 16384
PAGE_SIZE = 128

HEAD_DIM = 128
N_HEADS = 12
KV_HEADS = 4
DIFF_DIM = HEAD_DIM // 2
MEM_HEADS = 4
MEM_LEN = 256
WINDOW = 128
Q_BLOCK = 128
NUM_BUCKETS = 32
MAX_DISTANCE = 128
N_EXPERTS = 16
N_GROUPS = 4
EXPERTS_PER_GROUP = N_EXPERTS // N_GROUPS
TOP_K = 2
D_EXPERT = D_MODEL // 2
MOE_BLOCK = 128
N_SWA_LAYERS = (DEPTH + 1) // 2
N_DIFF_LAYERS = DEPTH // 2
Q_W = N_HEADS * HEAD_DIM
KV_W = KV_HEADS * HEAD_DIM
MEM_W = MEM_HEADS * HEAD_DIM
IN_W = Q_W + 2 * KV_W + MEM_W
MIX_W = Q_W + MEM_W
ALPHA = (2.0 * DEPTH) ** 0.25
BETA = (8.0 * DEPTH) ** -0.25
LN_EPS = 1e-5
RMS_EPS = 1e-5
NEG_INF = -1e30

kernel_name = 'hybrid_swa_diffattn_memxattn_groupmoe_step'


def t5_bucket(rel):
    n = jnp.maximum(-rel, 0)
    max_exact = NUM_BUCKETS // 2
    nf = jnp.maximum(n, 1).astype(jnp.float32)
    large = max_exact + (jnp.log(nf / max_exact) / math.log(MAX_DISTANCE / max_exact)
                         * (NUM_BUCKETS - max_exact)).astype(jnp.int32)
    large = jnp.minimum(large, NUM_BUCKETS - 1)
    return jnp.where(n < max_exact, n, large)


def rel_bias_logits(table, q_pos, k_pos):
    rel = k_pos[None, :] - q_pos[:, None]
    bias = jnp.transpose(table[t5_bucket(rel)], (2, 0, 1)).astype(jnp.float32)
    return bias, rel


def layer_norm(x, g, b):
    xf = x.astype(jnp.float32)
    mu = jnp.mean(xf, -1, keepdims=True)
    var = jnp.mean(jnp.square(xf - mu), -1, keepdims=True)
    y = (xf - mu) * lax.rsqrt(var + LN_EPS)
    return (y * g.astype(jnp.float32) + b.astype(jnp.float32)).astype(x.dtype)


def deepnorm(x, f, g, b):
    return layer_norm(ALPHA * x + f, g, b)


def split_proj(x, w):
    B, T, _ = x.shape
    p = x @ w
    q, k, v, qm = jnp.split(p, [Q_W, Q_W + KV_W, Q_W + 2 * KV_W], axis=-1)
    return (q.reshape(B, T, N_HEADS, HEAD_DIM), k.reshape(B, T, KV_HEADS, HEAD_DIM),
            v.reshape(B, T, KV_HEADS, HEAD_DIM), qm.reshape(B, T, MEM_HEADS, HEAD_DIM))


def sink_attend(q, k, v, bias, mask, sink):
    N, Tq, H, d = q.shape
    Tk, G = k.shape[1], k.shape[2]
    R = H // G
    s = jnp.einsum('nqgrd,nkgd->ngrqk', q.reshape(N, Tq, G, R, d), k,
                   preferred_element_type=jnp.float32)
    s = s * (d ** -0.5) + bias.reshape(G, R, Tq, Tk)
    s = jnp.where(mask, s, NEG_INF)
    sink_col = jnp.broadcast_to(sink.astype(jnp.float32).reshape(G, R, 1, 1), s.shape[:-1] + (1,))
    p = jax.nn.softmax(jnp.concatenate([s, sink_col], axis=-1), axis=-1)[..., :Tk]
    o = jnp.einsum('ngrqk,nkgd->nqgrd', p.astype(v.dtype), v)
    return o.reshape(N, Tq, H * d)


def swa_prompt(q, k, v, table, sink):
    B, S, H, d = q.shape
    G = k.shape[2]
    nb = S // WINDOW
    pad = ((0, 0), (WINDOW, 0), (0, 0), (0, 0))

    def band(t):
        tb = jnp.pad(t, pad).reshape(B, nb + 1, WINDOW, G, d)
        return jnp.concatenate([tb[:, :-1], tb[:, 1:]], axis=2).reshape(B * nb, 2 * WINDOW, G, d)

    q_off = jnp.arange(WINDOW)
    k_off = jnp.arange(2 * WINDOW) - WINDOW
    bias, rel = rel_bias_logits(table, q_off, k_off)
    band_mask = (rel <= 0) & (rel > -WINDOW)
    k_abs = jnp.arange(nb)[:, None] * WINDOW + k_off[None, :]
    mask = band_mask[None] & (k_abs >= 0)[:, None, :]
    mask = jnp.broadcast_to(mask[None], (B, nb, WINDOW, 2 * WINDOW)).reshape(B * nb, 1, 1, WINDOW, 2 * WINDOW)
    o = sink_attend(q.reshape(B * nb, WINDOW, H, d), band(k), band(v), bias, mask, sink)
    return o.reshape(B, S, H * d)


def swa_sample(q, k_new, v_new, win_k, win_v, table, sink):
    T = q.shape[1]
    kk = jnp.concatenate([win_k, k_new], axis=1)
    vv = jnp.concatenate([win_v, v_new], axis=1)
    q_pos = PAST_LEN + jnp.arange(T)
    k_pos = PAST_LEN - WINDOW + jnp.arange(WINDOW + T)
    bias, rel = rel_bias_logits(table, q_pos, k_pos)
    mask = (rel <= 0) & (rel > -WINDOW)
    o = sink_attend(q, kk, vv, bias, mask, sink)
    return o, kk[:, T:], vv[:, T:]


def diff_core(q, k, v, bias, mask, lam, lam_init, g):
    N, Tq, H = q.shape[:3]
    Tk, G = k.shape[1], k.shape[2]
    R = H // G
    s = jnp.einsum('nqgrcd,nkgcd->ngrcqk', q.reshape(N, Tq, G, R, 2, DIFF_DIM), k,
                   preferred_element_type=jnp.float32)
    s = s * (DIFF_DIM ** -0.5) + bias.reshape(G, R, 1, Tq, Tk)
    s = jnp.where(mask, s, NEG_INF)
    p = jax.nn.softmax(s, axis=-1)
    a = p[:, :, :, 0] - lam * p[:, :, :, 1]
    o = jnp.einsum('ngrqk,nkgd->nqgrd', a.astype(v.dtype), v).astype(jnp.float32)
    o = o * lax.rsqrt(jnp.mean(jnp.square(o), -1, keepdims=True) + RMS_EPS) * g.astype(jnp.float32) * (1.0 - lam_init)
    return o.astype(v.dtype).reshape(N, Tq, H * HEAD_DIM)


def diff_prompt(q, k, v, table, lam, lam_init, g):
    B, S = q.shape[:2]
    nb = S // Q_BLOCK
    qb = q.reshape(B, nb, Q_BLOCK, N_HEADS, 2, DIFF_DIM).swapaxes(0, 1)
    kk = k.reshape(B, S, KV_HEADS, 2, DIFF_DIM)
    k_pos = jnp.arange(S)

    def block(args):
        qi, bi = args
        q_pos = bi * Q_BLOCK + jnp.arange(Q_BLOCK)
        bias, rel = rel_bias_logits(table, q_pos, k_pos)
        return diff_core(qi, kk, v, bias, rel <= 0, lam, lam_init, g)

    o = lax.map(block, (qb, jnp.arange(nb)))
    return o.swapaxes(0, 1).reshape(B, S, N_HEADS * HEAD_DIM)


def diff_sample(q, k_new, v_new, cache_k, cache_v, li, page_table, table, lam, lam_init, g):
    T = q.shape[1]
    q_pos = PAST_LEN + jnp.arange(T)
    k_pos = jnp.arange(PAST_LEN + T)
    bias, rel = rel_bias_logits(table, q_pos, k_pos)
    mask = rel <= 0

    def one_seq(args):
        qi, kn, vn, pt = args
        pk = cache_k[li, pt].reshape(PAST_LEN, KV_HEADS, HEAD_DIM)
        pv = cache_v[li, pt].reshape(PAST_LEN, KV_HEADS, HEAD_DIM)
        kk = jnp.concatenate([pk, kn], axis=0).reshape(1, PAST_LEN + T, KV_HEADS, 2, DIFF_DIM)
        vv = jnp.concatenate([pv, vn], axis=0)[None]
        return diff_core(qi.reshape(1, T, N_HEADS, 2, DIFF_DIM), kk, vv, bias, mask, lam, lam_init, g)[0]

    return lax.map(one_seq, (q, k_new, v_new, page_table))


def mem_kv(mem, w):
    N, M, _ = mem.shape
    k, v = jnp.split(mem @ w, 2, axis=-1)
    return k.reshape(N, M, MEM_HEADS, HEAD_DIM), v.reshape(N, M, MEM_HEADS, HEAD_DIM)


def mem_attend(qm, mk, mv):
    N, Tq = qm.shape[:2]
    s = jnp.einsum('nqhd,nkhd->nhqk', qm, mk, preferred_element_type=jnp.float32) * (HEAD_DIM ** -0.5)
    p = jax.nn.softmax(s, axis=-1)
    o = jnp.einsum('nhqk,nkhd->nqhd', p.astype(mv.dtype), mv)
    return o.reshape(N, Tq, MEM_W)


def route(x2d, w_router, router_bias):
    logits = jnp.dot(x2d, w_router, preferred_element_type=jnp.float32)
    scores = jax.nn.sigmoid(logits)
    sel = (scores + router_bias.astype(jnp.float32)).reshape(-1, N_GROUPS, EXPERTS_PER_GROUP)
    group_score = jnp.sum(lax.top_k(sel, 2)[0], axis=-1)
    g_idx = jnp.argmax(group_score, axis=-1).astype(jnp.int32)
    in_group = jnp.take_along_axis(sel, g_idx[:, None, None], axis=1)[:, 0]
    _, local = lax.top_k(in_group, TOP_K)
    e_idx = g_idx[:, None] * EXPERTS_PER_GROUP + local
    w = jnp.take_along_axis(scores, e_idx, axis=1)
    return e_idx, w / jnp.sum(w, -1, keepdims=True)


def moe_ffn(x2d, w_router, router_bias, w_gu, w_dn):
    T = x2d.shape[0]
    e_idx, gate = route(x2d, w_router, router_bias)
    A = T * TOP_K
    flat_e = e_idx.reshape(A)
    flat_tok = jnp.arange(A, dtype=jnp.int32) // TOP_K
    order = jnp.argsort(flat_e)
    se = flat_e[order]
    counts = jnp.zeros((N_EXPERTS,), jnp.int32).at[flat_e].add(1)
    starts = jnp.cumsum(counts) - counts
    padded = (counts + MOE_BLOCK - 1) // MOE_BLOCK * MOE_BLOCK
    pad_ends = jnp.cumsum(padded)
    pad_starts = pad_ends - padded
    dest = pad_starts[se] + jnp.arange(A, dtype=jnp.int32) - starts[se]
    n_blocks = -(-A // MOE_BLOCK) + N_EXPERTS
    sorted_tok = flat_tok[order]
    row_tok = jnp.zeros((n_blocks * MOE_BLOCK,), jnp.int32).at[dest].set(sorted_tok)
    blk_e = jnp.minimum(jnp.searchsorted(pad_ends, jnp.arange(n_blocks, dtype=jnp.int32) * MOE_BLOCK,
                                         side='right'), N_EXPERTS - 1)
    xb = x2d[row_tok].reshape(n_blocks, MOE_BLOCK, x2d.shape[1])

    def expert_block(args):
        xi, e = args
        g, u = jnp.split(xi @ w_gu[e], 2, axis=-1)
        return (jax.nn.silu(g) * u) @ w_dn[e]

    yb = lax.map(expert_block, (xb, blk_e)).reshape(n_blocks * MOE_BLOCK, x2d.shape[1])
    ya = yb[dest] * gate.reshape(A)[order][:, None].astype(x2d.dtype)
    return jnp.zeros_like(x2d).at[sorted_tok].add(ya)


def setup_inputs(seed: int = 0) -> dict:
    key = jax.random.key(seed)
    ks = jax.random.split(key, 32)
    f32 = jnp.float32
    n_pages = PAST_LEN // PAGE_SIZE
    n_pool = (5 * DEC_BATCH * n_pages) // 4

    def nrm(k, shape, scale):
        return jax.random.normal(k, shape, f32) * scale

    in_scale = jnp.ones((IN_W,), f32).at[Q_W + KV_W:Q_W + 2 * KV_W].set(BETA)
    mem_scale = jnp.ones((2 * MEM_W,), f32).at[MEM_W:].set(BETA)
    page_table = jax.random.permutation(ks[9], n_pool)[:DEC_BATCH * n_pages].reshape(DEC_BATCH, n_pages).astype(jnp.int32)
    return {
        'x_prompt': nrm(ks[0], (BATCH, SEQ, D_MODEL), 1.0),
        'x_sample': nrm(ks[1], (DEC_BATCH, DEC_SEQ, D_MODEL), 1.0),
        'cache_win_k': nrm(ks[2], (N_SWA_LAYERS, DEC_BATCH, WINDOW, KV_HEADS, HEAD_DIM), 1.0),
        'cache_win_v': nrm(ks[3], (N_SWA_LAYERS, DEC_BATCH, WINDOW, KV_HEADS, HEAD_DIM), BETA),
        'cache_diff_k': nrm(ks[4], (N_DIFF_LAYERS, n_pool, PAGE_SIZE, KV_HEADS, HEAD_DIM), 1.0),
        'cache_diff_v': nrm(ks[5], (N_DIFF_LAYERS, n_pool, PAGE_SIZE, KV_HEADS, HEAD_DIM), BETA),
        'cache_mem_k': nrm(ks[6], (DEPTH, DEC_BATCH, MEM_LEN, MEM_HEADS, HEAD_DIM), 1.0),
        'cache_mem_v': nrm(ks[7], (DEPTH, DEC_BATCH, MEM_LEN, MEM_HEADS, HEAD_DIM), BETA),
        'page_table': page_table,
        'mem_prompt': nrm(ks[10], (BATCH, MEM_LEN, D_MODEL), 1.0),
        'w_in': nrm(ks[11], (DEPTH, D_MODEL, IN_W), D_MODEL ** -0.5) * in_scale,
        'w_mem_kv': nrm(ks[12], (DEPTH, D_MODEL, 2 * MEM_W), D_MODEL ** -0.5) * mem_scale,
        'w_o': nrm(ks[13], (DEPTH, MIX_W, D_MODEL), MIX_W ** -0.5 * BETA),
        'rel_bias': nrm(ks[14], (NUM_BUCKETS, N_HEADS), 0.5),
        'sinks': nrm(ks[15], (N_SWA_LAYERS, N_HEADS), 0.5),
        'lam_q1': nrm(ks[16], (N_DIFF_LAYERS, DIFF_DIM), 0.1),
        'lam_k1': nrm(ks[17], (N_DIFF_LAYERS, DIFF_DIM), 0.1),
        'lam_q2': nrm(ks[18], (N_DIFF_LAYERS, DIFF_DIM), 0.1),
        'lam_k2': nrm(ks[19], (N_DIFF_LAYERS, DIFF_DIM), 0.1),
        'subln_g': 1.0 + nrm(ks[20], (N_DIFF_LAYERS, HEAD_DIM), 0.02),
        'ln1_g': 1.0 + nrm(ks[21], (DEPTH, D_MODEL), 0.02),
        'ln1_b': nrm(ks[22], (DEPTH, D_MODEL), 0.02),
        'ln2_g': 1.0 + nrm(ks[23], (DEPTH, D_MODEL), 0.02),
        'ln2_b': nrm(ks[24], (DEPTH, D_MODEL), 0.02),
        'w_router': nrm(ks[25], (D_MODEL, N_EXPERTS), D_MODEL ** -0.5),
        'router_bias': nrm(ks[26], (N_EXPERTS,), 0.01),
        'w_gate_up': nrm(ks[27], (DEPTH, N_EXPERTS, D_MODEL, 2 * D_EXPERT), D_MODEL ** -0.5),
        'w_down': nrm(ks[28], (DEPTH, N_EXPERTS, D_EXPERT, D_MODEL), D_EXPERT ** -0.5 * BETA),
    }


def reference(x_prompt, x_sample, cache_win_k, cache_win_v, cache_diff_k, cache_diff_v,
              cache_mem_k, cache_mem_v, page_table, mem_prompt, w_in, w_mem_kv, w_o,
              rel_bias, sinks, lam_q1, lam_k1, lam_q2, lam_k2, subln_g,
              ln1_g, ln1_b, ln2_g, ln2_b, w_router, router_bias, w_gate_up, w_down):
    f32 = jnp.float32
    B, S = x_prompt.shape[:2]
    DB, T = x_sample.shape[:2]
    win_k_p, win_v_p, win_k_s, win_v_s = [], [], [], []
    diff_k_p, diff_v_p, diff_k_s, diff_v_s = [], [], [], []
    mem_k_p, mem_v_p = [], []
    xp, xs = x_prompt, x_sample
    for l in range(DEPTH):
        i = l // 2
        qp, kp, vp, qmp = split_proj(xp, w_in[l])
        qs, ks, vs, qms = split_proj(xs, w_in[l])
        if l % 2 == 0:
            self_p = swa_prompt(qp, kp, vp, rel_bias, sinks[i])
            self_s, nk, nv = swa_sample(qs, ks, vs, cache_win_k[i], cache_win_v[i], rel_bias, sinks[i])
            win_k_p.append(kp[:, S - WINDOW:])
            win_v_p.append(vp[:, S - WINDOW:])
            win_k_s.append(nk)
            win_v_s.append(nv)
        else:
            lam_init = 0.8 - 0.6 * math.exp(-0.3 * l)
            lam = (jnp.exp(jnp.sum(lam_q1[i].astype(f32) * lam_k1[i].astype(f32)))
                   - jnp.exp(jnp.sum(lam_q2[i].astype(f32) * lam_k2[i].astype(f32))) + lam_init)
            self_p = diff_prompt(qp, kp, vp, rel_bias, lam, lam_init, subln_g[i])
            self_s = diff_sample(qs, ks, vs, cache_diff_k, cache_diff_v, i, page_table,
                                 rel_bias, lam, lam_init, subln_g[i])
            diff_k_p.append(kp)
            diff_v_p.append(vp)
            diff_k_s.append(ks)
            diff_v_s.append(vs)
        mkp, mvp = mem_kv(mem_prompt, w_mem_kv[l])
        mem_k_p.append(mkp)
        mem_v_p.append(mvp)
        cross_p = mem_attend(qmp, mkp, mvp)
        cross_s = mem_attend(qms, cache_mem_k[l], cache_mem_v[l])
        xp = deepnorm(xp, jnp.concatenate([self_p, cross_p], axis=-1) @ w_o[l], ln1_g[l], ln1_b[l])
        xs = deepnorm(xs, jnp.concatenate([self_s, cross_s], axis=-1) @ w_o[l], ln1_g[l], ln1_b[l])
        fp = moe_ffn(xp.reshape(B * S, D_MODEL), w_router, router_bias, w_gate_up[l], w_down[l])
        fs = moe_ffn(xs.reshape(DB * T, D_MODEL), w_router, router_bias, w_gate_up[l], w_down[l])
        xp = deepnorm(xp, fp.reshape(B, S, D_MODEL), ln2_g[l], ln2_b[l])
        xs = deepnorm(xs, fs.reshape(DB, T, D_MODEL), ln2_g[l], ln2_b[l])
    return (xp, xs,
            jnp.stack(win_k_p), jnp.stack(win_v_p), jnp.stack(win_k_s), jnp.stack(win_v_s),
            jnp.stack(diff_k_p), jnp.stack(diff_v_p), jnp.stack(diff_k_s), jnp.stack(diff_v_s),
            jnp.stack(mem_k_p), jnp.stack(mem_v_p))
```

```python
import functools
import math

import jax
import jax.numpy as jnp
from jax import lax
from jax.experimental import pallas as pl
from jax.experimental.pallas import tpu as pltpu

f32 = jnp.float32
bf16 = jnp.bfloat16
i32 = jnp.int32

D_MODEL = 2048
DEPTH = 4
PAST_LEN = 16384
PAGE_SIZE = 128
HEAD_DIM = 128
N_HEADS = 12
KV_HEADS = 4
REP = N_HEADS // KV_HEADS
DIFF_DIM = HEAD_DIM // 2
MEM_HEADS = 4
MEM_LEN = 256
WINDOW = 128
NUM_BUCKETS = 32
MAX_DISTANCE = 128
N_EXPERTS = 16
N_GROUPS = 4
EXPERTS_PER_GROUP = N_EXPERTS // N_GROUPS
TOP_K = 2
D_EXPERT = D_MODEL // 2
Q_W = N_HEADS * HEAD_DIM
KV_W = KV_HEADS * HEAD_DIM
MEM_W = MEM_HEADS * HEAD_DIM
IN_W = Q_W + 2 * KV_W + MEM_W
ALPHA = (2.0 * DEPTH) ** 0.25
LN_EPS = 1e-5
RMS_EPS = 1e-5
NEG_INF = -1e30

VMEM_LIMIT_BYTES = 56 * 1024 * 1024

K_COL = Q_W // HEAD_DIM
V_COL = (Q_W + KV_W) // HEAD_DIM
QM_COL = (Q_W + 2 * KV_W) // HEAD_DIM

DIFF_TQ = 256
DIFF_TK = 256
PAGES_PER_STEP = 8
MOE_TM = 512


def _params(n_axes):
    return pltpu.CompilerParams(dimension_semantics=("arbitrary",) * n_axes,
                                vmem_limit_bytes=VMEM_LIMIT_BYTES)


def _dot_nt(a, b):
    return lax.dot_general(a, b, (((1,), (1,)), ((), ())), preferred_element_type=f32)


def _dot(a, b):
    return jnp.dot(a, b, preferred_element_type=f32)


def _mm_kernel(a_ref, b_ref, o_ref):
    o_ref[...] = _dot(a_ref[...], b_ref[...])


def matmul(a, b, tm, tn):
    M, K = a.shape
    N = b.shape[1]
    return pl.pallas_call(
        _mm_kernel,
        grid=(M // tm, N // tn),
        in_specs=[pl.BlockSpec((tm, K), lambda i, j: (i, 0)),
                  pl.BlockSpec((K, tn), lambda i, j: (0, j))],
        out_specs=pl.BlockSpec((tm, tn), lambda i, j: (i, j)),
        out_shape=jax.ShapeDtypeStruct((M, N), f32),
        compiler_params=_params(2),
        name="matmul",
    )(a, b)


def _layer_norm(y, g, b):
    mu = jnp.mean(y, -1, keepdims=True)
    yc = y - mu
    var = jnp.mean(yc * yc, -1, keepdims=True)
    return yc * lax.rsqrt(var + LN_EPS) * g + b


def _wo_ln_kernel(ms_ref, mc_ref, x_ref, ws_ref, wc_ref, g_ref, b_ref, o_ref, ob_ref):
    f = _dot(ms_ref[...], ws_ref[...]) + _dot(mc_ref[...], wc_ref[...])
    out = _layer_norm(ALPHA * x_ref[...] + f, g_ref[...], b_ref[...])
    o_ref[...] = out
    ob_ref[...] = out.astype(bf16)


def wo_ln(mix_self, mix_cross, x, w_self, w_cross, g, b, tm):
    M = x.shape[0]
    row = lambda i: (i, 0)
    const = lambda i: (0, 0)
    return pl.pallas_call(
        _wo_ln_kernel,
        grid=(M // tm,),
        in_specs=[pl.BlockSpec((tm, Q_W), row), pl.BlockSpec((tm, MEM_W), row),
                  pl.BlockSpec((tm, D_MODEL), row),
                  pl.BlockSpec((Q_W, D_MODEL), const), pl.BlockSpec((MEM_W, D_MODEL), const),
                  pl.BlockSpec((1, D_MODEL), const), pl.BlockSpec((1, D_MODEL), const)],
        out_specs=[pl.BlockSpec((tm, D_MODEL), row), pl.BlockSpec((tm, D_MODEL), row)],
        out_shape=[jax.ShapeDtypeStruct((M, D_MODEL), f32),
                   jax.ShapeDtypeStruct((M, D_MODEL), bf16)],
        compiler_params=_params(1),
        name="wo_ln",
    )(mix_self, mix_cross, x, w_self, w_cross, g, b)


def _combine_ln_kernel(x_ref, y0_ref, y1_ref, gate_ref, g_ref, b_ref, o_ref, ob_ref):
    gate = gate_ref[...]
    f = y0_ref[...] * gate[:, 0:1] + y1_ref[...] * gate[:, 1:2]
    out = _layer_norm(ALPHA * x_ref[...] + f, g_ref[...], b_ref[...])
    o_ref[...] = out
    ob_ref[...] = out.astype(bf16)


def combine_ln(x, y0, y1, gate_t, g, b, tm):
    M = x.shape[0]
    row = lambda i: (i, 0)
    const = lambda i: (0, 0)
    return pl.pallas_call(
        _combine_ln_kernel,
        grid=(M // tm,),
        in_specs=[pl.BlockSpec((tm, D_MODEL), row), pl.BlockSpec((tm, D_MODEL), row),
                  pl.BlockSpec((tm, D_MODEL), row), pl.BlockSpec((tm, TOP_K), row),
                  pl.BlockSpec((1, D_MODEL), const), pl.BlockSpec((1, D_MODEL), const)],
        out_specs=[pl.BlockSpec((tm, D_MODEL), row), pl.BlockSpec((tm, D_MODEL), row)],
        out_shape=[jax.ShapeDtypeStruct((M, D_MODEL), f32),
                   jax.ShapeDtypeStruct((M, D_MODEL), bf16)],
        compiler_params=_params(1),
        name="combine_ln",
    )(x, y0, y1, gate_t, g, b)


def _router_kernel(x_ref, wt_ref, bias_ref, e_ref, gate_ref):
    logits = lax.dot_general(wt_ref[...], x_ref[...], (((1,), (1,)), ((), ())),
                             preferred_element_type=f32, precision=lax.Precision.HIGHEST)
    scores = jax.nn.sigmoid(logits)
    sel = scores + bias_ref[...][:, 0:1]
    n = EXPERTS_PER_GROUP
    rows = [sel[e:e + 1, :] for e in range(N_EXPERTS)]
    srows = [scores[e:e + 1, :] for e in range(N_EXPERTS)]
    gscore = []
    for g in range(N_GROUPS):
        v = rows[g * n:(g + 1) * n]
        best = None
        for a in range(n):
            for b in range(a + 1, n):
                pair = v[a] + v[b]
                best = pair if best is None else jnp.maximum(best, pair)
        gscore.append(best)
    g_idx = jnp.zeros_like(gscore[0], dtype=i32)
    best = gscore[0]
    for g in range(1, N_GROUPS):
        take = gscore[g] > best
        g_idx = jnp.where(take, g, g_idx)
        best = jnp.where(take, gscore[g], best)
    ing, sg = [], []
    for a in range(n):
        va, sa = rows[a], srows[a]
        for g in range(1, N_GROUPS):
            va = jnp.where(g_idx == g, rows[g * n + a], va)
            sa = jnp.where(g_idx == g, srows[g * n + a], sa)
        ing.append(va)
        sg.append(sa)
    l_idx = [jnp.zeros_like(g_idx), jnp.zeros_like(g_idx)]
    w = [jnp.zeros_like(best), jnp.zeros_like(best)]
    for a in range(n):
        rank = jnp.zeros_like(g_idx)
        for b in range(n):
            if b == a:
                continue
            ahead = (ing[b] > ing[a]) | ((ing[b] == ing[a]) & (b < a))
            rank = rank + ahead.astype(i32)
        for k in range(TOP_K):
            hit = rank == k
            l_idx[k] = jnp.where(hit, a, l_idx[k])
            w[k] = jnp.where(hit, sg[a], w[k])
    tot = w[0] + w[1]
    e_ref[...] = jnp.concatenate([g_idx * n + l_idx[0], g_idx * n + l_idx[1]], axis=0)
    gate_ref[...] = jnp.concatenate([w[0] / tot, w[1] / tot], axis=0)


def router(x, w_router_t, router_bias, tm):
    M = x.shape[0]
    return pl.pallas_call(
        _router_kernel,
        grid=(M // tm,),
        in_specs=[pl.BlockSpec((tm, D_MODEL), lambda i: (i, 0)),
                  pl.BlockSpec((N_EXPERTS, D_MODEL), lambda i: (0, 0)),
                  pl.BlockSpec((N_EXPERTS, 128), lambda i: (0, 0))],
        out_specs=[pl.BlockSpec((TOP_K, tm), lambda i: (0, i)),
                   pl.BlockSpec((TOP_K, tm), lambda i: (0, i))],
        out_shape=[jax.ShapeDtypeStruct((TOP_K, M), i32),
                   jax.ShapeDtypeStruct((TOP_K, M), f32)],
        compiler_params=_params(1),
        name="router",
    )(x, w_router_t, router_bias)


def _moe_up_kernel(be_ref, nu_ref, x_ref, wg_ref, wu_ref, h_ref):
    @pl.when(pl.program_id(1) < nu_ref[0])
    def _():
        x = x_ref[...]
        g = _dot(x, wg_ref[0, 0].astype(bf16))
        u = _dot(x, wu_ref[0, 0].astype(bf16))
        h_ref[...] = (g * jax.nn.sigmoid(g) * u).astype(bf16)


def _moe_down_kernel(be_ref, nu_ref, h_ref, wd_ref, y_ref):
    @pl.when(pl.program_id(1) < nu_ref[0])
    def _():
        y_ref[...] = _dot(h_ref[...], wd_ref[0, 0].astype(bf16))


def moe_ffn_sorted(xs, blk_e, n_used, w_gu, w_dn, l, tn_up=512, tn_dn=1024):
    R = xs.shape[0]
    n_blocks = R // MOE_TM
    nj = D_EXPERT // tn_up
    blk = lambda j, i, be, nu: jnp.minimum(i, nu[0] - 1)
    h = pl.pallas_call(
        _moe_up_kernel,
        grid_spec=pltpu.PrefetchScalarGridSpec(
            num_scalar_prefetch=2,
            grid=(nj, n_blocks),
            in_specs=[pl.BlockSpec((MOE_TM, D_MODEL), lambda j, i, be, nu: (blk(j, i, be, nu), 0)),
                      pl.BlockSpec((1, 1, D_MODEL, tn_up), lambda j, i, be, nu: (l, be[i], 0, j)),
                      pl.BlockSpec((1, 1, D_MODEL, tn_up), lambda j, i, be, nu: (l, be[i], 0, nj + j))],
            out_specs=pl.BlockSpec((MOE_TM, tn_up), lambda j, i, be, nu: (blk(j, i, be, nu), j)),
        ),
        out_shape=jax.ShapeDtypeStruct((R, D_EXPERT), bf16),
        compiler_params=_params(2),
        name="moe_up",
    )(blk_e, n_used, xs, w_gu, w_gu)
    nj2 = D_MODEL // tn_dn
    return pl.pallas_call(
        _moe_down_kernel,
        grid_spec=pltpu.PrefetchScalarGridSpec(
            num_scalar_prefetch=2,
            grid=(nj2, n_blocks),
            in_specs=[pl.BlockSpec((MOE_TM, D_EXPERT), lambda j, i, be, nu: (blk(j, i, be, nu), 0)),
                      pl.BlockSpec((1, 1, D_EXPERT, tn_dn), lambda j, i, be, nu: (l, be[i], 0, j))],
            out_specs=pl.BlockSpec((MOE_TM, tn_dn), lambda j, i, be, nu: (blk(j, i, be, nu), j)),
        ),
        out_shape=jax.ShapeDtypeStruct((R, D_MODEL), f32),
        compiler_params=_params(2),
        name="moe_down",
    )(blk_e, n_used, h, w_dn)


def moe_dispatch(e_idx):
    T = e_idx.shape[1]
    A = T * TOP_K
    flat_e = e_idx.T.reshape(A)
    onehot = (flat_e[:, None] == jnp.arange(N_EXPERTS, dtype=i32)[None, :]).astype(i32)
    csum = jnp.cumsum(onehot, axis=0)
    rank = jnp.take_along_axis(csum, flat_e[:, None], axis=1)[:, 0] - 1
    counts = csum[-1]
    padded = (counts + MOE_TM - 1) // MOE_TM * MOE_TM
    pad_ends = jnp.cumsum(padded)
    pad_starts = pad_ends - padded
    dest = pad_starts[flat_e] + rank
    n_blocks = -(-A // MOE_TM) + N_EXPERTS
    tok = jnp.arange(A, dtype=i32) // TOP_K
    row_tok = jnp.zeros((n_blocks * MOE_TM,), i32).at[dest].set(tok)
    n_used = (pad_ends[-1] // MOE_TM).astype(i32)
    blk_start = jnp.minimum(jnp.arange(n_blocks, dtype=i32), n_used - 1) * MOE_TM
    blk_e = jnp.minimum(jnp.searchsorted(pad_ends, blk_start, side='right'), N_EXPERTS - 1).astype(i32)
    return dest.reshape(T, TOP_K), row_tok, blk_e, n_used.reshape(1)


def _t5_bucket(rel):
    n = jnp.maximum(-rel, 0)
    max_exact = NUM_BUCKETS // 2
    nf = jnp.maximum(n, 1).astype(f32)
    large = max_exact + (jnp.log(nf / max_exact) / math.log(MAX_DISTANCE / max_exact)
                         * (NUM_BUCKETS - max_exact)).astype(i32)
    large = jnp.minimum(large, NUM_BUCKETS - 1)
    return jnp.where(n < max_exact, n, large)


def _bias_tile(table, rel, mask, shift=None):
    b = jnp.transpose(table[_t5_bucket(rel)], (2, 0, 1)).astype(f32)
    if shift is not None:
        b = b - shift[:, None, None]
    return jnp.where(mask[None], b, NEG_INF)


def _group_rows(b, comps):
    H, Tq, Tk = b.shape
    b = b.reshape(KV_HEADS, REP, 1, Tq, Tk)
    b = jnp.broadcast_to(b, (KV_HEADS, REP, comps, Tq, Tk))
    return b.reshape(KV_HEADS, REP * comps * Tq, Tk)


def _mem_attn_kernel(q_ref, k_ref, v_ref, o_ref):
    scale = HEAD_DIM ** -0.5
    k_all = k_ref[...].reshape(MEM_LEN, MEM_W)
    v_all = v_ref[...].reshape(MEM_LEN, MEM_W)
    for h in range(MEM_HEADS):
        sl = slice(h * HEAD_DIM, (h + 1) * HEAD_DIM)
        s = _dot_nt(q_ref[:, sl].astype(bf16), k_all[:, sl].astype(bf16)) * scale
        m = jnp.max(s, -1, keepdims=True)
        e = jnp.exp(s - m)
        p = e * (1.0 / jnp.sum(e, -1, keepdims=True))
        o_ref[:, sl] = _dot(p.astype(bf16), v_all[:, sl].astype(bf16)).astype(o_ref.dtype)


def mem_attend(p, row_block0, n_seq, tq, tiles_per_seq, mem_k, mem_v, k_map, v_map, kv_block):
    rows = n_seq * tiles_per_seq * tq
    return pl.pallas_call(
        _mem_attn_kernel,
        grid=(n_seq, tiles_per_seq),
        in_specs=[pl.BlockSpec((tq, MEM_W), lambda n, i: (row_block0 + n * tiles_per_seq + i, QM_COL // MEM_HEADS)),
                  pl.BlockSpec(kv_block, k_map), pl.BlockSpec(kv_block, v_map)],
        out_specs=pl.BlockSpec((tq, MEM_W), lambda n, i: (n * tiles_per_seq + i, 0)),
        out_shape=jax.ShapeDtypeStruct((rows, MEM_W), f32 if tq < 16 else bf16),
        compiler_params=_params(2),
        name="mem_attn",
    )(p, mem_k, mem_v)


def _softmax_sink_pv(s, sink_col, v):
    m = jnp.maximum(jnp.max(s, -1, keepdims=True), sink_col)
    e = jnp.exp(s - m)
    den = jnp.sum(e, -1, keepdims=True) + jnp.exp(sink_col - m)
    p = e * (1.0 / den)
    return _dot(p.astype(bf16), v)


def _sink_col(sink_ref, g, tq):
    return jnp.concatenate([jnp.full((tq, 1), sink_ref[REP * g + r], f32) for r in range(REP)], axis=0)


def _swa_prompt_kernel(q_ref, kc_ref, kp_ref, vc_ref, vp_ref, bias_ref, sink_ref, o_ref):
    i = pl.program_id(1)
    scale = HEAD_DIM ** -0.5
    for g in range(KV_HEADS):
        sl = slice(g * HEAD_DIM, (g + 1) * HEAD_DIM)
        kk = jnp.concatenate([kp_ref[:, sl], kc_ref[:, sl]], axis=0).astype(bf16)
        vv = jnp.concatenate([vp_ref[:, sl], vc_ref[:, sl]], axis=0).astype(bf16)
        q3 = jnp.concatenate([q_ref[:, (REP * g + r) * HEAD_DIM:(REP * g + r + 1) * HEAD_DIM]
                              for r in range(REP)], axis=0).astype(bf16)
        s = _dot_nt(q3, kk) * scale + bias_ref[g]
        col = lax.broadcasted_iota(i32, s.shape, 1)
        s = jnp.where((col >= WINDOW) | (i > 0), s, NEG_INF)
        o = _softmax_sink_pv(s, _sink_col(sink_ref, g, WINDOW), vv)
        for r in range(REP):
            o_ref[:, (REP * g + r) * HEAD_DIM:(REP * g + r + 1) * HEAD_DIM] = (
                o[r * WINDOW:(r + 1) * WINDOW].astype(o_ref.dtype))


def swa_prompt(p, B, S, bias, sink):
    nb = S // WINDOW
    cur = lambda c: (lambda b, i: (b * nb + i, c))
    prev = lambda c: (lambda b, i: (b * nb + jnp.maximum(i - 1, 0), c))
    kc, vc = K_COL // KV_HEADS, V_COL // KV_HEADS
    return pl.pallas_call(
        _swa_prompt_kernel,
        grid=(B, nb),
        in_specs=[pl.BlockSpec((WINDOW, Q_W), cur(0)),
                  pl.BlockSpec((WINDOW, KV_W), cur(kc)), pl.BlockSpec((WINDOW, KV_W), prev(kc)),
                  pl.BlockSpec((WINDOW, KV_W), cur(vc)), pl.BlockSpec((WINDOW, KV_W), prev(vc)),
                  pl.BlockSpec((KV_HEADS, REP * WINDOW, 2 * WINDOW), lambda b, i: (0, 0, 0)),
                  pl.BlockSpec(memory_space=pltpu.SMEM)],
        out_specs=pl.BlockSpec((WINDOW, Q_W), lambda b, i: (b * nb + i, 0)),
        out_shape=jax.ShapeDtypeStruct((B * S, Q_W), bf16),
        compiler_params=_params(2),
        name="swa_prompt",
    )(p, p, p, p, p, bias, sink)


def _swa_sample_kernel(q_ref, kn_ref, vn_ref, wk_ref, wv_ref, bias_ref, sink_ref, o_ref, nk_ref, nv_ref, *, T):
    scale = HEAD_DIM ** -0.5
    wk = wk_ref[...].reshape(WINDOW, KV_W)
    wv = wv_ref[...].reshape(WINDOW, KV_W)
    kn = kn_ref[...]
    vn = vn_ref[...]
    pad = jnp.zeros((WINDOW - T, HEAD_DIM), f32)
    for g in range(KV_HEADS):
        sl = slice(g * HEAD_DIM, (g + 1) * HEAD_DIM)
        kk = jnp.concatenate([wk[:, sl], kn[:, sl], pad], axis=0).astype(bf16)
        vv = jnp.concatenate([wv[:, sl], vn[:, sl], pad], axis=0).astype(bf16)
        q3 = jnp.concatenate([q_ref[:, (REP * g + r) * HEAD_DIM:(REP * g + r + 1) * HEAD_DIM]
                              for r in range(REP)], axis=0).astype(bf16)
        s = _dot_nt(q3, kk) * scale + bias_ref[g]
        o = _softmax_sink_pv(s, _sink_col(sink_ref, g, T), vv)
        for r in range(REP):
            o_ref[:, (REP * g + r) * HEAD_DIM:(REP * g + r + 1) * HEAD_DIM] = o[r * T:(r + 1) * T]
    nk_ref[0, 0:WINDOW - T, :] = wk[T:, :]
    nk_ref[0, WINDOW - T:, :] = kn
    nv_ref[0, 0:WINDOW - T, :] = wv[T:, :]
    nv_ref[0, WINDOW - T:, :] = vn


def swa_sample(p, row_block0, DB, T, win_k, win_v, li, bias, sink):
    kc, vc = K_COL // KV_HEADS, V_COL // KV_HEADS
    win = lambda n: (li, n, 0, 0)
    return pl.pallas_call(
        functools.partial(_swa_sample_kernel, T=T),
        grid=(DB,),
        in_specs=[pl.BlockSpec((T, Q_W), lambda n: (row_block0 + n, 0)),
                  pl.BlockSpec((T, KV_W), lambda n: (row_block0 + n, kc)),
                  pl.BlockSpec((T, KV_W), lambda n: (row_block0 + n, vc)),
                  pl.BlockSpec((1, 1, WINDOW, KV_W), win), pl.BlockSpec((1, 1, WINDOW, KV_W), win),
                  pl.BlockSpec((KV_HEADS, REP * T, 2 * WINDOW), lambda n: (0, 0, 0)),
                  pl.BlockSpec(memory_space=pltpu.SMEM)],
        out_specs=[pl.BlockSpec((T, Q_W), lambda n: (n, 0)),
                   pl.BlockSpec((1, WINDOW, KV_W), lambda n: (n, 0, 0)),
                   pl.BlockSpec((1, WINDOW, KV_W), lambda n: (n, 0, 0))],
        out_shape=[jax.ShapeDtypeStruct((DB * T, Q_W), f32),
                   jax.ShapeDtypeStruct((DB, WINDOW, KV_W), f32),
                   jax.ShapeDtypeStruct((DB, WINDOW, KV_W), f32)],
        compiler_params=_params(1),
        name="swa_sample",
    )(p, p, p, win_k, win_v, bias, sink)


def _split_components(q):
    lane = lax.broadcasted_iota(i32, q.shape, 1)
    qs = q * (DIFF_DIM ** -0.5)
    return [jnp.where(lane < DIFF_DIM, qs, 0.0), jnp.where(lane >= DIFF_DIM, qs, 0.0)]


def _diff_finish(o0, o1, lam, gain, lam_init):
    o = o0 - lam * o1
    return o * lax.rsqrt(jnp.mean(o * o, -1, keepdims=True) + RMS_EPS) * gain * (1.0 - lam_init)


def _diff_prompt_kernel(q_ref, k_ref, v_ref, bias_ref, lam_ref, gain_ref, o_ref, qp_scr, m_scr, acc_scr,
                        *, lam_init):
    i = pl.program_id(2)
    tq, tk = DIFF_TQ, DIFF_TK
    pieces = []
    for r in range(REP):
        pieces += _split_components(q_ref[:, r * HEAD_DIM:(r + 1) * HEAD_DIM])
    qp_scr[...] = jnp.concatenate(pieces, axis=0).astype(bf16)
    m_scr[...] = jnp.full(m_scr.shape, NEG_INF, f32)
    acc_scr[...] = jnp.zeros(acc_scr.shape, f32)
    ones_col = (lax.broadcasted_iota(i32, (tk, HEAD_DIM), 1) == 0).astype(bf16)

    def chunk(k0, bias):
        kc = k_ref[pl.ds(k0, tk), :].astype(bf16)
        vc = jnp.concatenate([v_ref[pl.ds(k0, tk), :].astype(bf16), ones_col], axis=1)
        s = _dot_nt(qp_scr[...], kc)
        if bias is not None:
            s = s + bias
        m_old = m_scr[...]
        m_new = jnp.maximum(m_old, jnp.max(s, -1, keepdims=True))
        p = jnp.exp(s - m_new).astype(bf16)
        acc_scr[...] = jnp.exp(m_old - m_new) * acc_scr[...] + _dot(p, vc)
        m_scr[...] = m_new

    q0 = pl.multiple_of(i * tq, tq)
    chunk(q0, bias_ref[0, :, tq:])

    @pl.when(i > 0)
    def _():
        chunk(pl.multiple_of(q0 - tq, tq), bias_ref[0, :, :tq])

    def far(j, carry):
        chunk(pl.multiple_of(j * tk, tk), None)
        return carry

    lax.fori_loop(0, jnp.maximum(i - 1, 0) * (tq // tk), far, 0)

    acc = acc_scr[...]
    lam = lam_ref[0]
    for r in range(REP):
        a0 = acc[(2 * r) * tq:(2 * r + 1) * tq]
        a1 = acc[(2 * r + 1) * tq:(2 * r + 2) * tq]
        o0 = a0[:, :HEAD_DIM] * (1.0 / a0[:, HEAD_DIM:HEAD_DIM + 1])
        o1 = a1[:, :HEAD_DIM] * (1.0 / a1[:, HEAD_DIM:HEAD_DIM + 1])
        o = _diff_finish(o0, o1, lam, gain_ref[...], lam_init)
        o_ref[:, r * HEAD_DIM:(r + 1) * HEAD_DIM] = o.astype(o_ref.dtype)


def diff_prompt(p, B, S, bias, lam, gain, lam_init):
    nq = S // DIFF_TQ
    rows = 2 * REP * DIFF_TQ
    return pl.pallas_call(
        functools.partial(_diff_prompt_kernel, lam_init=lam_init),
        grid=(B, KV_HEADS, nq),
        in_specs=[pl.BlockSpec((DIFF_TQ, REP * HEAD_DIM), lambda b, g, i: (b * nq + i, g)),
                  pl.BlockSpec((S, HEAD_DIM), lambda b, g, i: (b, K_COL + g)),
                  pl.BlockSpec((S, HEAD_DIM), lambda b, g, i: (b, V_COL + g)),
                  pl.BlockSpec((1, rows, 2 * DIFF_TQ), lambda b, g, i: (g, 0, 0)),
                  pl.BlockSpec(memory_space=pltpu.SMEM),
                  pl.BlockSpec((1, HEAD_DIM), lambda b, g, i: (0, 0))],
        out_specs=pl.BlockSpec((DIFF_TQ, REP * HEAD_DIM), lambda b, g, i: (b * nq + i, g)),
        out_shape=jax.ShapeDtypeStruct((B * S, Q_W), bf16),
        scratch_shapes=[pltpu.VMEM((rows, HEAD_DIM), bf16),
                        pltpu.VMEM((rows, 1), f32),
                        pltpu.VMEM((rows, 2 * HEAD_DIM), f32)],
        compiler_params=_params(3),
        name="diff_prompt",
    )(p, p, p, bias, lam, gain)


def _diff_sample_kernel(pt_ref, q_ref, kn_ref, vn_ref, *rest, T, lam_init, n_steps):
    npg = PAGES_PER_STEP
    k_refs, v_refs = rest[:npg], rest[npg:2 * npg]
    bias_last_ref, bias_new_ref, lam_ref, gain_ref, o_ref, qp_scr, m_scr, l_scr, acc_scr = rest[2 * npg:]
    s_id = pl.program_id(1)
    rows = 2 * REP * T

    @pl.when(s_id == 0)
    def _():
        for g in range(KV_HEADS):
            pieces = []
            for r in range(REP):
                pieces += _split_components(q_ref[:, (REP * g + r) * HEAD_DIM:(REP * g + r + 1) * HEAD_DIM])
            qp_scr[g] = jnp.concatenate(pieces, axis=0)
        m_scr[...] = jnp.full(m_scr.shape, NEG_INF, f32)
        l_scr[...] = jnp.zeros(l_scr.shape, f32)
        acc_scr[...] = jnp.zeros(acc_scr.shape, f32)

    def update(g, s, v):
        m_old = m_scr[g]
        m_new = jnp.maximum(m_old, jnp.max(s, -1, keepdims=True))
        p = jnp.exp(s - m_new)
        alpha = jnp.exp(m_old - m_new)
        l_scr[g] = alpha * l_scr[g] + jnp.sum(p, -1, keepdims=True)
        acc_scr[g] = alpha * acc_scr[g] + _dot(p.astype(bf16), v)
        m_scr[g] = m_new

    last = s_id == n_steps - 1
    for g in range(KV_HEADS):
        kg = jnp.concatenate([k_refs[j][0, 0, pl.ds(g, PAGE_SIZE, stride=KV_HEADS), :] for j in range(npg)],
                             axis=0).astype(bf16)
        vg = jnp.concatenate([v_refs[j][0, 0, pl.ds(g, PAGE_SIZE, stride=KV_HEADS), :] for j in range(npg)],
                             axis=0).astype(bf16)
        qg = qp_scr[g].astype(bf16)
        s = _dot_nt(qg, kg)
        tail = s[:, (npg - 1) * PAGE_SIZE:] + jnp.where(last, bias_last_ref[g], 0.0)
        s = jnp.concatenate([s[:, :(npg - 1) * PAGE_SIZE], tail], axis=1)
        update(g, s, vg)

    @pl.when(last)
    def _():
        lam = lam_ref[0]
        pad = jnp.zeros((PAGE_SIZE - T, HEAD_DIM), f32)
        for g in range(KV_HEADS):
            sl = slice(g * HEAD_DIM, (g + 1) * HEAD_DIM)
            kn = jnp.concatenate([kn_ref[:, sl], pad], axis=0).astype(bf16)
            vn = jnp.concatenate([vn_ref[:, sl], pad], axis=0).astype(bf16)
            update(g, _dot_nt(qp_scr[g].astype(bf16), kn) + bias_new_ref[g], vn)
            acc = acc_scr[g] * (1.0 / l_scr[g])
            for r in range(REP):
                o = _diff_finish(acc[(2 * r) * T:(2 * r + 1) * T], acc[(2 * r + 1) * T:(2 * r + 2) * T],
                                 lam, gain_ref[...], lam_init)
                o_ref[:, (REP * g + r) * HEAD_DIM:(REP * g + r + 1) * HEAD_DIM] = o


def diff_sample(p, row_block0, DB, T, cache_k, cache_v, li, page_table, bias_last, bias_new, lam, gain, lam_init):
    n_pages = PAST_LEN // PAGE_SIZE
    npg = PAGES_PER_STEP
    n_steps = n_pages // npg
    rows = 2 * REP * T
    kc, vc = K_COL // KV_HEADS, V_COL // KV_HEADS

    def page(j):
        return lambda n, s, pt: (li, pt[n * n_pages + s * npg + j], 0, 0)

    page_block = (1, 1, PAGE_SIZE * KV_HEADS, HEAD_DIM)
    const3 = lambda n, s, pt: (0, 0, 0)
    in_specs = ([pl.BlockSpec((T, Q_W), lambda n, s, pt: (row_block0 + n, 0)),
                 pl.BlockSpec((T, KV_W), lambda n, s, pt: (row_block0 + n, kc)),
                 pl.BlockSpec((T, KV_W), lambda n, s, pt: (row_block0 + n, vc))]
                + [pl.BlockSpec(page_block, page(j)) for j in range(npg)]
                + [pl.BlockSpec(page_block, page(j)) for j in range(npg)]
                + [pl.BlockSpec((KV_HEADS, rows, PAGE_SIZE), const3),
                   pl.BlockSpec((KV_HEADS, rows, PAGE_SIZE), const3),
                   pl.BlockSpec(memory_space=pltpu.SMEM),
                   pl.BlockSpec((1, HEAD_DIM), lambda n, s, pt: (0, 0))])
    return pl.pallas_call(
        functools.partial(_diff_sample_kernel, T=T, lam_init=lam_init, n_steps=n_steps),
        grid_spec=pltpu.PrefetchScalarGridSpec(
            num_scalar_prefetch=1,
            grid=(DB, n_steps),
            in_specs=in_specs,
            out_specs=pl.BlockSpec((T, Q_W), lambda n, s, pt: (n, 0)),
            scratch_shapes=[pltpu.VMEM((KV_HEADS, rows, HEAD_DIM), f32),
                            pltpu.VMEM((KV_HEADS, rows, 1), f32),
                            pltpu.VMEM((KV_HEADS, rows, 1), f32),
                            pltpu.VMEM((KV_HEADS, rows, HEAD_DIM), f32)],
        ),
        out_shape=jax.ShapeDtypeStruct((DB * T, Q_W), f32),
        compiler_params=_params(2),
        name="diff_sample",
    )(page_table, p, p, p, *([cache_k] * npg), *([cache_v] * npg), bias_last, bias_new, lam, gain)


def _row_tile(total, candidates):
    for t in candidates:
        if total % t == 0:
            return t
    raise ValueError(f"no row tile for {total} rows")


def kernel(x_prompt, x_sample, cache_win_k, cache_win_v, cache_diff_k, cache_diff_v, cache_mem_k, cache_mem_v,
           page_table, mem_prompt, w_in, w_mem_kv, w_o, rel_bias, sinks, lam_q1, lam_k1, lam_q2, lam_k2, subln_g,
           ln1_g, ln1_b, ln2_g, ln2_b, w_router, router_bias, w_gate_up, w_down):
    B, S, _ = x_prompt.shape
    DB, T, _ = x_sample.shape
    TP, TS = B * S, DB * T
    TT = TP + TS
    assert S % DIFF_TQ == 0 and S % WINDOW == 0 and TP % T == 0 and T % 8 == 0
    n_swa, n_pool = cache_win_k.shape[0], cache_diff_k.shape[1]
    sample_block0 = TP // T

    x = jnp.concatenate([x_prompt.reshape(TP, D_MODEL), x_sample.reshape(TS, D_MODEL)], axis=0)
    xb = x.astype(bf16)
    w_in_b = w_in.astype(bf16)
    w_o_b = w_o.astype(bf16)
    w_mem_b = w_mem_kv.astype(bf16)
    mem_b = mem_prompt.reshape(B * MEM_LEN, D_MODEL).astype(bf16)
    w_router_t = w_router.T
    router_bias_b = jnp.broadcast_to(router_bias.astype(f32)[:, None], (N_EXPERTS, 128))
    win_k = cache_win_k.reshape(n_swa, DB, WINDOW, KV_W)
    win_v = cache_win_v.reshape(n_swa, DB, WINDOW, KV_W)
    pool_k = cache_diff_k.reshape(-1, n_pool, PAGE_SIZE * KV_HEADS, HEAD_DIM)
    pool_v = cache_diff_v.reshape(-1, n_pool, PAGE_SIZE * KV_HEADS, HEAD_DIM)
    cmem_k = cache_mem_k.reshape(DEPTH, DB, MEM_LEN, MEM_W)
    cmem_v = cache_mem_v.reshape(DEPTH, DB, MEM_LEN, MEM_W)
    pt_flat = page_table.reshape(-1).astype(i32)

    far = rel_bias[NUM_BUCKETS - 1].astype(f32)
    qo = jnp.arange(WINDOW)
    rel = (jnp.arange(2 * WINDOW) - WINDOW)[None, :] - qo[:, None]
    bias_swa_p = _group_rows(_bias_tile(rel_bias, rel, (rel <= 0) & (rel > -WINDOW)), 1)
    tt = jnp.arange(T)
    kpos = jnp.concatenate([jnp.arange(WINDOW) - WINDOW, tt, jnp.full((WINDOW - T,), T)])
    rel = kpos[None, :] - tt[:, None]
    bias_swa_s = _group_rows(_bias_tile(rel_bias, rel, (rel <= 0) & (rel > -WINDOW)), 1)
    qo = jnp.arange(DIFF_TQ)
    rel = (jnp.arange(2 * DIFF_TQ) - DIFF_TQ)[None, :] - qo[:, None]
    bias_diff_p = _group_rows(_bias_tile(rel_bias, rel, rel <= 0, far), 2)
    rel = (jnp.arange(PAGE_SIZE) - PAGE_SIZE)[None, :] - tt[:, None]
    bias_diff_last = _group_rows(_bias_tile(rel_bias, rel, rel <= 0, far), 2)
    kpos = jnp.concatenate([tt, jnp.full((PAGE_SIZE - T,), T)])
    rel = kpos[None, :] - tt[:, None]
    bias_diff_new = _group_rows(_bias_tile(rel_bias, rel, rel <= 0, far), 2)

    tm_mm = _row_tile(TT, (1056, 1024, 512, 256))
    tm_ln = _row_tile(TT, (352, 256, 128))
    tm_rt = _row_tile(TT, (1408, 1024, 512, 256, 128))
    tq_mem = _row_tile(S, (512, 256, 128))

    win_k_p, win_v_p, win_k_s, win_v_s = [], [], [], []
    diff_k_p, diff_v_p, diff_k_s, diff_v_s = [], [], [], []
    mem_k_p, mem_v_p = [], []
    for l in range(DEPTH):
        i = l // 2
        p = matmul(xb, w_in_b[l], tm_mm, 512)
        k_p = p[:TP, Q_W:Q_W + KV_W].reshape(B, S, KV_HEADS, HEAD_DIM)
        v_p = p[:TP, Q_W + KV_W:Q_W + 2 * KV_W].reshape(B, S, KV_HEADS, HEAD_DIM)
        if l % 2 == 0:
            sink = sinks[i].astype(f32)
            self_p = swa_prompt(p, B, S, bias_swa_p, sink)
            self_s, nk, nv = swa_sample(p, sample_block0, DB, T, win_k, win_v, i, bias_swa_s, sink)
            win_k_p.append(k_p[:, S - WINDOW:])
            win_v_p.append(v_p[:, S - WINDOW:])
            win_k_s.append(nk.reshape(DB, WINDOW, KV_HEADS, HEAD_DIM))
            win_v_s.append(nv.reshape(DB, WINDOW, KV_HEADS, HEAD_DIM))
        else:
            lam_init = 0.8 - 0.6 * math.exp(-0.3 * l)
            lam = (jnp.exp(jnp.sum(lam_q1[i].astype(f32) * lam_k1[i].astype(f32)))
                   - jnp.exp(jnp.sum(lam_q2[i].astype(f32) * lam_k2[i].astype(f32))) + lam_init).reshape(1)
            gain = subln_g[i].astype(f32).reshape(1, HEAD_DIM)
            self_p = diff_prompt(p, B, S, bias_diff_p, lam, gain, lam_init)
            self_s = diff_sample(p, sample_block0, DB, T, pool_k, pool_v, i, pt_flat,
                                 bias_diff_last, bias_diff_new, lam, gain, lam_init)
            diff_k_p.append(k_p)
            diff_v_p.append(v_p)
            diff_k_s.append(p[TP:, Q_W:Q_W + KV_W].reshape(DB, T, KV_HEADS, HEAD_DIM))
            diff_v_s.append(p[TP:, Q_W + KV_W:Q_W + 2 * KV_W].reshape(DB, T, KV_HEADS, HEAD_DIM))
        mkv = matmul(mem_b, w_mem_b[l], B * MEM_LEN, 512).reshape(B, MEM_LEN, 2 * MEM_W)
        mem_k_p.append(mkv[:, :, :MEM_W].reshape(B, MEM_LEN, MEM_HEADS, HEAD_DIM))
        mem_v_p.append(mkv[:, :, MEM_W:].reshape(B, MEM_LEN, MEM_HEADS, HEAD_DIM))
        cross_p = mem_attend(p, 0, B, tq_mem, S // tq_mem, mkv, mkv,
                             lambda n, t: (n, 0, 0), lambda n, t: (n, 0, 1), (1, MEM_LEN, MEM_W))
        cross_s = mem_attend(p, sample_block0, DB, T, 1, cmem_k, cmem_v,
                             lambda n, t: (l, n, 0, 0), lambda n, t: (l, n, 0, 0), (1, 1, MEM_LEN, MEM_W))
        mix_self = jnp.concatenate([self_p, self_s.astype(bf16)], axis=0)
        mix_cross = jnp.concatenate([cross_p, cross_s.astype(bf16)], axis=0)
        x, xb = wo_ln(mix_self, mix_cross, x, w_o_b[l, :Q_W], w_o_b[l, Q_W:],
                      ln1_g[l].reshape(1, D_MODEL), ln1_b[l].reshape(1, D_MODEL), tm_ln)
        e_idx, gate = router(x, w_router_t, router_bias_b, tm_rt)
        dest, row_tok, blk_e, n_used = moe_dispatch(e_idx)
        yb = moe_ffn_sorted(xb[row_tok], blk_e, n_used, w_gate_up, w_down, l)
        x, xb = combine_ln(x, yb[dest[:, 0]], yb[dest[:, 1]], gate.T,
                           ln2_g[l].reshape(1, D_MODEL), ln2_b[l].reshape(1, D_MODEL), tm_ln)
    return (x[:TP].reshape(B, S, D_MODEL), x[TP:].reshape(DB, T, D_MODEL),
            jnp.stack(win_k_p), jnp.stack(win_v_p), jnp.stack(win_k_s), jnp.stack(win_v_s),
            jnp.stack(diff_k_p), jnp.stack(diff_v_p), jnp.stack(diff_k_s), jnp.stack(diff_v_s),
            jnp.stack(mem_k_p), jnp.stack(mem_v_p))
```

```python
import functools
import math

import jax
import jax.numpy as jnp
from jax import lax
from jax.experimental import pallas as pl
from jax.experimental.pallas import tpu as pltpu

f32 = jnp.float32
bf16 = jnp.bfloat16
i32 = jnp.int32

D_MODEL = 2048
DEPTH = 4
PAST_LEN = 16384
PAGE_SIZE = 128
HEAD_DIM = 128
N_HEADS = 12
KV_HEADS = 4
REP = N_HEADS // KV_HEADS
DIFF_DIM = HEAD_DIM // 2
MEM_HEADS = 4
MEM_LEN = 256
WINDOW = 128
NUM_BUCKETS = 32
MAX_DISTANCE = 128
N_EXPERTS = 16
N_GROUPS = 4
EXPERTS_PER_GROUP = N_EXPERTS // N_GROUPS
TOP_K = 2
D_EXPERT = D_MODEL // 2
Q_W = N_HEADS * HEAD_DIM
KV_W = KV_HEADS * HEAD_DIM
MEM_W = MEM_HEADS * HEAD_DIM
IN_W = Q_W + 2 * KV_W + MEM_W
ALPHA = (2.0 * DEPTH) ** 0.25
LN_EPS = 1e-5
RMS_EPS = 1e-5
NEG_INF = -1e30
LOG2_E = math.log2(math.e)

VMEM_LIMIT_BYTES = 56 * 1024 * 1024

K_COL = Q_W // HEAD_DIM
V_COL = (Q_W + KV_W) // HEAD_DIM
QM_COL = (Q_W + 2 * KV_W) // HEAD_DIM

DIFF_TQ = 256
DIFF_TK = 256
PAGES_PER_STEP = 16
MOE_TM = 512


def _params(n_axes):
    return pltpu.CompilerParams(dimension_semantics=("arbitrary",) * n_axes,
                                vmem_limit_bytes=VMEM_LIMIT_BYTES)


def _dot_nt(a, b):
    return lax.dot_general(a, b, (((1,), (1,)), ((), ())), preferred_element_type=f32)


def _dot(a, b):
    return jnp.dot(a, b, preferred_element_type=f32)


def _mm_kernel(a_ref, b_ref, o_ref):
    o_ref[...] = _dot(a_ref[...], b_ref[...])


def matmul(a, b, tm, tn):
    M, K = a.shape
    N = b.shape[1]
    return pl.pallas_call(
        _mm_kernel,
        grid=(M // tm, N // tn),
        in_specs=[pl.BlockSpec((tm, K), lambda i, j: (i, 0)),
                  pl.BlockSpec((K, tn), lambda i, j: (0, j))],
        out_specs=pl.BlockSpec((tm, tn), lambda i, j: (i, j)),
        out_shape=jax.ShapeDtypeStruct((M, N), f32),
        compiler_params=_params(2),
        name="matmul",
    )(a, b)


def _layer_norm(y, g, b):
    mu = jnp.mean(y, -1, keepdims=True)
    yc = y - mu
    var = jnp.mean(yc * yc, -1, keepdims=True)
    return yc * lax.rsqrt(var + LN_EPS) * g + b


def _wo_ln_kernel(ms_ref, mc_ref, x_ref, ws_ref, wc_ref, g_ref, b_ref, o_ref, ob_ref):
    f = _dot(ms_ref[...], ws_ref[...]) + _dot(mc_ref[...], wc_ref[...])
    out = _layer_norm(ALPHA * x_ref[...] + f, g_ref[...], b_ref[...])
    o_ref[...] = out
    ob_ref[...] = out.astype(bf16)


def wo_ln(mix_self, mix_cross, x, w_self, w_cross, g, b, tm):
    M = x.shape[0]
    row = lambda i: (i, 0)
    const = lambda i: (0, 0)
    return pl.pallas_call(
        _wo_ln_kernel,
        grid=(M // tm,),
        in_specs=[pl.BlockSpec((tm, Q_W), row), pl.BlockSpec((tm, MEM_W), row),
                  pl.BlockSpec((tm, D_MODEL), row),
                  pl.BlockSpec((Q_W, D_MODEL), const), pl.BlockSpec((MEM_W, D_MODEL), const),
                  pl.BlockSpec((1, D_MODEL), const), pl.BlockSpec((1, D_MODEL), const)],
        out_specs=[pl.BlockSpec((tm, D_MODEL), row), pl.BlockSpec((tm, D_MODEL), row)],
        out_shape=[jax.ShapeDtypeStruct((M, D_MODEL), f32),
                   jax.ShapeDtypeStruct((M, D_MODEL), bf16)],
        compiler_params=_params(1),
        name="wo_ln",
    )(mix_self, mix_cross, x, w_self, w_cross, g, b)


def _combine_ln_kernel(x_ref, y0_ref, y1_ref, gate_ref, g_ref, b_ref, o_ref, ob_ref):
    gate = gate_ref[...]
    f = y0_ref[...] * gate[:, 0:1] + y1_ref[...] * gate[:, 1:2]
    out = _layer_norm(ALPHA * x_ref[...] + f, g_ref[...], b_ref[...])
    o_ref[...] = out
    ob_ref[...] = out.astype(bf16)


def combine_ln(x, y0, y1, gate_t, g, b, tm):
    M = x.shape[0]
    row = lambda i: (i, 0)
    const = lambda i: (0, 0)
    return pl.pallas_call(
        _combine_ln_kernel,
        grid=(M // tm,),
        in_specs=[pl.BlockSpec((tm, D_MODEL), row), pl.BlockSpec((tm, D_MODEL), row),
                  pl.BlockSpec((tm, D_MODEL), row), pl.BlockSpec((tm, TOP_K), row),
                  pl.BlockSpec((1, D_MODEL), const), pl.BlockSpec((1, D_MODEL), const)],
        out_specs=[pl.BlockSpec((tm, D_MODEL), row), pl.BlockSpec((tm, D_MODEL), row)],
        out_shape=[jax.ShapeDtypeStruct((M, D_MODEL), f32),
                   jax.ShapeDtypeStruct((M, D_MODEL), bf16)],
        compiler_params=_params(1),
        name="combine_ln",
    )(x, y0, y1, gate_t, g, b)


def _router_kernel(x_ref, wt_ref, bias_ref, e_ref, gate_ref):
    logits = lax.dot_general(wt_ref[...], x_ref[...], (((1,), (1,)), ((), ())),
                             preferred_element_type=f32, precision=lax.Precision.HIGHEST)
    scores = jax.nn.sigmoid(logits)
    sel = scores + bias_ref[...][:, 0:1]
    n = EXPERTS_PER_GROUP
    rows = [sel[e:e + 1, :] for e in range(N_EXPERTS)]
    srows = [scores[e:e + 1, :] for e in range(N_EXPERTS)]
    gscore = []
    for g in range(N_GROUPS):
        v = rows[g * n:(g + 1) * n]
        best = None
        for a in range(n):
            for b in range(a + 1, n):
                pair = v[a] + v[b]
                best = pair if best is None else jnp.maximum(best, pair)
        gscore.append(best)
    g_idx = jnp.zeros_like(gscore[0], dtype=i32)
    best = gscore[0]
    for g in range(1, N_GROUPS):
        take = gscore[g] > best
        g_idx = jnp.where(take, g, g_idx)
        best = jnp.where(take, gscore[g], best)
    ing, sg = [], []
    for a in range(n):
        va, sa = rows[a], srows[a]
        for g in range(1, N_GROUPS):
            va = jnp.where(g_idx == g, rows[g * n + a], va)
            sa = jnp.where(g_idx == g, srows[g * n + a], sa)
        ing.append(va)
        sg.append(sa)
    l_idx = [jnp.zeros_like(g_idx), jnp.zeros_like(g_idx)]
    w = [jnp.zeros_like(best), jnp.zeros_like(best)]
    for a in range(n):
        rank = jnp.zeros_like(g_idx)
        for b in range(n):
            if b == a:
                continue
            ahead = (ing[b] > ing[a]) | ((ing[b] == ing[a]) & (b < a))
            rank = rank + ahead.astype(i32)
        for k in range(TOP_K):
            hit = rank == k
            l_idx[k] = jnp.where(hit, a, l_idx[k])
            w[k] = jnp.where(hit, sg[a], w[k])
    tot = w[0] + w[1]
    e_ref[...] = jnp.concatenate([g_idx * n + l_idx[0], g_idx * n + l_idx[1]], axis=0)
    gate_ref[...] = jnp.concatenate([w[0] / tot, w[1] / tot], axis=0)


def router(x, w_router_t, router_bias, tm):
    M = x.shape[0]
    return pl.pallas_call(
        _router_kernel,
        grid=(M // tm,),
        in_specs=[pl.BlockSpec((tm, D_MODEL), lambda i: (i, 0)),
                  pl.BlockSpec((N_EXPERTS, D_MODEL), lambda i: (0, 0)),
                  pl.BlockSpec((N_EXPERTS, 128), lambda i: (0, 0))],
        out_specs=[pl.BlockSpec((TOP_K, tm), lambda i: (0, i)),
                   pl.BlockSpec((TOP_K, tm), lambda i: (0, i))],
        out_shape=[jax.ShapeDtypeStruct((TOP_K, M), i32),
                   jax.ShapeDtypeStruct((TOP_K, M), f32)],
        compiler_params=_params(1),
        name="router",
    )(x, w_router_t, router_bias)


def _moe_up_kernel(be_ref, nu_ref, x_ref, wg_ref, wu_ref, h_ref):
    @pl.when(pl.program_id(1) < nu_ref[0])
    def _():
        x = x_ref[...]
        g = _dot(x, wg_ref[0, 0].astype(bf16))
        u = _dot(x, wu_ref[0, 0].astype(bf16))
        h_ref[...] = (g * jax.nn.sigmoid(g) * u).astype(bf16)


def _moe_down_kernel(be_ref, nu_ref, h_ref, wd_ref, y_ref):
    @pl.when(pl.program_id(1) < nu_ref[0])
    def _():
        y_ref[...] = _dot(h_ref[...], wd_ref[0, 0].astype(bf16))


def moe_ffn_sorted(xs, blk_e, n_used, w_gu, w_dn, l, tn_up=512, tn_dn=1024):
    R = xs.shape[0]
    n_blocks = R // MOE_TM
    nj = D_EXPERT // tn_up
    blk = lambda j, i, be, nu: jnp.minimum(i, nu[0] - 1)
    h = pl.pallas_call(
        _moe_up_kernel,
        grid_spec=pltpu.PrefetchScalarGridSpec(
            num_scalar_prefetch=2,
            grid=(nj, n_blocks),
            in_specs=[pl.BlockSpec((MOE_TM, D_MODEL), lambda j, i, be, nu: (blk(j, i, be, nu), 0)),
                      pl.BlockSpec((1, 1, D_MODEL, tn_up), lambda j, i, be, nu: (l, be[i], 0, j)),
                      pl.BlockSpec((1, 1, D_MODEL, tn_up), lambda j, i, be, nu: (l, be[i], 0, nj + j))],
            out_specs=pl.BlockSpec((MOE_TM, tn_up), lambda j, i, be, nu: (blk(j, i, be, nu), j)),
        ),
        out_shape=jax.ShapeDtypeStruct((R, D_EXPERT), bf16),
        compiler_params=_params(2),
        name="moe_up",
    )(blk_e, n_used, xs, w_gu, w_gu)
    nj2 = D_MODEL // tn_dn
    return pl.pallas_call(
        _moe_down_kernel,
        grid_spec=pltpu.PrefetchScalarGridSpec(
            num_scalar_prefetch=2,
            grid=(nj2, n_blocks),
            in_specs=[pl.BlockSpec((MOE_TM, D_EXPERT), lambda j, i, be, nu: (blk(j, i, be, nu), 0)),
                      pl.BlockSpec((1, 1, D_EXPERT, tn_dn), lambda j, i, be, nu: (l, be[i], 0, j))],
            out_specs=pl.BlockSpec((MOE_TM, tn_dn), lambda j, i, be, nu: (blk(j, i, be, nu), j)),
        ),
        out_shape=jax.ShapeDtypeStruct((R, D_MODEL), f32),
        compiler_params=_params(2),
        name="moe_down",
    )(blk_e, n_used, h, w_dn)


def moe_dispatch(e_idx):
    T = e_idx.shape[1]
    A = T * TOP_K
    flat_e = e_idx.T.reshape(A)
    onehot = (flat_e[:, None] == jnp.arange(N_EXPERTS, dtype=i32)[None, :]).astype(i32)
    csum = jnp.cumsum(onehot, axis=0)
    rank = jnp.take_along_axis(csum, flat_e[:, None], axis=1)[:, 0] - 1
    counts = csum[-1]
    padded = (counts + MOE_TM - 1) // MOE_TM * MOE_TM
    pad_ends = jnp.cumsum(padded)
    pad_starts = pad_ends - padded
    dest = pad_starts[flat_e] + rank
    n_blocks = -(-A // MOE_TM) + N_EXPERTS
    tok = jnp.arange(A, dtype=i32) // TOP_K
    row_tok = jnp.zeros((n_blocks * MOE_TM,), i32).at[dest].set(tok)
    n_used = (pad_ends[-1] // MOE_TM).astype(i32)
    blk_start = jnp.minimum(jnp.arange(n_blocks, dtype=i32), n_used - 1) * MOE_TM
    blk_e = jnp.minimum(jnp.searchsorted(pad_ends, blk_start, side='right'), N_EXPERTS - 1).astype(i32)
    return dest.reshape(T, TOP_K), row_tok, blk_e, n_used.reshape(1)


def _t5_bucket(rel):
    n = jnp.maximum(-rel, 0)
    max_exact = NUM_BUCKETS // 2
    nf = jnp.maximum(n, 1).astype(f32)
    large = max_exact + (jnp.log(nf / max_exact) / math.log(MAX_DISTANCE / max_exact)
                         * (NUM_BUCKETS - max_exact)).astype(i32)
    large = jnp.minimum(large, NUM_BUCKETS - 1)
    return jnp.where(n < max_exact, n, large)


def _bias_tile(table, rel, mask, shift=None):
    b = jnp.transpose(table[_t5_bucket(rel)], (2, 0, 1)).astype(f32)
    if shift is not None:
        b = b - shift[:, None, None]
    return jnp.where(mask[None], b, NEG_INF)


def _group_rows(b, comps):
    H, Tq, Tk = b.shape
    b = b.reshape(KV_HEADS, REP, 1, Tq, Tk)
    b = jnp.broadcast_to(b, (KV_HEADS, REP, comps, Tq, Tk))
    return b.reshape(KV_HEADS, REP * comps * Tq, Tk)


def _mem_attn_kernel(q_ref, k_ref, v_ref, o_ref):
    scale = HEAD_DIM ** -0.5
    k_all = k_ref[...].reshape(MEM_LEN, MEM_W)
    v_all = v_ref[...].reshape(MEM_LEN, MEM_W)
    for h in range(MEM_HEADS):
        sl = slice(h * HEAD_DIM, (h + 1) * HEAD_DIM)
        s = _dot_nt(q_ref[:, sl].astype(bf16), k_all[:, sl].astype(bf16)) * scale
        m = jnp.max(s, -1, keepdims=True)
        e = jnp.exp(s - m)
        p = e * (1.0 / jnp.sum(e, -1, keepdims=True))
        o_ref[:, sl] = _dot(p.astype(bf16), v_all[:, sl].astype(bf16)).astype(o_ref.dtype)


def mem_attend(p, row_block0, n_seq, tq, tiles_per_seq, mem_k, mem_v, k_map, v_map, kv_block):
    rows = n_seq * tiles_per_seq * tq
    return pl.pallas_call(
        _mem_attn_kernel,
        grid=(n_seq, tiles_per_seq),
        in_specs=[pl.BlockSpec((tq, MEM_W), lambda n, i: (row_block0 + n * tiles_per_seq + i, QM_COL // MEM_HEADS)),
                  pl.BlockSpec(kv_block, k_map), pl.BlockSpec(kv_block, v_map)],
        out_specs=pl.BlockSpec((tq, MEM_W), lambda n, i: (n * tiles_per_seq + i, 0)),
        out_shape=jax.ShapeDtypeStruct((rows, MEM_W), f32 if tq < 16 else bf16),
        compiler_params=_params(2),
        name="mem_attn",
    )(p, mem_k, mem_v)


def _softmax_sink_pv(s, sink_col, v):
    m = jnp.maximum(jnp.max(s, -1, keepdims=True), sink_col)
    e = jnp.exp(s - m)
    den = jnp.sum(e, -1, keepdims=True) + jnp.exp(sink_col - m)
    p = e * (1.0 / den)
    return _dot(p.astype(bf16), v)


def _sink_col(sink_ref, g, tq):
    return jnp.concatenate([jnp.full((tq, 1), sink_ref[REP * g + r], f32) for r in range(REP)], axis=0)


def _swa_prompt_kernel(q_ref, kc_ref, kp_ref, vc_ref, vp_ref, bias_ref, sink_ref, o_ref):
    i = pl.program_id(1)
    scale = HEAD_DIM ** -0.5
    for g in range(KV_HEADS):
        sl = slice(g * HEAD_DIM, (g + 1) * HEAD_DIM)
        kk = jnp.concatenate([kp_ref[:, sl], kc_ref[:, sl]], axis=0).astype(bf16)
        vv = jnp.concatenate([vp_ref[:, sl], vc_ref[:, sl]], axis=0).astype(bf16)
        q3 = jnp.concatenate([q_ref[:, (REP * g + r) * HEAD_DIM:(REP * g + r + 1) * HEAD_DIM]
                              for r in range(REP)], axis=0).astype(bf16)
        s = _dot_nt(q3, kk) * scale + bias_ref[g]
        col = lax.broadcasted_iota(i32, s.shape, 1)
        s = jnp.where((col >= WINDOW) | (i > 0), s, NEG_INF)
        o = _softmax_sink_pv(s, _sink_col(sink_ref, g, WINDOW), vv)
        for r in range(REP):
            o_ref[:, (REP * g + r) * HEAD_DIM:(REP * g + r + 1) * HEAD_DIM] = (
                o[r * WINDOW:(r + 1) * WINDOW].astype(o_ref.dtype))


def swa_prompt(p, B, S, bias, sink):
    nb = S // WINDOW
    cur = lambda c: (lambda b, i: (b * nb + i, c))
    prev = lambda c: (lambda b, i: (b * nb + jnp.maximum(i - 1, 0), c))
    kc, vc = K_COL // KV_HEADS, V_COL // KV_HEADS
    return pl.pallas_call(
        _swa_prompt_kernel,
        grid=(B, nb),
        in_specs=[pl.BlockSpec((WINDOW, Q_W), cur(0)),
                  pl.BlockSpec((WINDOW, KV_W), cur(kc)), pl.BlockSpec((WINDOW, KV_W), prev(kc)),
                  pl.BlockSpec((WINDOW, KV_W), cur(vc)), pl.BlockSpec((WINDOW, KV_W), prev(vc)),
                  pl.BlockSpec((KV_HEADS, REP * WINDOW, 2 * WINDOW), lambda b, i: (0, 0, 0)),
                  pl.BlockSpec(memory_space=pltpu.SMEM)],
        out_specs=pl.BlockSpec((WINDOW, Q_W), lambda b, i: (b * nb + i, 0)),
        out_shape=jax.ShapeDtypeStruct((B * S, Q_W), bf16),
        compiler_params=_params(2),
        name="swa_prompt",
    )(p, p, p, p, p, bias, sink)


def _swa_sample_kernel(q_ref, kn_ref, vn_ref, wk_ref, wv_ref, bias_ref, sink_ref, o_ref, nk_ref, nv_ref, *, T):
    scale = HEAD_DIM ** -0.5
    wk = wk_ref[...].reshape(WINDOW, KV_W)
    wv = wv_ref[...].reshape(WINDOW, KV_W)
    kn = kn_ref[...]
    vn = vn_ref[...]
    pad = jnp.zeros((WINDOW - T, HEAD_DIM), f32)
    for g in range(KV_HEADS):
        sl = slice(g * HEAD_DIM, (g + 1) * HEAD_DIM)
        kk = jnp.concatenate([wk[:, sl], kn[:, sl], pad], axis=0).astype(bf16)
        vv = jnp.concatenate([wv[:, sl], vn[:, sl], pad], axis=0).astype(bf16)
        q3 = jnp.concatenate([q_ref[:, (REP * g + r) * HEAD_DIM:(REP * g + r + 1) * HEAD_DIM]
                              for r in range(REP)], axis=0).astype(bf16)
        s = _dot_nt(q3, kk) * scale + bias_ref[g]
        o = _softmax_sink_pv(s, _sink_col(sink_ref, g, T), vv)
        for r in range(REP):
            o_ref[:, (REP * g + r) * HEAD_DIM:(REP * g + r + 1) * HEAD_DIM] = o[r * T:(r + 1) * T]
    nk_ref[0, 0:WINDOW - T, :] = wk[T:, :]
    nk_ref[0, WINDOW - T:, :] = kn
    nv_ref[0, 0:WINDOW - T, :] = wv[T:, :]
    nv_ref[0, WINDOW - T:, :] = vn


def swa_sample(p, row_block0, DB, T, win_k, win_v, li, bias, sink):
    kc, vc = K_COL // KV_HEADS, V_COL // KV_HEADS
    win = lambda n: (li, n, 0, 0)
    return pl.pallas_call(
        functools.partial(_swa_sample_kernel, T=T),
        grid=(DB,),
        in_specs=[pl.BlockSpec((T, Q_W), lambda n: (row_block0 + n, 0)),
                  pl.BlockSpec((T, KV_W), lambda n: (row_block0 + n, kc)),
                  pl.BlockSpec((T, KV_W), lambda n: (row_block0 + n, vc)),
                  pl.BlockSpec((1, 1, WINDOW, KV_W), win), pl.BlockSpec((1, 1, WINDOW, KV_W), win),
                  pl.BlockSpec((KV_HEADS, REP * T, 2 * WINDOW), lambda n: (0, 0, 0)),
                  pl.BlockSpec(memory_space=pltpu.SMEM)],
        out_specs=[pl.BlockSpec((T, Q_W), lambda n: (n, 0)),
                   pl.BlockSpec((1, WINDOW, KV_W), lambda n: (n, 0, 0)),
                   pl.BlockSpec((1, WINDOW, KV_W), lambda n: (n, 0, 0))],
        out_shape=[jax.ShapeDtypeStruct((DB * T, Q_W), f32),
                   jax.ShapeDtypeStruct((DB, WINDOW, KV_W), f32),
                   jax.ShapeDtypeStruct((DB, WINDOW, KV_W), f32)],
        compiler_params=_params(1),
        name="swa_sample",
    )(p, p, p, win_k, win_v, bias, sink)


def _split_components(q, scale):
    lane = lax.broadcasted_iota(i32, q.shape, 1)
    qs = q * scale
    return [jnp.where(lane < DIFF_DIM, qs, 0.0), jnp.where(lane >= DIFF_DIM, qs, 0.0)]


def _diff_finish(o0, o1, lam, gain, lam_init, axis):
    o = o0 - lam * o1
    return o * lax.rsqrt(jnp.mean(o * o, axis, keepdims=True) + RMS_EPS) * gain * (1.0 - lam_init)


def _diff_prompt_kernel(q_ref, k_ref, v_ref, bias_ref, lam_ref, gain_ref, o_ref,
                        kb_scr, vt_scr, qpt_scr, m_scr, l_scr, acc_scr, *, lam_init):
    i = pl.program_id(2)
    tq, tk = DIFF_TQ, DIFF_TK
    n_chunks = k_ref.shape[0] // tk

    @pl.when(i == 0)
    def _():
        for c in range(n_chunks):
            kb_scr[c] = k_ref[c * tk:(c + 1) * tk, :].astype(bf16)
            vt_scr[c] = v_ref[c * tk:(c + 1) * tk, :].T.astype(bf16)

    pieces = []
    for r in range(REP):
        for part in _split_components(q_ref[:, r * HEAD_DIM:(r + 1) * HEAD_DIM], DIFF_DIM ** -0.5 * LOG2_E):
            pieces.append(part.T)
    qpt_scr[...] = jnp.concatenate(pieces, axis=1).astype(bf16)
    m_scr[...] = jnp.full(m_scr.shape, NEG_INF, f32)
    l_scr[...] = jnp.zeros(l_scr.shape, f32)
    acc_scr[...] = jnp.zeros(acc_scr.shape, f32)

    def chunk(c, bias):
        s = _dot(kb_scr[c], qpt_scr[...])
        if bias is not None:
            s = s + bias
        m_old = m_scr[...]
        m_new = jnp.maximum(m_old, jnp.max(s, axis=0, keepdims=True))
        p = jnp.exp2(s - m_new)
        alpha = jnp.exp2(m_old - m_new)
        l_scr[...] = alpha * l_scr[...] + jnp.sum(p, axis=0, keepdims=True)
        acc_scr[...] = alpha * acc_scr[...] + _dot(vt_scr[c], p.astype(bf16))
        m_scr[...] = m_new

    step = tq // tk
    for d in range(step):
        chunk(i * step + d, bias_ref[0, tq + d * tk:tq + (d + 1) * tk, :])

    @pl.when(i > 0)
    def _():
        for d in range(step):
            chunk((i - 1) * step + d, bias_ref[0, d * tk:(d + 1) * tk, :])

    def far(c, carry):
        chunk(c, None)
        return carry

    lax.fori_loop(0, jnp.maximum(i - 1, 0) * step, far, 0)

    acc = acc_scr[...] * (1.0 / l_scr[...])
    lam = lam_ref[0]
    for r in range(REP):
        o = _diff_finish(acc[:, (2 * r) * tq:(2 * r + 1) * tq], acc[:, (2 * r + 1) * tq:(2 * r + 2) * tq],
                         lam, gain_ref[...], lam_init, 0)
        o_ref[:, r * HEAD_DIM:(r + 1) * HEAD_DIM] = o.T.astype(o_ref.dtype)


def diff_prompt(p, B, S, bias_t, lam, gain_col, lam_init):
    nq = S // DIFF_TQ
    cols = 2 * REP * DIFF_TQ
    n_chunks = S // DIFF_TK
    return pl.pallas_call(
        functools.partial(_diff_prompt_kernel, lam_init=lam_init),
        grid=(B, KV_HEADS, nq),
        in_specs=[pl.BlockSpec((DIFF_TQ, REP * HEAD_DIM), lambda b, g, i: (b * nq + i, g)),
                  pl.BlockSpec((S, HEAD_DIM), lambda b, g, i: (b, K_COL + g)),
                  pl.BlockSpec((S, HEAD_DIM), lambda b, g, i: (b, V_COL + g)),
                  pl.BlockSpec((1, 2 * DIFF_TQ, cols), lambda b, g, i: (g, 0, 0)),
                  pl.BlockSpec(memory_space=pltpu.SMEM),
                  pl.BlockSpec((HEAD_DIM, 1), lambda b, g, i: (0, 0))],
        out_specs=pl.BlockSpec((DIFF_TQ, REP * HEAD_DIM), lambda b, g, i: (b * nq + i, g)),
        out_shape=jax.ShapeDtypeStruct((B * S, Q_W), bf16),
        scratch_shapes=[pltpu.VMEM((n_chunks, DIFF_TK, HEAD_DIM), bf16),
                        pltpu.VMEM((n_chunks, HEAD_DIM, DIFF_TK), bf16),
                        pltpu.VMEM((HEAD_DIM, cols), bf16),
                        pltpu.VMEM((1, cols), f32),
                        pltpu.VMEM((1, cols), f32),
                        pltpu.VMEM((HEAD_DIM, cols), f32)],
        compiler_params=_params(3),
        name="diff_prompt",
    )(p, p, p, bias_t, lam, gain_col)


def _diff_sample_kernel(pt_ref, q_ref, kn_ref, vn_ref, *rest, T, lam_init, n_steps):
    npg = PAGES_PER_STEP
    k_refs, v_refs = rest[:npg], rest[npg:2 * npg]
    bias_last_ref, bias_new_ref, lam_ref, gain_ref, o_ref, qp_scr, m_scr, l_scr, acc_scr = rest[2 * npg:]
    s_id = pl.program_id(1)
    rows = 2 * REP * T

    @pl.when(s_id == 0)
    def _():
        pieces = []
        for h in range(N_HEADS):
            pieces += _split_components(q_ref[:, h * HEAD_DIM:(h + 1) * HEAD_DIM], DIFF_DIM ** -0.5)
        qp_scr[...] = jnp.concatenate(pieces, axis=0)
        m_scr[...] = jnp.full(m_scr.shape, NEG_INF, f32)
        l_scr[...] = jnp.zeros(l_scr.shape, f32)
        acc_scr[...] = jnp.zeros(acc_scr.shape, f32)

    def update(s, vs):
        m_old = m_scr[...]
        m_new = jnp.maximum(m_old, jnp.max(s, -1, keepdims=True))
        p = jnp.exp(s - m_new)
        alpha = jnp.exp(m_old - m_new)
        l_scr[...] = alpha * l_scr[...] + jnp.sum(p, -1, keepdims=True)
        pb = p.astype(bf16)
        pv = jnp.concatenate([_dot(pb[g * rows:(g + 1) * rows], vs[g]) for g in range(KV_HEADS)], axis=0)
        acc_scr[...] = alpha * acc_scr[...] + pv
        m_scr[...] = m_new

    def scores(ks):
        qp = qp_scr[...].astype(bf16)
        return jnp.concatenate([_dot_nt(qp[g * rows:(g + 1) * rows], ks[g]) for g in range(KV_HEADS)], axis=0)

    def head_rows(refs, g):
        return jnp.concatenate([r[0, 0, pl.ds(g, PAGE_SIZE, stride=KV_HEADS), :] for r in refs],
                               axis=0).astype(bf16)

    last = s_id == n_steps - 1
    s = scores([head_rows(k_refs, g) for g in range(KV_HEADS)])
    tail = s[:, (npg - 1) * PAGE_SIZE:] + jnp.where(last, bias_last_ref[...], 0.0)
    s = jnp.concatenate([s[:, :(npg - 1) * PAGE_SIZE], tail], axis=1)
    update(s, [head_rows(v_refs, g) for g in range(KV_HEADS)])

    @pl.when(last)
    def _():
        pad = jnp.zeros((PAGE_SIZE - T, HEAD_DIM), f32)
        kn = [jnp.concatenate([kn_ref[:, g * HEAD_DIM:(g + 1) * HEAD_DIM], pad], axis=0).astype(bf16)
              for g in range(KV_HEADS)]
        vn = [jnp.concatenate([vn_ref[:, g * HEAD_DIM:(g + 1) * HEAD_DIM], pad], axis=0).astype(bf16)
              for g in range(KV_HEADS)]
        update(scores(kn) + bias_new_ref[...], vn)
        acc = acc_scr[...] * (1.0 / l_scr[...])
        lam = lam_ref[0]
        for h in range(N_HEADS):
            o = _diff_finish(acc[(2 * h) * T:(2 * h + 1) * T], acc[(2 * h + 1) * T:(2 * h + 2) * T],
                             lam, gain_ref[...], lam_init, -1)
            o_ref[:, h * HEAD_DIM:(h + 1) * HEAD_DIM] = o


def diff_sample(p, row_block0, DB, T, cache_k, cache_v, li, page_table, bias_last, bias_new, lam, gain, lam_init):
    n_pages = PAST_LEN // PAGE_SIZE
    npg = PAGES_PER_STEP
    n_steps = n_pages // npg
    rows = 2 * N_HEADS * T
    kc, vc = K_COL // KV_HEADS, V_COL // KV_HEADS

    def page(j):
        return lambda n, s, pt: (li, pt[n * n_pages + s * npg + j], 0, 0)

    page_block = (1, 1, PAGE_SIZE * KV_HEADS, HEAD_DIM)
    const2 = lambda n, s, pt: (0, 0)
    in_specs = ([pl.BlockSpec((T, Q_W), lambda n, s, pt: (row_block0 + n, 0)),
                 pl.BlockSpec((T, KV_W), lambda n, s, pt: (row_block0 + n, kc)),
                 pl.BlockSpec((T, KV_W), lambda n, s, pt: (row_block0 + n, vc))]
                + [pl.BlockSpec(page_block, page(j)) for j in range(npg)]
                + [pl.BlockSpec(page_block, page(j)) for j in range(npg)]
                + [pl.BlockSpec((rows, PAGE_SIZE), const2),
                   pl.BlockSpec((rows, PAGE_SIZE), const2),
                   pl.BlockSpec(memory_space=pltpu.SMEM),
                   pl.BlockSpec((1, HEAD_DIM), const2)])
    return pl.pallas_call(
        functools.partial(_diff_sample_kernel, T=T, lam_init=lam_init, n_steps=n_steps),
        grid_spec=pltpu.PrefetchScalarGridSpec(
            num_scalar_prefetch=1,
            grid=(DB, n_steps),
            in_specs=in_specs,
            out_specs=pl.BlockSpec((T, Q_W), lambda n, s, pt: (n, 0)),
            scratch_shapes=[pltpu.VMEM((rows, HEAD_DIM), f32),
                            pltpu.VMEM((rows, 1), f32),
                            pltpu.VMEM((rows, 1), f32),
                            pltpu.VMEM((rows, HEAD_DIM), f32)],
        ),
        out_shape=jax.ShapeDtypeStruct((DB * T, Q_W), f32),
        compiler_params=_params(2),
        name="diff_sample",
    )(page_table, p, p, p, *([cache_k] * npg), *([cache_v] * npg), bias_last, bias_new, lam, gain)


def _row_tile(total, candidates):
    for t in candidates:
        if total % t == 0:
            return t
    raise ValueError(f"no row tile for {total} rows")


def kernel(x_prompt, x_sample, cache_win_k, cache_win_v, cache_diff_k, cache_diff_v, cache_mem_k, cache_mem_v,
           page_table, mem_prompt, w_in, w_mem_kv, w_o, rel_bias, sinks, lam_q1, lam_k1, lam_q2, lam_k2, subln_g,
           ln1_g, ln1_b, ln2_g, ln2_b, w_router, router_bias, w_gate_up, w_down):
    B, S, _ = x_prompt.shape
    DB, T, _ = x_sample.shape
    TP, TS = B * S, DB * T
    TT = TP + TS
    assert S % DIFF_TQ == 0 and S % WINDOW == 0 and TP % T == 0 and T % 8 == 0
    n_swa, n_pool = cache_win_k.shape[0], cache_diff_k.shape[1]
    sample_block0 = TP // T

    x = jnp.concatenate([x_prompt.reshape(TP, D_MODEL), x_sample.reshape(TS, D_MODEL)], axis=0)
    xb = x.astype(bf16)
    w_in_b = w_in.astype(bf16)
    w_o_b = w_o.astype(bf16)
    w_mem_b = w_mem_kv.astype(bf16)
    mem_b = mem_prompt.reshape(B * MEM_LEN, D_MODEL).astype(bf16)
    w_router_t = w_router.T
    router_bias_b = jnp.broadcast_to(router_bias.astype(f32)[:, None], (N_EXPERTS, 128))
    win_k = cache_win_k.reshape(n_swa, DB, WINDOW, KV_W)
    win_v = cache_win_v.reshape(n_swa, DB, WINDOW, KV_W)
    pool_k = cache_diff_k.reshape(-1, n_pool, PAGE_SIZE * KV_HEADS, HEAD_DIM)
    pool_v = cache_diff_v.reshape(-1, n_pool, PAGE_SIZE * KV_HEADS, HEAD_DIM)
    cmem_k = cache_mem_k.reshape(DEPTH, DB, MEM_LEN, MEM_W)
    cmem_v = cache_mem_v.reshape(DEPTH, DB, MEM_LEN, MEM_W)
    pt_flat = page_table.reshape(-1).astype(i32)

    far = rel_bias[NUM_BUCKETS - 1].astype(f32)
    qo = jnp.arange(WINDOW)
    rel = (jnp.arange(2 * WINDOW) - WINDOW)[None, :] - qo[:, None]
    bias_swa_p = _group_rows(_bias_tile(rel_bias, rel, (rel <= 0) & (rel > -WINDOW)), 1)
    tt = jnp.arange(T)
    kpos = jnp.concatenate([jnp.arange(WINDOW) - WINDOW, tt, jnp.full((WINDOW - T,), T)])
    rel = kpos[None, :] - tt[:, None]
    bias_swa_s = _group_rows(_bias_tile(rel_bias, rel, (rel <= 0) & (rel > -WINDOW)), 1)
    qo = jnp.arange(DIFF_TQ)
    rel = (jnp.arange(2 * DIFF_TQ) - DIFF_TQ)[None, :] - qo[:, None]
    bias_diff_p = jnp.swapaxes(_group_rows(_bias_tile(rel_bias, rel, rel <= 0, far), 2), 1, 2) * LOG2_E
    rel = (jnp.arange(PAGE_SIZE) - PAGE_SIZE)[None, :] - tt[:, None]
    bias_diff_last = _group_rows(_bias_tile(rel_bias, rel, rel <= 0, far), 2).reshape(-1, PAGE_SIZE)
    kpos = jnp.concatenate([tt, jnp.full((PAGE_SIZE - T,), T)])
    rel = kpos[None, :] - tt[:, None]
    bias_diff_new = _group_rows(_bias_tile(rel_bias, rel, rel <= 0, far), 2).reshape(-1, PAGE_SIZE)

    tm_mm = _row_tile(TT, (1056, 1024, 512, 256))
    tm_ln = _row_tile(TT, (352, 256, 128))
    tm_rt = _row_tile(TT, (1408, 1024, 512, 256, 128))
    tq_mem = _row_tile(S, (512, 256, 128))

    win_k_p, win_v_p, win_k_s, win_v_s = [], [], [], []
    diff_k_p, diff_v_p, diff_k_s, diff_v_s = [], [], [], []
    mem_k_p, mem_v_p = [], []
    for l in range(DEPTH):
        i = l // 2
        p = matmul(xb, w_in_b[l], tm_mm, 512)
        k_p = p[:TP, Q_W:Q_W + KV_W].reshape(B, S, KV_HEADS, HEAD_DIM)
        v_p = p[:TP, Q_W + KV_W:Q_W + 2 * KV_W].reshape(B, S, KV_HEADS, HEAD_DIM)
        if l % 2 == 0:
            sink = sinks[i].astype(f32)
            self_p = swa_prompt(p, B, S, bias_swa_p, sink)
            self_s, nk, nv = swa_sample(p, sample_block0, DB, T, win_k, win_v, i, bias_swa_s, sink)
            win_k_p.append(k_p[:, S - WINDOW:])
            win_v_p.append(v_p[:, S - WINDOW:])
            win_k_s.append(nk.reshape(DB, WINDOW, KV_HEADS, HEAD_DIM))
            win_v_s.append(nv.reshape(DB, WINDOW, KV_HEADS, HEAD_DIM))
        else:
            lam_init = 0.8 - 0.6 * math.exp(-0.3 * l)
            lam = (jnp.exp(jnp.sum(lam_q1[i].astype(f32) * lam_k1[i].astype(f32)))
                   - jnp.exp(jnp.sum(lam_q2[i].astype(f32) * lam_k2[i].astype(f32))) + lam_init).reshape(1)
            gain = subln_g[i].astype(f32).reshape(1, HEAD_DIM)
            self_p = diff_prompt(p, B, S, bias_diff_p, lam, gain.reshape(HEAD_DIM, 1), lam_init)
            self_s = diff_sample(p, sample_block0, DB, T, pool_k, pool_v, i, pt_flat,
                                 bias_diff_last, bias_diff_new, lam, gain, lam_init)
            diff_k_p.append(k_p)
            diff_v_p.append(v_p)
            diff_k_s.append(p[TP:, Q_W:Q_W + KV_W].reshape(DB, T, KV_HEADS, HEAD_DIM))
            diff_v_s.append(p[TP:, Q_W + KV_W:Q_W + 2 * KV_W].reshape(DB, T, KV_HEADS, HEAD_DIM))
        mkv = matmul(mem_b, w_mem_b[l], B * MEM_LEN, 512).reshape(B, MEM_LEN, 2 * MEM_W)
        mem_k_p.append(mkv[:, :, :MEM_W].reshape(B, MEM_LEN, MEM_HEADS, HEAD_DIM))
        mem_v_p.append(mkv[:, :, MEM_W:].reshape(B, MEM_LEN, MEM_HEADS, HEAD_DIM))
        cross_p = mem_attend(p, 0, B, tq_mem, S // tq_mem, mkv, mkv,
                             lambda n, t: (n, 0, 0), lambda n, t: (n, 0, 1), (1, MEM_LEN, MEM_W))
        cross_s = mem_attend(p, sample_block0, DB, T, 1, cmem_k, cmem_v,
                             lambda n, t: (l, n, 0, 0), lambda n, t: (l, n, 0, 0), (1, 1, MEM_LEN, MEM_W))
        mix_self = jnp.concatenate([self_p, self_s.astype(bf16)], axis=0)
        mix_cross = jnp.concatenate([cross_p, cross_s.astype(bf16)], axis=0)
        x, xb = wo_ln(mix_self, mix_cross, x, w_o_b[l, :Q_W], w_o_b[l, Q_W:],
                      ln1_g[l].reshape(1, D_MODEL), ln1_b[l].reshape(1, D_MODEL), tm_ln)
        e_idx, gate = router(x, w_router_t, router_bias_b, tm_rt)
        dest, row_tok, blk_e, n_used = moe_dispatch(e_idx)
        yb = moe_ffn_sorted(xb[row_tok], blk_e, n_used, w_gate_up, w_down, l)
        x, xb = combine_ln(x, yb[dest[:, 0]], yb[dest[:, 1]], gate.T,
                           ln2_g[l].reshape(1, D_MODEL), ln2_b[l].reshape(1, D_MODEL), tm_ln)
    return (x[:TP].reshape(B, S, D_MODEL), x[TP:].reshape(DB, T, D_MODEL),
            jnp.stack(win_k_p), jnp.stack(win_v_p), jnp.stack(win_k_s), jnp.stack(win_v_s),
            jnp.stack(diff_k_p), jnp.stack(diff_v_p), jnp.stack(diff_k_s), jnp.stack(diff_v_s),
            jnp.stack(mem_k_p), jnp.stack(mem_v_p))
```

```python
import functools
import math

import jax
import jax.numpy as jnp
from jax import lax
from jax.experimental import pallas as pl
from jax.experimental.pallas import tpu as pltpu

f32 = jnp.float32
bf16 = jnp.bfloat16
i32 = jnp.int32

D_MODEL = 2048
DEPTH = 4
PAST_LEN = 16384
PAGE_SIZE = 128
HEAD_DIM = 128
N_HEADS = 12
KV_HEADS = 4
REP = N_HEADS // KV_HEADS
DIFF_DIM = HEAD_DIM // 2
MEM_HEADS = 4
MEM_LEN = 256
WINDOW = 128
NUM_BUCKETS = 32
MAX_DISTANCE = 128
N_EXPERTS = 16
N_GROUPS = 4
EXPERTS_PER_GROUP = N_EXPERTS // N_GROUPS
TOP_K = 2
D_EXPERT = D_MODEL // 2
Q_W = N_HEADS * HEAD_DIM
KV_W = KV_HEADS * HEAD_DIM
MEM_W = MEM_HEADS * HEAD_DIM
IN_W = Q_W + 2 * KV_W + MEM_W
ALPHA = (2.0 * DEPTH) ** 0.25
LN_EPS = 1e-5
RMS_EPS = 1e-5
NEG_INF = -1e30
LOG2_E = math.log2(math.e)

VMEM_LIMIT_BYTES = 56 * 1024 * 1024

K_COL = Q_W // HEAD_DIM
V_COL = (Q_W + KV_W) // HEAD_DIM
QM_COL = (Q_W + 2 * KV_W) // HEAD_DIM

DIFF_TQ = 256
DIFF_TK = 256
DIFF_FAR_CHUNKS = 2
PAGES_PER_STEP = 16
MOE_TM = 512
GATHER_ROWS = 512


def _params(n_axes):
    return pltpu.CompilerParams(dimension_semantics=("arbitrary",) * n_axes,
                                vmem_limit_bytes=VMEM_LIMIT_BYTES)


def _dot_nt(a, b):
    return lax.dot_general(a, b, (((1,), (1,)), ((), ())), preferred_element_type=f32)


def _dot(a, b):
    return jnp.dot(a, b, preferred_element_type=f32)


def _mm_kernel(a_ref, b_ref, o_ref):
    o_ref[...] = _dot(a_ref[...], b_ref[...])


def matmul(a, b, tm, tn):
    M, K = a.shape
    N = b.shape[1]
    return pl.pallas_call(
        _mm_kernel,
        grid=(M // tm, N // tn),
        in_specs=[pl.BlockSpec((tm, K), lambda i, j: (i, 0)),
                  pl.BlockSpec((K, tn), lambda i, j: (0, j))],
        out_specs=pl.BlockSpec((tm, tn), lambda i, j: (i, j)),
        out_shape=jax.ShapeDtypeStruct((M, N), f32),
        compiler_params=_params(2),
        name="matmul",
    )(a, b)


def _layer_norm(y, g, b):
    mu = jnp.mean(y, -1, keepdims=True)
    yc = y - mu
    var = jnp.mean(yc * yc, -1, keepdims=True)
    return yc * lax.rsqrt(var + LN_EPS) * g + b


def _wo_ln_kernel(msp_ref, mcp_ref, mss_ref, mcs_ref, x_ref, ws_ref, wc_ref, g_ref, b_ref, o_ref, ob_ref,
                  *, n_prompt_tiles):
    def body(ms, mc):
        f = _dot(ms, ws_ref[...]) + _dot(mc, wc_ref[...])
        out = _layer_norm(ALPHA * x_ref[...] + f, g_ref[...], b_ref[...])
        o_ref[...] = out
        ob_ref[...] = out.astype(bf16)

    is_prompt = pl.program_id(0) < n_prompt_tiles

    @pl.when(is_prompt)
    def _():
        body(msp_ref[...], mcp_ref[...])

    @pl.when(jnp.logical_not(is_prompt))
    def _():
        body(mss_ref[...].astype(bf16), mcs_ref[...].astype(bf16))


def wo_ln(self_p, cross_p, self_s, cross_s, x, w_self, w_cross, g, b, tm):
    M = x.shape[0]
    n_p = self_p.shape[0] // tm
    assert self_p.shape[0] % tm == 0 and self_s.shape[0] % tm == 0
    row = lambda i: (i, 0)
    prow = lambda i: (jnp.minimum(i, n_p - 1), 0)
    srow = lambda i: (jnp.maximum(i - n_p, 0), 0)
    const = lambda i: (0, 0)
    return pl.pallas_call(
        functools.partial(_wo_ln_kernel, n_prompt_tiles=n_p),
        grid=(M // tm,),
        in_specs=[pl.BlockSpec((tm, Q_W), prow), pl.BlockSpec((tm, MEM_W), prow),
                  pl.BlockSpec((tm, Q_W), srow), pl.BlockSpec((tm, MEM_W), srow),
                  pl.BlockSpec((tm, D_MODEL), row),
                  pl.BlockSpec((Q_W, D_MODEL), const), pl.BlockSpec((MEM_W, D_MODEL), const),
                  pl.BlockSpec((1, D_MODEL), const), pl.BlockSpec((1, D_MODEL), const)],
        out_specs=[pl.BlockSpec((tm, D_MODEL), row), pl.BlockSpec((tm, D_MODEL), row)],
        out_shape=[jax.ShapeDtypeStruct((M, D_MODEL), f32),
                   jax.ShapeDtypeStruct((M, D_MODEL), bf16)],
        compiler_params=_params(1),
        name="wo_ln",
    )(self_p, cross_p, self_s, cross_s, x, w_self, w_cross, g, b)


def _combine_ln_kernel(x_ref, y0_ref, y1_ref, gate_ref, g_ref, b_ref, o_ref, ob_ref):
    gate = gate_ref[...]
    f = y0_ref[...] * gate[:, 0:1] + y1_ref[...] * gate[:, 1:2]
    out = _layer_norm(ALPHA * x_ref[...] + f, g_ref[...], b_ref[...])
    o_ref[...] = out
    ob_ref[...] = out.astype(bf16)


def combine_ln(x, ya, gate_t, g, b, tm):
    M = x.shape[0]
    n_tiles = M // tm
    row = lambda i: (i, 0)
    const = lambda i: (0, 0)
    return pl.pallas_call(
        _combine_ln_kernel,
        grid=(n_tiles,),
        in_specs=[pl.BlockSpec((tm, D_MODEL), row), pl.BlockSpec((tm, D_MODEL), row),
                  pl.BlockSpec((tm, D_MODEL), lambda i: (n_tiles + i, 0)), pl.BlockSpec((tm, TOP_K), row),
                  pl.BlockSpec((1, D_MODEL), const), pl.BlockSpec((1, D_MODEL), const)],
        out_specs=[pl.BlockSpec((tm, D_MODEL), row), pl.BlockSpec((tm, D_MODEL), row)],
        out_shape=[jax.ShapeDtypeStruct((M, D_MODEL), f32),
                   jax.ShapeDtypeStruct((M, D_MODEL), bf16)],
        compiler_params=_params(1),
        name="combine_ln",
    )(x, ya, ya, gate_t, g, b)


def _gather_rows_kernel(nu_ref, idx_ref, src_ref, o_ref, *scratch, rows):
    if o_ref.dtype == src_ref.dtype:
        dst_ref, sem = o_ref, scratch[0]
    else:
        dst_ref, sem = scratch

    def row_copy(r, src_row):
        return pltpu.make_async_copy(src_ref.at[pl.ds(src_row, 1)], dst_ref.at[pl.ds(r, 1)], sem)

    used = pl.program_id(0) < nu_ref[0]

    @pl.when(used)
    def _():
        def start(r, carry):
            row_copy(r, idx_ref[0, 0, r]).start()
            return carry

        def wait(r, carry):
            row_copy(r, 0).wait()
            return carry

        lax.fori_loop(0, rows, start, 0, unroll=8)
        lax.fori_loop(0, rows, wait, 0, unroll=8)
        if dst_ref is not o_ref:
            o_ref[...] = dst_ref[...].astype(o_ref.dtype)

    @pl.when(jnp.logical_not(used))
    def _():
        o_ref[...] = jnp.zeros(o_ref.shape, o_ref.dtype)


def gather_rows(src, idx, n_used, rows, out_dtype):
    n_blocks = idx.shape[0] // rows
    D = src.shape[1]
    blk = lambda i, nu: (i, 0)
    scratch = [pltpu.SemaphoreType.DMA]
    if out_dtype != src.dtype:
        scratch = [pltpu.VMEM((rows, D), src.dtype)] + scratch
    return pl.pallas_call(
        functools.partial(_gather_rows_kernel, rows=rows),
        grid_spec=pltpu.PrefetchScalarGridSpec(
            num_scalar_prefetch=1,
            grid=(n_blocks,),
            in_specs=[pl.BlockSpec((1, 1, rows), lambda i, nu: (jnp.minimum(i, nu[0] - 1), 0, 0),
                                   memory_space=pltpu.SMEM),
                      pl.BlockSpec(memory_space=pl.ANY)],
            out_specs=pl.BlockSpec((rows, D), blk),
            scratch_shapes=scratch,
        ),
        out_shape=jax.ShapeDtypeStruct((n_blocks * rows, D), out_dtype),
        compiler_params=_params(1),
        name="gather_rows",
    )(n_used, idx.reshape(n_blocks, 1, rows), src)


def _router_kernel(x_ref, wt_ref, bias_ref, e_ref, gate_ref):
    logits = lax.dot_general(wt_ref[...], x_ref[...], (((1,), (1,)), ((), ())),
                             preferred_element_type=f32, precision=lax.Precision.HIGHEST)
    scores = jax.nn.sigmoid(logits)
    sel = scores + bias_ref[...][:, 0:1]
    n = EXPERTS_PER_GROUP
    rows = [sel[e:e + 1, :] for e in range(N_EXPERTS)]
    srows = [scores[e:e + 1, :] for e in range(N_EXPERTS)]
    gscore = []
    for g in range(N_GROUPS):
        v = rows[g * n:(g + 1) * n]
        best = None
        for a in range(n):
            for b in range(a + 1, n):
                pair = v[a] + v[b]
                best = pair if best is None else jnp.maximum(best, pair)
        gscore.append(best)
    g_idx = jnp.zeros_like(gscore[0], dtype=i32)
    best = gscore[0]
    for g in range(1, N_GROUPS):
        take = gscore[g] > best
        g_idx = jnp.where(take, g, g_idx)
        best = jnp.where(take, gscore[g], best)
    ing, sg = [], []
    for a in range(n):
        va, sa = rows[a], srows[a]
        for g in range(1, N_GROUPS):
            va = jnp.where(g_idx == g, rows[g * n + a], va)
            sa = jnp.where(g_idx == g, srows[g * n + a], sa)
        ing.append(va)
        sg.append(sa)
    l_idx = [jnp.zeros_like(g_idx), jnp.zeros_like(g_idx)]
    w = [jnp.zeros_like(best), jnp.zeros_like(best)]
    for a in range(n):
        rank = jnp.zeros_like(g_idx)
        for b in range(n):
            if b == a:
                continue
            ahead = (ing[b] > ing[a]) | ((ing[b] == ing[a]) & (b < a))
            rank = rank + ahead.astype(i32)
        for k in range(TOP_K):
            hit = rank == k
            l_idx[k] = jnp.where(hit, a, l_idx[k])
            w[k] = jnp.where(hit, sg[a], w[k])
    tot = w[0] + w[1]
    e_ref[...] = jnp.concatenate([g_idx * n + l_idx[0], g_idx * n + l_idx[1]], axis=0)
    gate_ref[...] = jnp.concatenate([w[0] / tot, w[1] / tot], axis=0)


def router(x, w_router_t, router_bias, tm):
    M = x.shape[0]
    return pl.pallas_call(
        _router_kernel,
        grid=(M // tm,),
        in_specs=[pl.BlockSpec((tm, D_MODEL), lambda i: (i, 0)),
                  pl.BlockSpec((N_EXPERTS, D_MODEL), lambda i: (0, 0)),
                  pl.BlockSpec((N_EXPERTS, 128), lambda i: (0, 0))],
        out_specs=[pl.BlockSpec((TOP_K, tm), lambda i: (0, i)),
                   pl.BlockSpec((TOP_K, tm), lambda i: (0, i))],
        out_shape=[jax.ShapeDtypeStruct((TOP_K, M), i32),
                   jax.ShapeDtypeStruct((TOP_K, M), f32)],
        compiler_params=_params(1),
        name="router",
    )(x, w_router_t, router_bias)


def _moe_up_kernel(be_ref, nu_ref, x_ref, wg_ref, wu_ref, h_ref):
    used = pl.program_id(1) < nu_ref[0]

    @pl.when(used)
    def _():
        x = x_ref[...]
        g = _dot(x, wg_ref[0, 0].astype(bf16))
        u = _dot(x, wu_ref[0, 0].astype(bf16))
        h_ref[...] = (g * jax.nn.sigmoid(g) * u).astype(bf16)

    @pl.when(jnp.logical_not(used))
    def _():
        h_ref[...] = jnp.zeros(h_ref.shape, h_ref.dtype)


def _moe_down_kernel(be_ref, nu_ref, h_ref, wd_ref, y_ref):
    used = pl.program_id(1) < nu_ref[0]

    @pl.when(used)
    def _():
        y_ref[...] = _dot(h_ref[...], wd_ref[0, 0].astype(bf16))

    @pl.when(jnp.logical_not(used))
    def _():
        y_ref[...] = jnp.zeros(y_ref.shape, y_ref.dtype)


def moe_ffn_sorted(xs, blk_e, n_used, w_gu, w_dn, l, tn_up=512, tn_dn=1024):
    R = xs.shape[0]
    n_blocks = R // MOE_TM
    nj = D_EXPERT // tn_up
    blk = lambda j, i, be, nu: jnp.minimum(i, nu[0] - 1)
    h = pl.pallas_call(
        _moe_up_kernel,
        grid_spec=pltpu.PrefetchScalarGridSpec(
            num_scalar_prefetch=2,
            grid=(nj, n_blocks),
            in_specs=[pl.BlockSpec((MOE_TM, D_MODEL), lambda j, i, be, nu: (blk(j, i, be, nu), 0)),
                      pl.BlockSpec((1, 1, D_MODEL, tn_up), lambda j, i, be, nu: (l, be[i], 0, j)),
                      pl.BlockSpec((1, 1, D_MODEL, tn_up), lambda j, i, be, nu: (l, be[i], 0, nj + j))],
            out_specs=pl.BlockSpec((MOE_TM, tn_up), lambda j, i, be, nu: (i, j)),
        ),
        out_shape=jax.ShapeDtypeStruct((R, D_EXPERT), bf16),
        compiler_params=_params(2),
        name="moe_up",
    )(blk_e, n_used, xs, w_gu, w_gu)
    nj2 = D_MODEL // tn_dn
    return pl.pallas_call(
        _moe_down_kernel,
        grid_spec=pltpu.PrefetchScalarGridSpec(
            num_scalar_prefetch=2,
            grid=(nj2, n_blocks),
            in_specs=[pl.BlockSpec((MOE_TM, D_EXPERT), lambda j, i, be, nu: (blk(j, i, be, nu), 0)),
                      pl.BlockSpec((1, 1, D_EXPERT, tn_dn), lambda j, i, be, nu: (l, be[i], 0, j))],
            out_specs=pl.BlockSpec((MOE_TM, tn_dn), lambda j, i, be, nu: (i, j)),
        ),
        out_shape=jax.ShapeDtypeStruct((R, D_MODEL), f32),
        compiler_params=_params(2),
        name="moe_down",
    )(blk_e, n_used, h, w_dn)


def moe_dispatch(e_idx):
    T = e_idx.shape[1]
    A = T * TOP_K
    flat_e = e_idx.T.reshape(A)
    onehot = (flat_e[:, None] == jnp.arange(N_EXPERTS, dtype=i32)[None, :]).astype(i32)
    csum = jnp.cumsum(onehot, axis=0)
    rank = jnp.take_along_axis(csum, flat_e[:, None], axis=1)[:, 0] - 1
    counts = csum[-1]
    padded = (counts + MOE_TM - 1) // MOE_TM * MOE_TM
    pad_ends = jnp.cumsum(padded)
    pad_starts = pad_ends - padded
    dest = pad_starts[flat_e] + rank
    n_blocks = -(-A // MOE_TM) + N_EXPERTS
    tok = jnp.arange(A, dtype=i32) // TOP_K
    row_tok = jnp.zeros((n_blocks * MOE_TM,), i32).at[dest].set(tok)
    n_used = (pad_ends[-1] // MOE_TM).astype(i32)
    blk_start = jnp.minimum(jnp.arange(n_blocks, dtype=i32), n_used - 1) * MOE_TM
    blk_e = jnp.minimum(jnp.searchsorted(pad_ends, blk_start, side='right'), N_EXPERTS - 1).astype(i32)
    return dest.reshape(T, TOP_K), row_tok, blk_e, n_used.reshape(1)


def _t5_bucket(rel):
    n = jnp.maximum(-rel, 0)
    max_exact = NUM_BUCKETS // 2
    nf = jnp.maximum(n, 1).astype(f32)
    large = max_exact + (jnp.log(nf / max_exact) / math.log(MAX_DISTANCE / max_exact)
                         * (NUM_BUCKETS - max_exact)).astype(i32)
    large = jnp.minimum(large, NUM_BUCKETS - 1)
    return jnp.where(n < max_exact, n, large)


def _bias_tile(table, rel, mask, shift=None):
    t = table.astype(f32)
    if shift is not None:
        t = t - shift[None, :]
    bucket = jnp.where(mask, _t5_bucket(rel), -1)[None]
    b = jnp.full((table.shape[1],) + rel.shape, NEG_INF, f32)
    for i in range(NUM_BUCKETS):
        b = jnp.where(bucket == i, t[i][:, None, None], b)
    return b


def _group_rows(b, comps):
    H, Tq, Tk = b.shape
    b = b.reshape(KV_HEADS, REP, 1, Tq, Tk)
    b = jnp.broadcast_to(b, (KV_HEADS, REP, comps, Tq, Tk))
    return b.reshape(KV_HEADS, REP * comps * Tq, Tk)


def _mem_attn_kernel(q_ref, k_ref, v_ref, o_ref, *, interleaved):
    scale = HEAD_DIM ** -0.5
    for h in range(MEM_HEADS):
        sl = slice(h * HEAD_DIM, (h + 1) * HEAD_DIM)
        if interleaved:
            k = k_ref[0, 0, pl.ds(h, MEM_LEN, stride=MEM_HEADS), :]
            v = v_ref[0, 0, pl.ds(h, MEM_LEN, stride=MEM_HEADS), :]
        else:
            k = k_ref[0, :, sl]
            v = v_ref[0, :, sl]
        s = _dot_nt(q_ref[:, sl].astype(bf16), k.astype(bf16)) * scale
        m = jnp.max(s, -1, keepdims=True)
        e = jnp.exp(s - m)
        p = e * (1.0 / jnp.sum(e, -1, keepdims=True))
        o_ref[:, sl] = _dot(p.astype(bf16), v.astype(bf16)).astype(o_ref.dtype)


def mem_attend(p, row_block0, n_seq, tq, tiles_per_seq, mem_k, mem_v, k_map, v_map, kv_block):
    rows = n_seq * tiles_per_seq * tq
    return pl.pallas_call(
        functools.partial(_mem_attn_kernel, interleaved=len(kv_block) == 4),
        grid=(n_seq, tiles_per_seq),
        in_specs=[pl.BlockSpec((tq, MEM_W), lambda n, i: (row_block0 + n * tiles_per_seq + i, QM_COL // MEM_HEADS)),
                  pl.BlockSpec(kv_block, k_map), pl.BlockSpec(kv_block, v_map)],
        out_specs=pl.BlockSpec((tq, MEM_W), lambda n, i: (n * tiles_per_seq + i, 0)),
        out_shape=jax.ShapeDtypeStruct((rows, MEM_W), f32 if tq < 16 else bf16),
        compiler_params=_params(2),
        name="mem_attn",
    )(p, mem_k, mem_v)


def _softmax_sink_pv(s, sink_col, v):
    m = jnp.maximum(jnp.max(s, -1, keepdims=True), sink_col)
    e = jnp.exp(s - m)
    den = jnp.sum(e, -1, keepdims=True) + jnp.exp(sink_col - m)
    p = e * (1.0 / den)
    return _dot(p.astype(bf16), v)


def _sink_col(sink_ref, g, tq):
    return jnp.concatenate([jnp.full((tq, 1), sink_ref[REP * g + r], f32) for r in range(REP)], axis=0)


def _swa_prompt_kernel(q_ref, kc_ref, kp_ref, vc_ref, vp_ref, bias_ref, sink_ref, o_ref):
    i = pl.program_id(1)
    scale = HEAD_DIM ** -0.5
    for g in range(KV_HEADS):
        sl = slice(g * HEAD_DIM, (g + 1) * HEAD_DIM)
        kk = jnp.concatenate([kp_ref[:, sl], kc_ref[:, sl]], axis=0).astype(bf16)
        vv = jnp.concatenate([vp_ref[:, sl], vc_ref[:, sl]], axis=0).astype(bf16)
        q3 = jnp.concatenate([q_ref[:, (REP * g + r) * HEAD_DIM:(REP * g + r + 1) * HEAD_DIM]
                              for r in range(REP)], axis=0).astype(bf16)
        s = _dot_nt(q3, kk) * scale + bias_ref[g]
        col = lax.broadcasted_iota(i32, s.shape, 1)
        s = jnp.where((col >= WINDOW) | (i > 0), s, NEG_INF)
        o = _softmax_sink_pv(s, _sink_col(sink_ref, g, WINDOW), vv)
        for r in range(REP):
            o_ref[:, (REP * g + r) * HEAD_DIM:(REP * g + r + 1) * HEAD_DIM] = (
                o[r * WINDOW:(r + 1) * WINDOW].astype(o_ref.dtype))


def swa_prompt(p, B, S, bias, sink):
    nb = S // WINDOW
    cur = lambda c: (lambda b, i: (b * nb + i, c))
    prev = lambda c: (lambda b, i: (b * nb + jnp.maximum(i - 1, 0), c))
    kc, vc = K_COL // KV_HEADS, V_COL // KV_HEADS
    return pl.pallas_call(
        _swa_prompt_kernel,
        grid=(B, nb),
        in_specs=[pl.BlockSpec((WINDOW, Q_W), cur(0)),
                  pl.BlockSpec((WINDOW, KV_W), cur(kc)), pl.BlockSpec((WINDOW, KV_W), prev(kc)),
                  pl.BlockSpec((WINDOW, KV_W), cur(vc)), pl.BlockSpec((WINDOW, KV_W), prev(vc)),
                  pl.BlockSpec((KV_HEADS, REP * WINDOW, 2 * WINDOW), lambda b, i: (0, 0, 0)),
                  pl.BlockSpec(memory_space=pltpu.SMEM)],
        out_specs=pl.BlockSpec((WINDOW, Q_W), lambda b, i: (b * nb + i, 0)),
        out_shape=jax.ShapeDtypeStruct((B * S, Q_W), bf16),
        compiler_params=_params(2),
        name="swa_prompt",
    )(p, p, p, p, p, bias, sink)


def _swa_sample_kernel(q_ref, kn_ref, vn_ref, wk_ref, wv_ref, bias_ref, sink_ref, o_ref, nk_ref, nv_ref, *, T):
    scale = HEAD_DIM ** -0.5
    wk = wk_ref[...].reshape(WINDOW, KV_W)
    wv = wv_ref[...].reshape(WINDOW, KV_W)
    kn = kn_ref[...]
    vn = vn_ref[...]
    pad = jnp.zeros((WINDOW - T, HEAD_DIM), f32)
    for g in range(KV_HEADS):
        sl = slice(g * HEAD_DIM, (g + 1) * HEAD_DIM)
        kk = jnp.concatenate([wk[:, sl], kn[:, sl], pad], axis=0).astype(bf16)
        vv = jnp.concatenate([wv[:, sl], vn[:, sl], pad], axis=0).astype(bf16)
        q3 = jnp.concatenate([q_ref[:, (REP * g + r) * HEAD_DIM:(REP * g + r + 1) * HEAD_DIM]
                              for r in range(REP)], axis=0).astype(bf16)
        s = _dot_nt(q3, kk) * scale + bias_ref[g]
        o = _softmax_sink_pv(s, _sink_col(sink_ref, g, T), vv)
        for r in range(REP):
            o_ref[:, (REP * g + r) * HEAD_DIM:(REP * g + r + 1) * HEAD_DIM] = o[r * T:(r + 1) * T]
    nk_ref[0, 0:WINDOW - T, :] = wk[T:, :]
    nk_ref[0, WINDOW - T:, :] = kn
    nv_ref[0, 0:WINDOW - T, :] = wv[T:, :]
    nv_ref[0, WINDOW - T:, :] = vn


def swa_sample(p, row_block0, DB, T, win_k, win_v, li, bias, sink):
    kc, vc = K_COL // KV_HEADS, V_COL // KV_HEADS
    win = lambda n: (li, n, 0, 0)
    return pl.pallas_call(
        functools.partial(_swa_sample_kernel, T=T),
        grid=(DB,),
        in_specs=[pl.BlockSpec((T, Q_W), lambda n: (row_block0 + n, 0)),
                  pl.BlockSpec((T, KV_W), lambda n: (row_block0 + n, kc)),
                  pl.BlockSpec((T, KV_W), lambda n: (row_block0 + n, vc)),
                  pl.BlockSpec((1, 1, WINDOW, KV_W), win), pl.BlockSpec((1, 1, WINDOW, KV_W), win),
                  pl.BlockSpec((KV_HEADS, REP * T, 2 * WINDOW), lambda n: (0, 0, 0)),
                  pl.BlockSpec(memory_space=pltpu.SMEM)],
        out_specs=[pl.BlockSpec((T, Q_W), lambda n: (n, 0)),
                   pl.BlockSpec((1, WINDOW, KV_W), lambda n: (n, 0, 0)),
                   pl.BlockSpec((1, WINDOW, KV_W), lambda n: (n, 0, 0))],
        out_shape=[jax.ShapeDtypeStruct((DB * T, Q_W), f32),
                   jax.ShapeDtypeStruct((DB, WINDOW, KV_W), f32),
                   jax.ShapeDtypeStruct((DB, WINDOW, KV_W), f32)],
        compiler_params=_params(1),
        name="swa_sample",
    )(p, p, p, win_k, win_v, bias, sink)


def _split_components(q, scale):
    lane = lax.broadcasted_iota(i32, q.shape, 1)
    qs = q * scale
    return [jnp.where(lane < DIFF_DIM, qs, 0.0), jnp.where(lane >= DIFF_DIM, qs, 0.0)]


def _diff_finish(o0, o1, lam, gain, lam_init, axis):
    o = o0 - lam * o1
    return o * lax.rsqrt(jnp.mean(o * o, axis, keepdims=True) + RMS_EPS) * gain * (1.0 - lam_init)


def _diff_prompt_kernel(q_ref, k_ref, v_ref, bias_ref, lam_ref, gain_ref, o_ref,
                        kb_scr, vt_scr, qpt_scr, m_scr, l_scr, acc_scr, *, lam_init):
    i = pl.program_id(2)
    tq, tk = DIFF_TQ, DIFF_TK
    n_chunks = k_ref.shape[0] // tk

    @pl.when(i == 0)
    def _():
        for c in range(n_chunks):
            kb_scr[c] = k_ref[c * tk:(c + 1) * tk, :].astype(bf16)
            vt_scr[c] = v_ref[c * tk:(c + 1) * tk, :].T.astype(bf16)

    pieces = []
    for r in range(REP):
        for part in _split_components(q_ref[:, r * HEAD_DIM:(r + 1) * HEAD_DIM], DIFF_DIM ** -0.5 * LOG2_E):
            pieces.append(part.T)
    qpt_scr[...] = jnp.concatenate(pieces, axis=1).astype(bf16)
    m_scr[...] = jnp.full(m_scr.shape, NEG_INF, f32)
    l_scr[...] = jnp.zeros(l_scr.shape, f32)
    acc_scr[...] = jnp.zeros(acc_scr.shape, f32)

    def chunk(c, n, bias):
        if n == 1:
            kb, vt = kb_scr[c], vt_scr[c]
        else:
            kb = kb_scr[pl.ds(c, n)].reshape(n * tk, HEAD_DIM)
            vt = jnp.concatenate([vt_scr[c + d] for d in range(n)], axis=1)
        s = _dot(kb, qpt_scr[...])
        if bias is not None:
            s = s + bias
        m_old = m_scr[...]
        m_new = jnp.maximum(m_old, jnp.max(s, axis=0, keepdims=True))
        p = jnp.exp2(s - m_new)
        alpha = jnp.exp2(m_old - m_new)
        l_scr[...] = alpha * l_scr[...] + jnp.sum(p, axis=0, keepdims=True)
        acc_scr[...] = alpha * acc_scr[...] + _dot(vt, p.astype(bf16))
        m_scr[...] = m_new

    step = tq // tk
    for d in range(step):
        chunk(i * step + d, 1, bias_ref[0, tq + d * tk:tq + (d + 1) * tk, :])

    @pl.when(i > 0)
    def _():
        for d in range(step):
            chunk((i - 1) * step + d, 1, bias_ref[0, d * tk:(d + 1) * tk, :])

    n_far = jnp.maximum(i - 1, 0) * step
    n_group = n_far // DIFF_FAR_CHUNKS

    def far_group(j, carry):
        chunk(j * DIFF_FAR_CHUNKS, DIFF_FAR_CHUNKS, None)
        return carry

    def far_single(c, carry):
        chunk(c, 1, None)
        return carry

    lax.fori_loop(0, n_group, far_group, 0)
    lax.fori_loop(n_group * DIFF_FAR_CHUNKS, n_far, far_single, 0)

    acc = acc_scr[...] * (1.0 / l_scr[...])
    lam = lam_ref[0]
    for r in range(REP):
        o = _diff_finish(acc[:, (2 * r) * tq:(2 * r + 1) * tq], acc[:, (2 * r + 1) * tq:(2 * r + 2) * tq],
                         lam, gain_ref[...], lam_init, 0)
        o_ref[:, r * HEAD_DIM:(r + 1) * HEAD_DIM] = o.T.astype(o_ref.dtype)


def diff_prompt(p, B, S, bias_t, lam, gain_col, lam_init):
    nq = S // DIFF_TQ
    cols = 2 * REP * DIFF_TQ
    n_chunks = S // DIFF_TK
    return pl.pallas_call(
        functools.partial(_diff_prompt_kernel, lam_init=lam_init),
        grid=(B, KV_HEADS, nq),
        in_specs=[pl.BlockSpec((DIFF_TQ, REP * HEAD_DIM), lambda b, g, i: (b * nq + i, g)),
                  pl.BlockSpec((S, HEAD_DIM), lambda b, g, i: (b, K_COL + g)),
                  pl.BlockSpec((S, HEAD_DIM), lambda b, g, i: (b, V_COL + g)),
                  pl.BlockSpec((1, 2 * DIFF_TQ, cols), lambda b, g, i: (g, 0, 0)),
                  pl.BlockSpec(memory_space=pltpu.SMEM),
                  pl.BlockSpec((HEAD_DIM, 1), lambda b, g, i: (0, 0))],
        out_specs=pl.BlockSpec((DIFF_TQ, REP * HEAD_DIM), lambda b, g, i: (b * nq + i, g)),
        out_shape=jax.ShapeDtypeStruct((B * S, Q_W), bf16),
        scratch_shapes=[pltpu.VMEM((n_chunks, DIFF_TK, HEAD_DIM), bf16),
                        pltpu.VMEM((n_chunks, HEAD_DIM, DIFF_TK), bf16),
                        pltpu.VMEM((HEAD_DIM, cols), bf16),
                        pltpu.VMEM((1, cols), f32),
                        pltpu.VMEM((1, cols), f32),
                        pltpu.VMEM((HEAD_DIM, cols), f32)],
        compiler_params=_params(3),
        name="diff_prompt",
    )(p, p, p, bias_t, lam, gain_col)


def _diff_sample_kernel(pt_ref, q_ref, kn_ref, vn_ref, *rest, T, lam_init, n_steps):
    npg = PAGES_PER_STEP
    k_refs, v_refs = rest[:npg], rest[npg:2 * npg]
    bias_last_ref, bias_new_ref, lam_ref, gain_ref, o_ref, qp_scr, m_scr, l_scr, acc_scr = rest[2 * npg:]
    s_id = pl.program_id(1)
    rows = 2 * REP * T

    @pl.when(s_id == 0)
    def _():
        pieces = []
        for h in range(N_HEADS):
            pieces += _split_components(q_ref[:, h * HEAD_DIM:(h + 1) * HEAD_DIM], DIFF_DIM ** -0.5)
        qp_scr[...] = jnp.concatenate(pieces, axis=0)
        m_scr[...] = jnp.full(m_scr.shape, NEG_INF, f32)
        l_scr[...] = jnp.zeros(l_scr.shape, f32)
        acc_scr[...] = jnp.zeros(acc_scr.shape, f32)

    def update(s, vs):
        m_old = m_scr[...]
        m_new = jnp.maximum(m_old, jnp.max(s, -1, keepdims=True))
        p = jnp.exp(s - m_new)
        alpha = jnp.exp(m_old - m_new)
        l_scr[...] = alpha * l_scr[...] + jnp.sum(p, -1, keepdims=True)
        pb = p.astype(bf16)
        pv = jnp.concatenate([_dot(pb[g * rows:(g + 1) * rows], vs[g]) for g in range(KV_HEADS)], axis=0)
        acc_scr[...] = alpha * acc_scr[...] + pv
        m_scr[...] = m_new

    def scores(ks):
        qp = qp_scr[...].astype(bf16)
        return jnp.concatenate([_dot_nt(qp[g * rows:(g + 1) * rows], ks[g]) for g in range(KV_HEADS)], axis=0)

    def head_rows(refs, g):
        return jnp.concatenate([r[0, 0, pl.ds(g, PAGE_SIZE, stride=KV_HEADS), :] for r in refs],
                               axis=0).astype(bf16)

    last = s_id == n_steps - 1
    s = scores([head_rows(k_refs, g) for g in range(KV_HEADS)])
    tail = s[:, (npg - 1) * PAGE_SIZE:] + jnp.where(last, bias_last_ref[...], 0.0)
    s = jnp.concatenate([s[:, :(npg - 1) * PAGE_SIZE], tail], axis=1)
    update(s, [head_rows(v_refs, g) for g in range(KV_HEADS)])

    @pl.when(last)
    def _():
        pad = jnp.zeros((PAGE_SIZE - T, HEAD_DIM), f32)
        kn = [jnp.concatenate([kn_ref[:, g * HEAD_DIM:(g + 1) * HEAD_DIM], pad], axis=0).astype(bf16)
              for g in range(KV_HEADS)]
        vn = [jnp.concatenate([vn_ref[:, g * HEAD_DIM:(g + 1) * HEAD_DIM], pad], axis=0).astype(bf16)
              for g in range(KV_HEADS)]
        update(scores(kn) + bias_new_ref[...], vn)
        acc = acc_scr[...] * (1.0 / l_scr[...])
        lam = lam_ref[0]
        for h in range(N_HEADS):
            o = _diff_finish(acc[(2 * h) * T:(2 * h + 1) * T], acc[(2 * h + 1) * T:(2 * h + 2) * T],
                             lam, gain_ref[...], lam_init, -1)
            o_ref[:, h * HEAD_DIM:(h + 1) * HEAD_DIM] = o


def diff_sample(p, row_block0, DB, T, cache_k, cache_v, li, page_table, bias_last, bias_new, lam, gain, lam_init):
    n_pages = PAST_LEN // PAGE_SIZE
    npg = PAGES_PER_STEP
    n_steps = n_pages // npg
    rows = 2 * N_HEADS * T
    kc, vc = K_COL // KV_HEADS, V_COL // KV_HEADS

    def page(j):
        return lambda n, s, pt: (li, pt[n * n_pages + s * npg + j], 0, 0)

    page_block = (1, 1, PAGE_SIZE * KV_HEADS, HEAD_DIM)
    const2 = lambda n, s, pt: (0, 0)
    in_specs = ([pl.BlockSpec((T, Q_W), lambda n, s, pt: (row_block0 + n, 0)),
                 pl.BlockSpec((T, KV_W), lambda n, s, pt: (row_block0 + n, kc)),
                 pl.BlockSpec((T, KV_W), lambda n, s, pt: (row_block0 + n, vc))]
                + [pl.BlockSpec(page_block, page(j)) for j in range(npg)]
                + [pl.BlockSpec(page_block, page(j)) for j in range(npg)]
                + [pl.BlockSpec((rows, PAGE_SIZE), const2),
                   pl.BlockSpec((rows, PAGE_SIZE), const2),
                   pl.BlockSpec(memory_space=pltpu.SMEM),
                   pl.BlockSpec((1, HEAD_DIM), const2)])
    return pl.pallas_call(
        functools.partial(_diff_sample_kernel, T=T, lam_init=lam_init, n_steps=n_steps),
        grid_spec=pltpu.PrefetchScalarGridSpec(
            num_scalar_prefetch=1,
            grid=(DB, n_steps),
            in_specs=in_specs,
            out_specs=pl.BlockSpec((T, Q_W), lambda n, s, pt: (n, 0)),
            scratch_shapes=[pltpu.VMEM((rows, HEAD_DIM), f32),
                            pltpu.VMEM((rows, 1), f32),
                            pltpu.VMEM((rows, 1), f32),
                            pltpu.VMEM((rows, HEAD_DIM), f32)],
        ),
        out_shape=jax.ShapeDtypeStruct((DB * T, Q_W), f32),
        compiler_params=_params(2),
        name="diff_sample",
    )(page_table, p, p, p, *([cache_k] * npg), *([cache_v] * npg), bias_last, bias_new, lam, gain)


def _row_tile(total, candidates):
    for t in candidates:
        if total % t == 0:
            return t
    raise ValueError(f"no row tile for {total} rows")


def kernel(x_prompt, x_sample, cache_win_k, cache_win_v, cache_diff_k, cache_diff_v, cache_mem_k, cache_mem_v,
           page_table, mem_prompt, w_in, w_mem_kv, w_o, rel_bias, sinks, lam_q1, lam_k1, lam_q2, lam_k2, subln_g,
           ln1_g, ln1_b, ln2_g, ln2_b, w_router, router_bias, w_gate_up, w_down):
    B, S, _ = x_prompt.shape
    DB, T, _ = x_sample.shape
    TP, TS = B * S, DB * T
    TT = TP + TS
    assert S % DIFF_TQ == 0 and S % WINDOW == 0 and TP % T == 0 and T % 8 == 0
    n_swa, n_pool = cache_win_k.shape[0], cache_diff_k.shape[1]
    sample_block0 = TP // T

    x = jnp.concatenate([x_prompt.reshape(TP, D_MODEL), x_sample.reshape(TS, D_MODEL)], axis=0)
    xb = x.astype(bf16)
    w_in_b = w_in.astype(bf16)
    w_o_b = w_o.astype(bf16)
    w_mem_b = w_mem_kv.astype(bf16)
    mem_b = mem_prompt.reshape(B * MEM_LEN, D_MODEL).astype(bf16)
    w_router_t = w_router.T
    router_bias_b = jnp.broadcast_to(router_bias.astype(f32)[:, None], (N_EXPERTS, 128))
    win_k = cache_win_k.reshape(n_swa, DB, WINDOW, KV_W)
    win_v = cache_win_v.reshape(n_swa, DB, WINDOW, KV_W)
    pool_k = cache_diff_k.reshape(-1, n_pool, PAGE_SIZE * KV_HEADS, HEAD_DIM)
    pool_v = cache_diff_v.reshape(-1, n_pool, PAGE_SIZE * KV_HEADS, HEAD_DIM)
    cmem_k = cache_mem_k.reshape(DEPTH, DB, MEM_LEN * MEM_HEADS, HEAD_DIM)
    cmem_v = cache_mem_v.reshape(DEPTH, DB, MEM_LEN * MEM_HEADS, HEAD_DIM)
    pt_flat = page_table.reshape(-1).astype(i32)

    far = rel_bias[NUM_BUCKETS - 1].astype(f32)
    qo = jnp.arange(WINDOW)
    rel = (jnp.arange(2 * WINDOW) - WINDOW)[None, :] - qo[:, None]
    bias_swa_p = _group_rows(_bias_tile(rel_bias, rel, (rel <= 0) & (rel > -WINDOW)), 1)
    tt = jnp.arange(T)
    kpos = jnp.concatenate([jnp.arange(WINDOW) - WINDOW, tt, jnp.full((WINDOW - T,), T)])
    rel = kpos[None, :] - tt[:, None]
    bias_swa_s = _group_rows(_bias_tile(rel_bias, rel, (rel <= 0) & (rel > -WINDOW)), 1)
    qo = jnp.arange(DIFF_TQ)
    rel = (jnp.arange(2 * DIFF_TQ) - DIFF_TQ)[None, :] - qo[:, None]
    bias_diff_p = jnp.swapaxes(_group_rows(_bias_tile(rel_bias, rel, rel <= 0, far), 2), 1, 2) * LOG2_E
    rel = (jnp.arange(PAGE_SIZE) - PAGE_SIZE)[None, :] - tt[:, None]
    bias_diff_last = _group_rows(_bias_tile(rel_bias, rel, rel <= 0, far), 2).reshape(-1, PAGE_SIZE)
    kpos = jnp.concatenate([tt, jnp.full((PAGE_SIZE - T,), T)])
    rel = kpos[None, :] - tt[:, None]
    bias_diff_new = _group_rows(_bias_tile(rel_bias, rel, rel <= 0, far), 2).reshape(-1, PAGE_SIZE)

    tm_mm = _row_tile(TT, (1056, 1024, 512, 256))
    tm_ln = _row_tile(TT, (352, 256, 128))
    tm_wo = _row_tile(math.gcd(TP, TS), (256, 128))
    tm_rt = _row_tile(TT, (1408, 1024, 512, 256, 128))
    tq_mem = _row_tile(S, (512, 256, 128))
    assert (TOP_K * TT) % GATHER_ROWS == 0
    all_blocks = jnp.full((1,), TOP_K * TT // GATHER_ROWS, i32)

    win_k_p, win_v_p, win_k_s, win_v_s = [], [], [], []
    diff_k_p, diff_v_p, diff_k_s, diff_v_s = [], [], [], []
    mem_k_p, mem_v_p = [], []
    for l in range(DEPTH):
        i = l // 2
        p = matmul(xb, w_in_b[l], tm_mm, 512)
        k_p = p[:TP, Q_W:Q_W + KV_W].reshape(B, S, KV_HEADS, HEAD_DIM)
        v_p = p[:TP, Q_W + KV_W:Q_W + 2 * KV_W].reshape(B, S, KV_HEADS, HEAD_DIM)
        if l % 2 == 0:
            sink = sinks[i].astype(f32)
            self_p = swa_prompt(p, B, S, bias_swa_p, sink)
            self_s, nk, nv = swa_sample(p, sample_block0, DB, T, win_k, win_v, i, bias_swa_s, sink)
            win_k_p.append(k_p[:, S - WINDOW:])
            win_v_p.append(v_p[:, S - WINDOW:])
            win_k_s.append(nk.reshape(DB, WINDOW, KV_HEADS, HEAD_DIM))
            win_v_s.append(nv.reshape(DB, WINDOW, KV_HEADS, HEAD_DIM))
        else:
            lam_init = 0.8 - 0.6 * math.exp(-0.3 * l)
            lam = (jnp.exp(jnp.sum(lam_q1[i].astype(f32) * lam_k1[i].astype(f32)))
                   - jnp.exp(jnp.sum(lam_q2[i].astype(f32) * lam_k2[i].astype(f32))) + lam_init).reshape(1)
            gain = subln_g[i].astype(f32).reshape(1, HEAD_DIM)
            self_p = diff_prompt(p, B, S, bias_diff_p, lam, gain.reshape(HEAD_DIM, 1), lam_init)
            self_s = diff_sample(p, sample_block0, DB, T, pool_k, pool_v, i, pt_flat,
                                 bias_diff_last, bias_diff_new, lam, gain, lam_init)
            diff_k_p.append(k_p)
            diff_v_p.append(v_p)
            diff_k_s.append(p[TP:, Q_W:Q_W + KV_W].reshape(DB, T, KV_HEADS, HEAD_DIM))
            diff_v_s.append(p[TP:, Q_W + KV_W:Q_W + 2 * KV_W].reshape(DB, T, KV_HEADS, HEAD_DIM))
        mkv = matmul(mem_b, w_mem_b[l], B * MEM_LEN, 512).reshape(B, MEM_LEN, 2 * MEM_W)
        mem_k_p.append(mkv[:, :, :MEM_W].reshape(B, MEM_LEN, MEM_HEADS, HEAD_DIM))
        mem_v_p.append(mkv[:, :, MEM_W:].reshape(B, MEM_LEN, MEM_HEADS, HEAD_DIM))
        cross_p = mem_attend(p, 0, B, tq_mem, S // tq_mem, mkv, mkv,
                             lambda n, t: (n, 0, 0), lambda n, t: (n, 0, 1), (1, MEM_LEN, MEM_W))
        cross_s = mem_attend(p, sample_block0, DB, T, 1, cmem_k, cmem_v,
                             lambda n, t: (l, n, 0, 0), lambda n, t: (l, n, 0, 0),
                             (1, 1, MEM_LEN * MEM_HEADS, HEAD_DIM))
        x, xb = wo_ln(self_p, cross_p, self_s, cross_s, x, w_o_b[l, :Q_W], w_o_b[l, Q_W:],
                      ln1_g[l].reshape(1, D_MODEL), ln1_b[l].reshape(1, D_MODEL), tm_wo)
        e_idx, gate = router(x, w_router_t, router_bias_b, tm_rt)
        dest, row_tok, blk_e, n_used = moe_dispatch(e_idx)
        xs = gather_rows(x, row_tok, n_used, MOE_TM, bf16)
        yb = moe_ffn_sorted(xs, blk_e, n_used, w_gate_up, w_down, l)
        ya = gather_rows(yb, dest.T.reshape(-1), all_blocks, GATHER_ROWS, f32)
        x, xb = combine_ln(x, ya, gate.T,
                           ln2_g[l].reshape(1, D_MODEL), ln2_b[l].reshape(1, D_MODEL), tm_ln)
    return (x[:TP].reshape(B, S, D_MODEL), x[TP:].reshape(DB, T, D_MODEL),
            jnp.stack(win_k_p), jnp.stack(win_v_p), jnp.stack(win_k_s), jnp.stack(win_v_s),
            jnp.stack(diff_k_p), jnp.stack(diff_v_p), jnp.stack(diff_k_s), jnp.stack(diff_v_s),
            jnp.stack(mem_k_p), jnp.stack(mem_v_p))
```

```python
import functools
import math

import jax
import jax.numpy as jnp
from jax import lax
from jax.experimental import pallas as pl
from jax.experimental.pallas import tpu as pltpu

f32 = jnp.float32
bf16 = jnp.bfloat16
i32 = jnp.int32

D_MODEL = 2048
DEPTH = 4
PAST_LEN = 16384
PAGE_SIZE = 128
HEAD_DIM = 128
N_HEADS = 12
KV_HEADS = 4
REP = N_HEADS // KV_HEADS
DIFF_DIM = HEAD_DIM // 2
MEM_HEADS = 4
MEM_LEN = 256
WINDOW = 128
NUM_BUCKETS = 32
MAX_DISTANCE = 128
N_EXPERTS = 16
N_GROUPS = 4
EXPERTS_PER_GROUP = N_EXPERTS // N_GROUPS
TOP_K = 2
D_EXPERT = D_MODEL // 2
Q_W = N_HEADS * HEAD_DIM
KV_W = KV_HEADS * HEAD_DIM
MEM_W = MEM_HEADS * HEAD_DIM
IN_W = Q_W + 2 * KV_W + MEM_W
ALPHA = (2.0 * DEPTH) ** 0.25
LN_EPS = 1e-5
RMS_EPS = 1e-5
NEG_INF = -1e30
LOG2_E = math.log2(math.e)

VMEM_LIMIT_BYTES = 56 * 1024 * 1024

K_COL = Q_W // HEAD_DIM
V_COL = (Q_W + KV_W) // HEAD_DIM
QM_COL = (Q_W + 2 * KV_W) // HEAD_DIM

DIFF_TQ = 256
DIFF_TK = 256
DIFF_FAR_CHUNKS = 2
PAGES_PER_STEP = 16
MOE_TM = 512


def _params(n_axes):
    return pltpu.CompilerParams(dimension_semantics=("arbitrary",) * n_axes,
                                vmem_limit_bytes=VMEM_LIMIT_BYTES)


def _dot_nt(a, b):
    return lax.dot_general(a, b, (((1,), (1,)), ((), ())), preferred_element_type=f32)


def _dot(a, b):
    return jnp.dot(a, b, preferred_element_type=f32)


def _mm_kernel(a_ref, b_ref, o_ref):
    o_ref[...] = _dot(a_ref[...], b_ref[...])


def matmul(a, b, tm, tn):
    M, K = a.shape
    N = b.shape[1]
    return pl.pallas_call(
        _mm_kernel,
        grid=(M // tm, N // tn),
        in_specs=[pl.BlockSpec((tm, K), lambda i, j: (i, 0)),
                  pl.BlockSpec((K, tn), lambda i, j: (0, j))],
        out_specs=pl.BlockSpec((tm, tn), lambda i, j: (i, j)),
        out_shape=jax.ShapeDtypeStruct((M, N), f32),
        compiler_params=_params(2),
        name="matmul",
    )(a, b)


def _layer_norm(y, g, b):
    mu = jnp.mean(y, -1, keepdims=True)
    yc = y - mu
    var = jnp.mean(yc * yc, -1, keepdims=True)
    return yc * lax.rsqrt(var + LN_EPS) * g + b


def _wo_ln_kernel(msp_ref, mcp_ref, mss_ref, mcs_ref, x_ref, ws_ref, wc_ref, g_ref, b_ref, o_ref, ob_ref,
                  *, n_prompt_tiles):
    def body(ms, mc):
        f = _dot(ms, ws_ref[...]) + _dot(mc, wc_ref[...])
        out = _layer_norm(ALPHA * x_ref[...] + f, g_ref[...], b_ref[...])
        o_ref[...] = out
        ob_ref[...] = out.astype(bf16)

    is_prompt = pl.program_id(0) < n_prompt_tiles

    @pl.when(is_prompt)
    def _():
        body(msp_ref[...], mcp_ref[...])

    @pl.when(jnp.logical_not(is_prompt))
    def _():
        body(mss_ref[...].astype(bf16), mcs_ref[...].astype(bf16))


def wo_ln(self_p, cross_p, self_s, cross_s, x, w_self, w_cross, g, b, tm):
    M = x.shape[0]
    n_p = self_p.shape[0] // tm
    assert self_p.shape[0] % tm == 0 and self_s.shape[0] % tm == 0
    row = lambda i: (i, 0)
    prow = lambda i: (jnp.minimum(i, n_p - 1), 0)
    srow = lambda i: (jnp.maximum(i - n_p, 0), 0)
    const = lambda i: (0, 0)
    return pl.pallas_call(
        functools.partial(_wo_ln_kernel, n_prompt_tiles=n_p),
        grid=(M // tm,),
        in_specs=[pl.BlockSpec((tm, Q_W), prow), pl.BlockSpec((tm, MEM_W), prow),
                  pl.BlockSpec((tm, Q_W), srow), pl.BlockSpec((tm, MEM_W), srow),
                  pl.BlockSpec((tm, D_MODEL), row),
                  pl.BlockSpec((Q_W, D_MODEL), const), pl.BlockSpec((MEM_W, D_MODEL), const),
                  pl.BlockSpec((1, D_MODEL), const), pl.BlockSpec((1, D_MODEL), const)],
        out_specs=[pl.BlockSpec((tm, D_MODEL), row), pl.BlockSpec((tm, D_MODEL), row)],
        out_shape=[jax.ShapeDtypeStruct((M, D_MODEL), f32),
                   jax.ShapeDtypeStruct((M, D_MODEL), bf16)],
        compiler_params=_params(1),
        name="wo_ln",
    )(self_p, cross_p, self_s, cross_s, x, w_self, w_cross, g, b)


def _pipelined_row_gather(i, n_used, idx_first_ref, idx_next_ref, src_ref, buf, sem, rows):
    def row_copy(slot, r, src_row):
        return pltpu.make_async_copy(src_ref.at[pl.ds(src_row, 1)], buf.at[slot, pl.ds(r, 1)], sem.at[slot])

    def start_block(slot, idx_ref):
        def body(r, carry):
            row_copy(slot, r, idx_ref[0, 0, r]).start()
            return carry
        lax.fori_loop(0, rows, body, 0, unroll=8)

    @pl.when(i == 0)
    def _():
        start_block(0, idx_first_ref)

    @pl.when(i + 1 < n_used)
    def _():
        start_block((i + 1) % 2, idx_next_ref)

    @pl.when(i < n_used)
    def _():
        def body(r, carry):
            row_copy(i % 2, r, 0).wait()
            return carry
        lax.fori_loop(0, rows, body, 0, unroll=8)


def _gather_idx_specs(rows, n_blocks):
    first = lambda i, *_: (0, 0, 0)
    nxt = lambda i, *_: (jnp.minimum(i + 1, n_blocks - 1), 0, 0)
    return [pl.BlockSpec((1, 1, rows), first, memory_space=pltpu.SMEM),
            pl.BlockSpec((1, 1, rows), nxt, memory_space=pltpu.SMEM)]


def _gather_rows_kernel(nu_ref, idx_first_ref, idx_next_ref, src_ref, o_ref, buf, sem, *, rows):
    i = pl.program_id(0)
    used = i < nu_ref[0]
    _pipelined_row_gather(i, nu_ref[0], idx_first_ref, idx_next_ref, src_ref, buf, sem, rows)

    @pl.when(used)
    def _():
        o_ref[...] = buf[i % 2].astype(o_ref.dtype)

    @pl.when(jnp.logical_not(used))
    def _():
        o_ref[...] = jnp.zeros(o_ref.shape, o_ref.dtype)


def gather_rows(src, idx, n_used, rows, out_dtype):
    n_blocks = idx.shape[0] // rows
    D = src.shape[1]
    idx3 = idx.reshape(n_blocks, 1, rows)
    return pl.pallas_call(
        functools.partial(_gather_rows_kernel, rows=rows),
        grid_spec=pltpu.PrefetchScalarGridSpec(
            num_scalar_prefetch=1,
            grid=(n_blocks,),
            in_specs=_gather_idx_specs(rows, n_blocks) + [pl.BlockSpec(memory_space=pl.ANY)],
            out_specs=pl.BlockSpec((rows, D), lambda i, nu: (i, 0)),
            scratch_shapes=[pltpu.VMEM((2, rows, D), src.dtype), pltpu.SemaphoreType.DMA((2,))],
        ),
        out_shape=jax.ShapeDtypeStruct((n_blocks * rows, D), out_dtype),
        compiler_params=_params(1),
        name="gather_rows",
    )(n_used, idx3, idx3, src)


def _combine_ln_kernel(idx_first_ref, idx_next_ref, y_ref, x_ref, gate_ref, g_ref, b_ref, o_ref, ob_ref,
                       buf, sem, *, tm):
    i = pl.program_id(0)
    _pipelined_row_gather(i, pl.num_programs(0), idx_first_ref, idx_next_ref, y_ref, buf, sem, TOP_K * tm)
    gate = gate_ref[...]
    y = buf[i % 2]
    f = y[:tm] * gate[:, 0:1] + y[tm:] * gate[:, 1:2]
    out = _layer_norm(ALPHA * x_ref[...] + f, g_ref[...], b_ref[...])
    o_ref[...] = out
    ob_ref[...] = out.astype(bf16)


def combine_ln(x, yb, dest, gate_t, g, b, tm):
    M = x.shape[0]
    n_tiles = M // tm
    idx3 = jnp.swapaxes(dest.reshape(n_tiles, tm, TOP_K), 1, 2).reshape(n_tiles, 1, TOP_K * tm)
    row = lambda i: (i, 0)
    const = lambda i: (0, 0)
    return pl.pallas_call(
        functools.partial(_combine_ln_kernel, tm=tm),
        grid=(n_tiles,),
        in_specs=_gather_idx_specs(TOP_K * tm, n_tiles) + [
            pl.BlockSpec(memory_space=pl.ANY),
            pl.BlockSpec((tm, D_MODEL), row), pl.BlockSpec((tm, TOP_K), row),
            pl.BlockSpec((1, D_MODEL), const), pl.BlockSpec((1, D_MODEL), const)],
        out_specs=[pl.BlockSpec((tm, D_MODEL), row), pl.BlockSpec((tm, D_MODEL), row)],
        out_shape=[jax.ShapeDtypeStruct((M, D_MODEL), f32),
                   jax.ShapeDtypeStruct((M, D_MODEL), bf16)],
        scratch_shapes=[pltpu.VMEM((2, TOP_K * tm, D_MODEL), f32), pltpu.SemaphoreType.DMA((2,))],
        compiler_params=_params(1),
        name="combine_ln",
    )(idx3, idx3, yb, x, gate_t, g, b)


def _router_kernel(x_ref, wt_ref, bias_ref, e_ref, gate_ref):
    logits = lax.dot_general(wt_ref[...], x_ref[...], (((1,), (1,)), ((), ())),
                             preferred_element_type=f32, precision=lax.Precision.HIGHEST)
    scores = jax.nn.sigmoid(logits)
    sel = scores + bias_ref[...][:, 0:1]
    n = EXPERTS_PER_GROUP
    rows = [sel[e:e + 1, :] for e in range(N_EXPERTS)]
    srows = [scores[e:e + 1, :] for e in range(N_EXPERTS)]
    gscore = []
    for g in range(N_GROUPS):
        v = rows[g * n:(g + 1) * n]
        best = None
        for a in range(n):
            for b in range(a + 1, n):
                pair = v[a] + v[b]
                best = pair if best is None else jnp.maximum(best, pair)
        gscore.append(best)
    g_idx = jnp.zeros_like(gscore[0], dtype=i32)
    best = gscore[0]
    for g in range(1, N_GROUPS):
        take = gscore[g] > best
        g_idx = jnp.where(take, g, g_idx)
        best = jnp.where(take, gscore[g], best)
    ing, sg = [], []
    for a in range(n):
        va, sa = rows[a], srows[a]
        for g in range(1, N_GROUPS):
            va = jnp.where(g_idx == g, rows[g * n + a], va)
            sa = jnp.where(g_idx == g, srows[g * n + a], sa)
        ing.append(va)
        sg.append(sa)
    l_idx = [jnp.zeros_like(g_idx), jnp.zeros_like(g_idx)]
    w = [jnp.zeros_like(best), jnp.zeros_like(best)]
    for a in range(n):
        rank = jnp.zeros_like(g_idx)
        for b in range(n):
            if b == a:
                continue
            ahead = (ing[b] > ing[a]) | ((ing[b] == ing[a]) & (b < a))
            rank = rank + ahead.astype(i32)
        for k in range(TOP_K):
            hit = rank == k
            l_idx[k] = jnp.where(hit, a, l_idx[k])
            w[k] = jnp.where(hit, sg[a], w[k])
    tot = w[0] + w[1]
    e_ref[...] = jnp.concatenate([g_idx * n + l_idx[0], g_idx * n + l_idx[1]], axis=0)
    gate_ref[...] = jnp.concatenate([w[0] / tot, w[1] / tot], axis=0)


def router(x, w_router_t, router_bias, tm):
    M = x.shape[0]
    return pl.pallas_call(
        _router_kernel,
        grid=(M // tm,),
        in_specs=[pl.BlockSpec((tm, D_MODEL), lambda i: (i, 0)),
                  pl.BlockSpec((N_EXPERTS, D_MODEL), lambda i: (0, 0)),
                  pl.BlockSpec((N_EXPERTS, 128), lambda i: (0, 0))],
        out_specs=[pl.BlockSpec((TOP_K, tm), lambda i: (0, i)),
                   pl.BlockSpec((TOP_K, tm), lambda i: (0, i))],
        out_shape=[jax.ShapeDtypeStruct((TOP_K, M), i32),
                   jax.ShapeDtypeStruct((TOP_K, M), f32)],
        compiler_params=_params(1),
        name="router",
    )(x, w_router_t, router_bias)


def _moe_up_kernel(be_ref, nu_ref, x_ref, wg_ref, wu_ref, h_ref):
    used = pl.program_id(1) < nu_ref[0]

    @pl.when(used)
    def _():
        x = x_ref[...]
        g = _dot(x, wg_ref[0, 0].astype(bf16))
        u = _dot(x, wu_ref[0, 0].astype(bf16))
        h_ref[...] = (g * jax.nn.sigmoid(g) * u).astype(bf16)

    @pl.when(jnp.logical_not(used))
    def _():
        h_ref[...] = jnp.zeros(h_ref.shape, h_ref.dtype)


def _moe_down_kernel(be_ref, nu_ref, h_ref, wd_ref, y_ref):
    used = pl.program_id(1) < nu_ref[0]

    @pl.when(used)
    def _():
        y_ref[...] = _dot(h_ref[...], wd_ref[0, 0].astype(bf16))

    @pl.when(jnp.logical_not(used))
    def _():
        y_ref[...] = jnp.zeros(y_ref.shape, y_ref.dtype)


def moe_ffn_sorted(xs, blk_e, n_used, w_gu, w_dn, l, tn_up=512, tn_dn=1024):
    R = xs.shape[0]
    n_blocks = R // MOE_TM
    nj = D_EXPERT // tn_up
    blk = lambda j, i, be, nu: jnp.minimum(i, nu[0] - 1)
    h = pl.pallas_call(
        _moe_up_kernel,
        grid_spec=pltpu.PrefetchScalarGridSpec(
            num_scalar_prefetch=2,
            grid=(nj, n_blocks),
            in_specs=[pl.BlockSpec((MOE_TM, D_MODEL), lambda j, i, be, nu: (blk(j, i, be, nu), 0)),
                      pl.BlockSpec((1, 1, D_MODEL, tn_up), lambda j, i, be, nu: (l, be[i], 0, j)),
                      pl.BlockSpec((1, 1, D_MODEL, tn_up), lambda j, i, be, nu: (l, be[i], 0, nj + j))],
            out_specs=pl.BlockSpec((MOE_TM, tn_up), lambda j, i, be, nu: (i, j)),
        ),
        out_shape=jax.ShapeDtypeStruct((R, D_EXPERT), bf16),
        compiler_params=_params(2),
        name="moe_up",
    )(blk_e, n_used, xs, w_gu, w_gu)
    nj2 = D_MODEL // tn_dn
    return pl.pallas_call(
        _moe_down_kernel,
        grid_spec=pltpu.PrefetchScalarGridSpec(
            num_scalar_prefetch=2,
            grid=(nj2, n_blocks),
            in_specs=[pl.BlockSpec((MOE_TM, D_EXPERT), lambda j, i, be, nu: (blk(j, i, be, nu), 0)),
                      pl.BlockSpec((1, 1, D_EXPERT, tn_dn), lambda j, i, be, nu: (l, be[i], 0, j))],
            out_specs=pl.BlockSpec((MOE_TM, tn_dn), lambda j, i, be, nu: (i, j)),
        ),
        out_shape=jax.ShapeDtypeStruct((R, D_MODEL), f32),
        compiler_params=_params(2),
        name="moe_down",
    )(blk_e, n_used, h, w_dn)


def moe_dispatch(e_idx):
    T = e_idx.shape[1]
    A = T * TOP_K
    flat_e = e_idx.T.reshape(A)
    onehot = (flat_e[:, None] == jnp.arange(N_EXPERTS, dtype=i32)[None, :]).astype(i32)
    csum = jnp.cumsum(onehot, axis=0)
    rank = jnp.take_along_axis(csum, flat_e[:, None], axis=1)[:, 0] - 1
    counts = csum[-1]
    padded = (counts + MOE_TM - 1) // MOE_TM * MOE_TM
    pad_ends = jnp.cumsum(padded)
    pad_starts = pad_ends - padded
    dest = pad_starts[flat_e] + rank
    n_blocks = -(-A // MOE_TM) + N_EXPERTS
    tok = jnp.arange(A, dtype=i32) // TOP_K
    row_tok = (jnp.arange(n_blocks * MOE_TM, dtype=i32) % T).at[dest].set(tok)
    n_used = (pad_ends[-1] // MOE_TM).astype(i32)
    blk_start = jnp.minimum(jnp.arange(n_blocks, dtype=i32), n_used - 1) * MOE_TM
    blk_e = jnp.minimum(jnp.searchsorted(pad_ends, blk_start, side='right'), N_EXPERTS - 1).astype(i32)
    return dest.reshape(T, TOP_K), row_tok, blk_e, n_used.reshape(1)


def _t5_bucket(rel):
    n = jnp.maximum(-rel, 0)
    max_exact = NUM_BUCKETS // 2
    nf = jnp.maximum(n, 1).astype(f32)
    large = max_exact + (jnp.log(nf / max_exact) / math.log(MAX_DISTANCE / max_exact)
                         * (NUM_BUCKETS - max_exact)).astype(i32)
    large = jnp.minimum(large, NUM_BUCKETS - 1)
    return jnp.where(n < max_exact, n, large)


def _bias_tile(table, rel, mask, shift=None):
    t = table.astype(f32)
    if shift is not None:
        t = t - shift[None, :]
    bucket = jnp.where(mask, _t5_bucket(rel), -1)[None]
    b = jnp.full((table.shape[1],) + rel.shape, NEG_INF, f32)
    for i in range(NUM_BUCKETS):
        b = jnp.where(bucket == i, t[i][:, None, None], b)
    return b


def _group_rows(b, comps):
    H, Tq, Tk = b.shape
    b = b.reshape(KV_HEADS, REP, 1, Tq, Tk)
    b = jnp.broadcast_to(b, (KV_HEADS, REP, comps, Tq, Tk))
    return b.reshape(KV_HEADS, REP * comps * Tq, Tk)


def _mem_attn_kernel(q_ref, k_ref, v_ref, o_ref, *, interleaved):
    scale = HEAD_DIM ** -0.5
    for h in range(MEM_HEADS):
        sl = slice(h * HEAD_DIM, (h + 1) * HEAD_DIM)
        if interleaved:
            k = k_ref[0, 0, pl.ds(h, MEM_LEN, stride=MEM_HEADS), :]
            v = v_ref[0, 0, pl.ds(h, MEM_LEN, stride=MEM_HEADS), :]
        else:
            k = k_ref[0, :, sl]
            v = v_ref[0, :, sl]
        s = _dot_nt(q_ref[:, sl].astype(bf16), k.astype(bf16)) * scale
        m = jnp.max(s, -1, keepdims=True)
        e = jnp.exp(s - m)
        p = e * (1.0 / jnp.sum(e, -1, keepdims=True))
        o_ref[:, sl] = _dot(p.astype(bf16), v.astype(bf16)).astype(o_ref.dtype)


def mem_attend(p, row_block0, n_seq, tq, tiles_per_seq, mem_k, mem_v, k_map, v_map, kv_block):
    rows = n_seq * tiles_per_seq * tq
    return pl.pallas_call(
        functools.partial(_mem_attn_kernel, interleaved=len(kv_block) == 4),
        grid=(n_seq, tiles_per_seq),
        in_specs=[pl.BlockSpec((tq, MEM_W), lambda n, i: (row_block0 + n * tiles_per_seq + i, QM_COL // MEM_HEADS)),
                  pl.BlockSpec(kv_block, k_map), pl.BlockSpec(kv_block, v_map)],
        out_specs=pl.BlockSpec((tq, MEM_W), lambda n, i: (n * tiles_per_seq + i, 0)),
        out_shape=jax.ShapeDtypeStruct((rows, MEM_W), f32 if tq < 16 else bf16),
        compiler_params=_params(2),
        name="mem_attn",
    )(p, mem_k, mem_v)


def _softmax_sink_pv(s, sink_col, v):
    m = jnp.maximum(jnp.max(s, -1, keepdims=True), sink_col)
    e = jnp.exp(s - m)
    den = jnp.sum(e, -1, keepdims=True) + jnp.exp(sink_col - m)
    p = e * (1.0 / den)
    return _dot(p.astype(bf16), v)


def _sink_col(sink_ref, g, tq):
    return jnp.concatenate([jnp.full((tq, 1), sink_ref[REP * g + r], f32) for r in range(REP)], axis=0)


def _swa_prompt_kernel(q_ref, kc_ref, kp_ref, vc_ref, vp_ref, bias_ref, sink_ref, o_ref):
    i = pl.program_id(1)
    scale = HEAD_DIM ** -0.5
    for g in range(KV_HEADS):
        sl = slice(g * HEAD_DIM, (g + 1) * HEAD_DIM)
        kk = jnp.concatenate([kp_ref[:, sl], kc_ref[:, sl]], axis=0).astype(bf16)
        vv = jnp.concatenate([vp_ref[:, sl], vc_ref[:, sl]], axis=0).astype(bf16)
        q3 = jnp.concatenate([q_ref[:, (REP * g + r) * HEAD_DIM:(REP * g + r + 1) * HEAD_DIM]
                              for r in range(REP)], axis=0).astype(bf16)
        s = _dot_nt(q3, kk) * scale + bias_ref[g]
        col = lax.broadcasted_iota(i32, s.shape, 1)
        s = jnp.where((col >= WINDOW) | (i > 0), s, NEG_INF)
        o = _softmax_sink_pv(s, _sink_col(sink_ref, g, WINDOW), vv)
        for r in range(REP):
            o_ref[:, (REP * g + r) * HEAD_DIM:(REP * g + r + 1) * HEAD_DIM] = (
                o[r * WINDOW:(r + 1) * WINDOW].astype(o_ref.dtype))


def swa_prompt(p, B, S, bias, sink):
    nb = S // WINDOW
    cur = lambda c: (lambda b, i: (b * nb + i, c))
    prev = lambda c: (lambda b, i: (b * nb + jnp.maximum(i - 1, 0), c))
    kc, vc = K_COL // KV_HEADS, V_COL // KV_HEADS
    return pl.pallas_call(
        _swa_prompt_kernel,
        grid=(B, nb),
        in_specs=[pl.BlockSpec((WINDOW, Q_W), cur(0)),
                  pl.BlockSpec((WINDOW, KV_W), cur(kc)), pl.BlockSpec((WINDOW, KV_W), prev(kc)),
                  pl.BlockSpec((WINDOW, KV_W), cur(vc)), pl.BlockSpec((WINDOW, KV_W), prev(vc)),
                  pl.BlockSpec((KV_HEADS, REP * WINDOW, 2 * WINDOW), lambda b, i: (0, 0, 0)),
                  pl.BlockSpec(memory_space=pltpu.SMEM)],
        out_specs=pl.BlockSpec((WINDOW, Q_W), lambda b, i: (b * nb + i, 0)),
        out_shape=jax.ShapeDtypeStruct((B * S, Q_W), bf16),
        compiler_params=_params(2),
        name="swa_prompt",
    )(p, p, p, p, p, bias, sink)


def _swa_sample_kernel(q_ref, kn_ref, vn_ref, wk_ref, wv_ref, bias_ref, sink_ref, o_ref, nk_ref, nv_ref, *, T):
    scale = HEAD_DIM ** -0.5
    wk = wk_ref[...].reshape(WINDOW, KV_W)
    wv = wv_ref[...].reshape(WINDOW, KV_W)
    kn = kn_ref[...]
    vn = vn_ref[...]
    pad = jnp.zeros((WINDOW - T, HEAD_DIM), f32)
    for g in range(KV_HEADS):
        sl = slice(g * HEAD_DIM, (g + 1) * HEAD_DIM)
        kk = jnp.concatenate([wk[:, sl], kn[:, sl], pad], axis=0).astype(bf16)
        vv = jnp.concatenate([wv[:, sl], vn[:, sl], pad], axis=0).astype(bf16)
        q3 = jnp.concatenate([q_ref[:, (REP * g + r) * HEAD_DIM:(REP * g + r + 1) * HEAD_DIM]
                              for r in range(REP)], axis=0).astype(bf16)
        s = _dot_nt(q3, kk) * scale + bias_ref[g]
        o = _softmax_sink_pv(s, _sink_col(sink_ref, g, T), vv)
        for r in range(REP):
            o_ref[:, (REP * g + r) * HEAD_DIM:(REP * g + r + 1) * HEAD_DIM] = o[r * T:(r + 1) * T]
    nk_ref[0, 0:WINDOW - T, :] = wk[T:, :]
    nk_ref[0, WINDOW - T:, :] = kn
    nv_ref[0, 0:WINDOW - T, :] = wv[T:, :]
    nv_ref[0, WINDOW - T:, :] = vn


def swa_sample(p, row_block0, DB, T, win_k, win_v, li, bias, sink):
    kc, vc = K_COL // KV_HEADS, V_COL // KV_HEADS
    win = lambda n: (li, n, 0, 0)
    return pl.pallas_call(
        functools.partial(_swa_sample_kernel, T=T),
        grid=(DB,),
        in_specs=[pl.BlockSpec((T, Q_W), lambda n: (row_block0 + n, 0)),
                  pl.BlockSpec((T, KV_W), lambda n: (row_block0 + n, kc)),
                  pl.BlockSpec((T, KV_W), lambda n: (row_block0 + n, vc)),
                  pl.BlockSpec((1, 1, WINDOW, KV_W), win), pl.BlockSpec((1, 1, WINDOW, KV_W), win),
                  pl.BlockSpec((KV_HEADS, REP * T, 2 * WINDOW), lambda n: (0, 0, 0)),
                  pl.BlockSpec(memory_space=pltpu.SMEM)],
        out_specs=[pl.BlockSpec((T, Q_W), lambda n: (n, 0)),
                   pl.BlockSpec((1, WINDOW, KV_W), lambda n: (n, 0, 0)),
                   pl.BlockSpec((1, WINDOW, KV_W), lambda n: (n, 0, 0))],
        out_shape=[jax.ShapeDtypeStruct((DB * T, Q_W), f32),
                   jax.ShapeDtypeStruct((DB, WINDOW, KV_W), f32),
                   jax.ShapeDtypeStruct((DB, WINDOW, KV_W), f32)],
        compiler_params=_params(1),
        name="swa_sample",
    )(p, p, p, win_k, win_v, bias, sink)


def _split_components(q, scale):
    lane = lax.broadcasted_iota(i32, q.shape, 1)
    qs = q * scale
    return [jnp.where(lane < DIFF_DIM, qs, 0.0), jnp.where(lane >= DIFF_DIM, qs, 0.0)]


def _diff_finish(o0, o1, lam, gain, lam_init, axis):
    o = o0 - lam * o1
    return o * lax.rsqrt(jnp.mean(o * o, axis, keepdims=True) + RMS_EPS) * gain * (1.0 - lam_init)


def _diff_prompt_kernel(q_ref, k_ref, v_ref, bias_ref, lam_ref, gain_ref, o_ref,
                        kb_scr, vt_scr, qpt_scr, m_scr, l_scr, acc_scr, *, lam_init):
    i = pl.program_id(2)
    tq, tk = DIFF_TQ, DIFF_TK
    n_chunks = k_ref.shape[0] // tk

    @pl.when(i == 0)
    def _():
        for c in range(n_chunks):
            kb_scr[c] = k_ref[c * tk:(c + 1) * tk, :].astype(bf16)
            vt_scr[c] = v_ref[c * tk:(c + 1) * tk, :].T.astype(bf16)

    pieces = []
    for r in range(REP):
        for part in _split_components(q_ref[:, r * HEAD_DIM:(r + 1) * HEAD_DIM], DIFF_DIM ** -0.5 * LOG2_E):
            pieces.append(part.T)
    qpt_scr[...] = jnp.concatenate(pieces, axis=1).astype(bf16)
    m_scr[...] = jnp.full(m_scr.shape, NEG_INF, f32)
    l_scr[...] = jnp.zeros(l_scr.shape, f32)
    acc_scr[...] = jnp.zeros(acc_scr.shape, f32)

    def chunk(c, n, bias):
        if n == 1:
            kb, vt = kb_scr[c], vt_scr[c]
        else:
            kb = kb_scr[pl.ds(c, n)].reshape(n * tk, HEAD_DIM)
            vt = jnp.concatenate([vt_scr[c + d] for d in range(n)], axis=1)
        s = _dot(kb, qpt_scr[...])
        if bias is not None:
            s = s + bias
        m_old = m_scr[...]
        m_new = jnp.maximum(m_old, jnp.max(s, axis=0, keepdims=True))
        p = jnp.exp2(s - m_new)
        alpha = jnp.exp2(m_old - m_new)
        l_scr[...] = alpha * l_scr[...] + jnp.sum(p, axis=0, keepdims=True)
        acc_scr[...] = alpha * acc_scr[...] + _dot(vt, p.astype(bf16))
        m_scr[...] = m_new

    step = tq // tk
    for d in range(step):
        chunk(i * step + d, 1, bias_ref[0, tq + d * tk:tq + (d + 1) * tk, :])

    @pl.when(i > 0)
    def _():
        for d in range(step):
            chunk((i - 1) * step + d, 1, bias_ref[0, d * tk:(d + 1) * tk, :])

    n_far = jnp.maximum(i - 1, 0) * step
    n_group = n_far // DIFF_FAR_CHUNKS

    def far_group(j, carry):
        chunk(j * DIFF_FAR_CHUNKS, DIFF_FAR_CHUNKS, None)
        return carry

    def far_single(c, carry):
        chunk(c, 1, None)
        return carry

    lax.fori_loop(0, n_group, far_group, 0)
    lax.fori_loop(n_group * DIFF_FAR_CHUNKS, n_far, far_single, 0)

    acc = acc_scr[...] * (1.0 / l_scr[...])
    lam = lam_ref[0]
    for r in range(REP):
        o = _diff_finish(acc[:, (2 * r) * tq:(2 * r + 1) * tq], acc[:, (2 * r + 1) * tq:(2 * r + 2) * tq],
                         lam, gain_ref[...], lam_init, 0)
        o_ref[:, r * HEAD_DIM:(r + 1) * HEAD_DIM] = o.T.astype(o_ref.dtype)


def diff_prompt(p, B, S, bias_t, lam, gain_col, lam_init):
    nq = S // DIFF_TQ
    cols = 2 * REP * DIFF_TQ
    n_chunks = S // DIFF_TK
    return pl.pallas_call(
        functools.partial(_diff_prompt_kernel, lam_init=lam_init),
        grid=(B, KV_HEADS, nq),
        in_specs=[pl.BlockSpec((DIFF_TQ, REP * HEAD_DIM), lambda b, g, i: (b * nq + i, g)),
                  pl.BlockSpec((S, HEAD_DIM), lambda b, g, i: (b, K_COL + g)),
                  pl.BlockSpec((S, HEAD_DIM), lambda b, g, i: (b, V_COL + g)),
                  pl.BlockSpec((1, 2 * DIFF_TQ, cols), lambda b, g, i: (g, 0, 0)),
                  pl.BlockSpec(memory_space=pltpu.SMEM),
                  pl.BlockSpec((HEAD_DIM, 1), lambda b, g, i: (0, 0))],
        out_specs=pl.BlockSpec((DIFF_TQ, REP * HEAD_DIM), lambda b, g, i: (b * nq + i, g)),
        out_shape=jax.ShapeDtypeStruct((B * S, Q_W), bf16),
        scratch_shapes=[pltpu.VMEM((n_chunks, DIFF_TK, HEAD_DIM), bf16),
                        pltpu.VMEM((n_chunks, HEAD_DIM, DIFF_TK), bf16),
                        pltpu.VMEM((HEAD_DIM, cols), bf16),
                        pltpu.VMEM((1, cols), f32),
                        pltpu.VMEM((1, cols), f32),
                        pltpu.VMEM((HEAD_DIM, cols), f32)],
        compiler_params=_params(3),
        name="diff_prompt",
    )(p, p, p, bias_t, lam, gain_col)


def _diff_sample_kernel(pt_ref, q_ref, kn_ref, vn_ref, *rest, T, lam_init, n_steps):
    npg = PAGES_PER_STEP
    k_refs, v_refs = rest[:npg], rest[npg:2 * npg]
    bias_last_ref, bias_new_ref, lam_ref, gain_ref, o_ref, qp_scr, m_scr, l_scr, acc_scr = rest[2 * npg:]
    s_id = pl.program_id(1)
    rows = 2 * REP * T

    @pl.when(s_id == 0)
    def _():
        pieces = []
        for h in range(N_HEADS):
            pieces += _split_components(q_ref[:, h * HEAD_DIM:(h + 1) * HEAD_DIM], DIFF_DIM ** -0.5)
        qp_scr[...] = jnp.concatenate(pieces, axis=0)
        m_scr[...] = jnp.full(m_scr.shape, NEG_INF, f32)
        l_scr[...] = jnp.zeros(l_scr.shape, f32)
        acc_scr[...] = jnp.zeros(acc_scr.shape, f32)

    def update(s, vs):
        m_old = m_scr[...]
        m_new = jnp.maximum(m_old, jnp.max(s, -1, keepdims=True))
        p = jnp.exp(s - m_new)
        alpha = jnp.exp(m_old - m_new)
        l_scr[...] = alpha * l_scr[...] + jnp.sum(p, -1, keepdims=True)
        pb = p.astype(bf16)
        pv = jnp.concatenate([_dot(pb[g * rows:(g + 1) * rows], vs[g]) for g in range(KV_HEADS)], axis=0)
        acc_scr[...] = alpha * acc_scr[...] + pv
        m_scr[...] = m_new

    def scores(ks):
        qp = qp_scr[...].astype(bf16)
        return jnp.concatenate([_dot_nt(qp[g * rows:(g + 1) * rows], ks[g]) for g in range(KV_HEADS)], axis=0)

    def head_rows(refs, g):
        return jnp.concatenate([r[0, 0, pl.ds(g, PAGE_SIZE, stride=KV_HEADS), :] for r in refs],
                               axis=0).astype(bf16)

    last = s_id == n_steps - 1
    s = scores([head_rows(k_refs, g) for g in range(KV_HEADS)])
    tail = s[:, (npg - 1) * PAGE_SIZE:] + jnp.where(last, bias_last_ref[...], 0.0)
    s = jnp.concatenate([s[:, :(npg - 1) * PAGE_SIZE], tail], axis=1)
    update(s, [head_rows(v_refs, g) for g in range(KV_HEADS)])

    @pl.when(last)
    def _():
        pad = jnp.zeros((PAGE_SIZE - T, HEAD_DIM), f32)
        kn = [jnp.concatenate([kn_ref[:, g * HEAD_DIM:(g + 1) * HEAD_DIM], pad], axis=0).astype(bf16)
              for g in range(KV_HEADS)]
        vn = [jnp.concatenate([vn_ref[:, g * HEAD_DIM:(g + 1) * HEAD_DIM], pad], axis=0).astype(bf16)
              for g in range(KV_HEADS)]
        update(scores(kn) + bias_new_ref[...], vn)
        acc = acc_scr[...] * (1.0 / l_scr[...])
        lam = lam_ref[0]
        for h in range(N_HEADS):
            o = _diff_finish(acc[(2 * h) * T:(2 * h + 1) * T], acc[(2 * h + 1) * T:(2 * h + 2) * T],
                             lam, gain_ref[...], lam_init, -1)
            o_ref[:, h * HEAD_DIM:(h + 1) * HEAD_DIM] = o


def diff_sample(p, row_block0, DB, T, cache_k, cache_v, li, page_table, bias_last, bias_new, lam, gain, lam_init):
    n_pages = PAST_LEN // PAGE_SIZE
    npg = PAGES_PER_STEP
    n_steps = n_pages // npg
    rows = 2 * N_HEADS * T
    kc, vc = K_COL // KV_HEADS, V_COL // KV_HEADS

    def page(j):
        return lambda n, s, pt: (li, pt[n * n_pages + s * npg + j], 0, 0)

    page_block = (1, 1, PAGE_SIZE * KV_HEADS, HEAD_DIM)
    const2 = lambda n, s, pt: (0, 0)
    in_specs = ([pl.BlockSpec((T, Q_W), lambda n, s, pt: (row_block0 + n, 0)),
                 pl.BlockSpec((T, KV_W), lambda n, s, pt: (row_block0 + n, kc)),
                 pl.BlockSpec((T, KV_W), lambda n, s, pt: (row_block0 + n, vc))]
                + [pl.BlockSpec(page_block, page(j)) for j in range(npg)]
                + [pl.BlockSpec(page_block, page(j)) for j in range(npg)]
                + [pl.BlockSpec((rows, PAGE_SIZE), const2),
                   pl.BlockSpec((rows, PAGE_SIZE), const2),
                   pl.BlockSpec(memory_space=pltpu.SMEM),
                   pl.BlockSpec((1, HEAD_DIM), const2)])
    return pl.pallas_call(
        functools.partial(_diff_sample_kernel, T=T, lam_init=lam_init, n_steps=n_steps),
        grid_spec=pltpu.PrefetchScalarGridSpec(
            num_scalar_prefetch=1,
            grid=(DB, n_steps),
            in_specs=in_specs,
            out_specs=pl.BlockSpec((T, Q_W), lambda n, s, pt: (n, 0)),
            scratch_shapes=[pltpu.VMEM((rows, HEAD_DIM), f32),
                            pltpu.VMEM((rows, 1), f32),
                            pltpu.VMEM((rows, 1), f32),
                            pltpu.VMEM((rows, HEAD_DIM), f32)],
        ),
        out_shape=jax.ShapeDtypeStruct((DB * T, Q_W), f32),
        compiler_params=_params(2),
        name="diff_sample",
    )(page_table, p, p, p, *([cache_k] * npg), *([cache_v] * npg), bias_last, bias_new, lam, gain)


def _row_tile(total, candidates):
    for t in candidates:
        if total % t == 0:
            return t
    raise ValueError(f"no row tile for {total} rows")


def kernel(x_prompt, x_sample, cache_win_k, cache_win_v, cache_diff_k, cache_diff_v, cache_mem_k, cache_mem_v,
           page_table, mem_prompt, w_in, w_mem_kv, w_o, rel_bias, sinks, lam_q1, lam_k1, lam_q2, lam_k2, subln_g,
           ln1_g, ln1_b, ln2_g, ln2_b, w_router, router_bias, w_gate_up, w_down):
    B, S, _ = x_prompt.shape
    DB, T, _ = x_sample.shape
    TP, TS = B * S, DB * T
    TT = TP + TS
    assert S % DIFF_TQ == 0 and S % WINDOW == 0 and TP % T == 0 and T % 8 == 0
    n_swa, n_pool = cache_win_k.shape[0], cache_diff_k.shape[1]
    sample_block0 = TP // T

    x = jnp.concatenate([x_prompt.reshape(TP, D_MODEL), x_sample.reshape(TS, D_MODEL)], axis=0)
    xb = x.astype(bf16)
    w_in_b = w_in.astype(bf16)
    w_o_b = w_o.astype(bf16)
    w_mem_b = w_mem_kv.astype(bf16)
    mem_b = mem_prompt.reshape(B * MEM_LEN, D_MODEL).astype(bf16)
    w_router_t = w_router.T
    router_bias_b = jnp.broadcast_to(router_bias.astype(f32)[:, None], (N_EXPERTS, 128))
    win_k = cache_win_k.reshape(n_swa, DB, WINDOW, KV_W)
    win_v = cache_win_v.reshape(n_swa, DB, WINDOW, KV_W)
    pool_k = cache_diff_k.reshape(-1, n_pool, PAGE_SIZE * KV_HEADS, HEAD_DIM)
    pool_v = cache_diff_v.reshape(-1, n_pool, PAGE_SIZE * KV_HEADS, HEAD_DIM)
    cmem_k = cache_mem_k.reshape(DEPTH, DB, MEM_LEN * MEM_HEADS, HEAD_DIM)
    cmem_v = cache_mem_v.reshape(DEPTH, DB, MEM_LEN * MEM_HEADS, HEAD_DIM)
    pt_flat = page_table.reshape(-1).astype(i32)

    far = rel_bias[NUM_BUCKETS - 1].astype(f32)
    qo = jnp.arange(WINDOW)
    rel = (jnp.arange(2 * WINDOW) - WINDOW)[None, :] - qo[:, None]
    bias_swa_p = _group_rows(_bias_tile(rel_bias, rel, (rel <= 0) & (rel > -WINDOW)), 1)
    tt = jnp.arange(T)
    kpos = jnp.concatenate([jnp.arange(WINDOW) - WINDOW, tt, jnp.full((WINDOW - T,), T)])
    rel = kpos[None, :] - tt[:, None]
    bias_swa_s = _group_rows(_bias_tile(rel_bias, rel, (rel <= 0) & (rel > -WINDOW)), 1)
    qo = jnp.arange(DIFF_TQ)
    rel = (jnp.arange(2 * DIFF_TQ) - DIFF_TQ)[None, :] - qo[:, None]
    bias_diff_p = jnp.swapaxes(_group_rows(_bias_tile(rel_bias, rel, rel <= 0, far), 2), 1, 2) * LOG2_E
    rel = (jnp.arange(PAGE_SIZE) - PAGE_SIZE)[None, :] - tt[:, None]
    bias_diff_last = _group_rows(_bias_tile(rel_bias, rel, rel <= 0, far), 2).reshape(-1, PAGE_SIZE)
    kpos = jnp.concatenate([tt, jnp.full((PAGE_SIZE - T,), T)])
    rel = kpos[None, :] - tt[:, None]
    bias_diff_new = _group_rows(_bias_tile(rel_bias, rel, rel <= 0, far), 2).reshape(-1, PAGE_SIZE)

    tm_mm = _row_tile(TT, (1056, 1024, 512, 256))
    tm_ln = _row_tile(TT, (352, 256, 128))
    tm_wo = _row_tile(math.gcd(TP, TS), (256, 128))
    tm_rt = _row_tile(TT, (1408, 1024, 512, 256, 128))
    tq_mem = _row_tile(S, (512, 256, 128))

    win_k_p, win_v_p, win_k_s, win_v_s = [], [], [], []
    diff_k_p, diff_v_p, diff_k_s, diff_v_s = [], [], [], []
    mem_k_p, mem_v_p = [], []
    for l in range(DEPTH):
        i = l // 2
        p = matmul(xb, w_in_b[l], tm_mm, 512)
        k_p = p[:TP, Q_W:Q_W + KV_W].reshape(B, S, KV_HEADS, HEAD_DIM)
        v_p = p[:TP, Q_W + KV_W:Q_W + 2 * KV_W].reshape(B, S, KV_HEADS, HEAD_DIM)
        if l % 2 == 0:
            sink = sinks[i].astype(f32)
            self_p = swa_prompt(p, B, S, bias_swa_p, sink)
            self_s, nk, nv = swa_sample(p, sample_block0, DB, T, win_k, win_v, i, bias_swa_s, sink)
            win_k_p.append(k_p[:, S - WINDOW:])
            win_v_p.append(v_p[:, S - WINDOW:])
            win_k_s.append(nk.reshape(DB, WINDOW, KV_HEADS, HEAD_DIM))
            win_v_s.append(nv.reshape(DB, WINDOW, KV_HEADS, HEAD_DIM))
        else:
            lam_init = 0.8 - 0.6 * math.exp(-0.3 * l)
            lam = (jnp.exp(jnp.sum(lam_q1[i].astype(f32) * lam_k1[i].astype(f32)))
                   - jnp.exp(jnp.sum(lam_q2[i].astype(f32) * lam_k2[i].astype(f32))) + lam_init).reshape(1)
            gain = subln_g[i].astype(f32).reshape(1, HEAD_DIM)
            self_p = diff_prompt(p, B, S, bias_diff_p, lam, gain.reshape(HEAD_DIM, 1), lam_init)
            self_s = diff_sample(p, sample_block0, DB, T, pool_k, pool_v, i, pt_flat,
                                 bias_diff_last, bias_diff_new, lam, gain, lam_init)
            diff_k_p.append(k_p)
            diff_v_p.append(v_p)
            diff_k_s.append(p[TP:, Q_W:Q_W + KV_W].reshape(DB, T, KV_HEADS, HEAD_DIM))
            diff_v_s.append(p[TP:, Q_W + KV_W:Q_W + 2 * KV_W].reshape(DB, T, KV_HEADS, HEAD_DIM))
        mkv = matmul(mem_b, w_mem_b[l], B * MEM_LEN, 512).reshape(B, MEM_LEN, 2 * MEM_W)
        mem_k_p.append(mkv[:, :, :MEM_W].reshape(B, MEM_LEN, MEM_HEADS, HEAD_DIM))
        mem_v_p.append(mkv[:, :, MEM_W:].reshape(B, MEM_LEN, MEM_HEADS, HEAD_DIM))
        cross_p = mem_attend(p, 0, B, tq_mem, S // tq_mem, mkv, mkv,
                             lambda n, t: (n, 0, 0), lambda n, t: (n, 0, 1), (1, MEM_LEN, MEM_W))
        cross_s = mem_attend(p, sample_block0, DB, T, 1, cmem_k, cmem_v,
                             lambda n, t: (l, n, 0, 0), lambda n, t: (l, n, 0, 0),
                             (1, 1, MEM_LEN * MEM_HEADS, HEAD_DIM))
        x, xb = wo_ln(self_p, cross_p, self_s, cross_s, x, w_o_b[l, :Q_W], w_o_b[l, Q_W:],
                      ln1_g[l].reshape(1, D_MODEL), ln1_b[l].reshape(1, D_MODEL), tm_wo)
        e_idx, gate = router(x, w_router_t, router_bias_b, tm_rt)
        dest, row_tok, blk_e, n_used = moe_dispatch(e_idx)
        xs = gather_rows(x, row_tok, n_used, MOE_TM, bf16)
        yb = moe_ffn_sorted(xs, blk_e, n_used, w_gate_up, w_down, l)
        x, xb = combine_ln(x, yb, dest, gate.T,
                           ln2_g[l].reshape(1, D_MODEL), ln2_b[l].reshape(1, D_MODEL), tm_ln)
    return (x[:TP].reshape(B, S, D_MODEL), x[TP:].reshape(DB, T, D_MODEL),
            jnp.stack(win_k_p), jnp.stack(win_v_p), jnp.stack(win_k_s), jnp.stack(win_v_s),
            jnp.stack(diff_k_p), jnp.stack(diff_v_p), jnp.stack(diff_k_s), jnp.stack(diff_v_s),
            jnp.stack(mem_k_p), jnp.stack(mem_v_p))
```

```python
import functools
import math

import jax
import jax.numpy as jnp
from jax import lax
from jax.experimental import pallas as pl
from jax.experimental.pallas import tpu as pltpu

f32 = jnp.float32
bf16 = jnp.bfloat16
i32 = jnp.int32

D_MODEL = 2048
DEPTH = 4
PAST_LEN = 16384
PAGE_SIZE = 128
HEAD_DIM = 128
N_HEADS = 12
KV_HEADS = 4
REP = N_HEADS // KV_HEADS
DIFF_DIM = HEAD_DIM // 2
MEM_HEADS = 4
MEM_LEN = 256
WINDOW = 128
NUM_BUCKETS = 32
MAX_DISTANCE = 128
N_EXPERTS = 16
N_GROUPS = 4
EXPERTS_PER_GROUP = N_EXPERTS // N_GROUPS
TOP_K = 2
D_EXPERT = D_MODEL // 2
Q_W = N_HEADS * HEAD_DIM
KV_W = KV_HEADS * HEAD_DIM
MEM_W = MEM_HEADS * HEAD_DIM
IN_W = Q_W + 2 * KV_W + MEM_W
ALPHA = (2.0 * DEPTH) ** 0.25
LN_EPS = 1e-5
RMS_EPS = 1e-5
NEG_INF = -1e30
LOG2_E = math.log2(math.e)

VMEM_LIMIT_BYTES = 56 * 1024 * 1024

K_COL = Q_W // HEAD_DIM
V_COL = (Q_W + KV_W) // HEAD_DIM
QM_COL = (Q_W + 2 * KV_W) // HEAD_DIM

DIFF_TQ = 256
DIFF_TK = 256
DIFF_FAR_CHUNKS = 2
PAGES_PER_STEP = 16
MOE_TM = 512


def _params(n_axes):
    return pltpu.CompilerParams(dimension_semantics=("arbitrary",) * n_axes,
                                vmem_limit_bytes=VMEM_LIMIT_BYTES)


def _dot_nt(a, b):
    return lax.dot_general(a, b, (((1,), (1,)), ((), ())), preferred_element_type=f32)


def _dot(a, b):
    return jnp.dot(a, b, preferred_element_type=f32)


def _mm_kernel(a_ref, b_ref, o_ref):
    o_ref[...] = _dot(a_ref[...], b_ref[...])


def matmul(a, b, tm, tn):
    M, K = a.shape
    N = b.shape[1]
    return pl.pallas_call(
        _mm_kernel,
        grid=(M // tm, N // tn),
        in_specs=[pl.BlockSpec((tm, K), lambda i, j: (i, 0)),
                  pl.BlockSpec((K, tn), lambda i, j: (0, j))],
        out_specs=pl.BlockSpec((tm, tn), lambda i, j: (i, j)),
        out_shape=jax.ShapeDtypeStruct((M, N), f32),
        compiler_params=_params(2),
        name="matmul",
    )(a, b)


def _layer_norm(y, g, b):
    mu = jnp.mean(y, -1, keepdims=True)
    yc = y - mu
    var = jnp.mean(yc * yc, -1, keepdims=True)
    return yc * lax.rsqrt(var + LN_EPS) * g + b


def _wo_ln_kernel(msp_ref, mcp_ref, mss_ref, mcs_ref, x_ref, ws_ref, wc_ref, g_ref, b_ref, o_ref, ob_ref,
                  *, n_prompt_tiles):
    def body(ms, mc):
        f = _dot(ms, ws_ref[...]) + _dot(mc, wc_ref[...])
        out = _layer_norm(ALPHA * x_ref[...] + f, g_ref[...], b_ref[...])
        o_ref[...] = out
        ob_ref[...] = out.astype(bf16)

    is_prompt = pl.program_id(0) < n_prompt_tiles

    @pl.when(is_prompt)
    def _():
        body(msp_ref[...], mcp_ref[...])

    @pl.when(jnp.logical_not(is_prompt))
    def _():
        body(mss_ref[...].astype(bf16), mcs_ref[...].astype(bf16))


def wo_ln(self_p, cross_p, self_s, cross_s, x, w_self, w_cross, g, b, tm):
    M = x.shape[0]
    n_p = self_p.shape[0] // tm
    assert self_p.shape[0] % tm == 0 and self_s.shape[0] % tm == 0
    row = lambda i: (i, 0)
    prow = lambda i: (jnp.minimum(i, n_p - 1), 0)
    srow = lambda i: (jnp.maximum(i - n_p, 0), 0)
    const = lambda i: (0, 0)
    return pl.pallas_call(
        functools.partial(_wo_ln_kernel, n_prompt_tiles=n_p),
        grid=(M // tm,),
        in_specs=[pl.BlockSpec((tm, Q_W), prow), pl.BlockSpec((tm, MEM_W), prow),
                  pl.BlockSpec((tm, Q_W), srow), pl.BlockSpec((tm, MEM_W), srow),
                  pl.BlockSpec((tm, D_MODEL), row),
                  pl.BlockSpec((Q_W, D_MODEL), const), pl.BlockSpec((MEM_W, D_MODEL), const),
                  pl.BlockSpec((1, D_MODEL), const), pl.BlockSpec((1, D_MODEL), const)],
        out_specs=[pl.BlockSpec((tm, D_MODEL), row), pl.BlockSpec((tm, D_MODEL), row)],
        out_shape=[jax.ShapeDtypeStruct((M, D_MODEL), f32),
                   jax.ShapeDtypeStruct((M, D_MODEL), bf16)],
        compiler_params=_params(1),
        name="wo_ln",
    )(self_p, cross_p, self_s, cross_s, x, w_self, w_cross, g, b)


def _pipelined_row_gather(i, n_used, idx_first_ref, idx_next_ref, src_ref, buf, sem, rows):
    def row_copy(slot, r, src_row):
        return pltpu.make_async_copy(src_ref.at[pl.ds(src_row, 1)], buf.at[slot, pl.ds(r, 1)], sem.at[slot])

    def start_block(slot, idx_ref):
        def body(h, carry):
            for q in range(2):
                r = 2 * h + q
                row_copy(slot, r, idx_ref[0, 0, r]).start(priority=q)
            return carry
        lax.fori_loop(0, rows // 2, body, 0, unroll=4)

    @pl.when(i == 0)
    def _():
        start_block(0, idx_first_ref)

    @pl.when(i + 1 < n_used)
    def _():
        start_block((i + 1) % 2, idx_next_ref)

    @pl.when(i < n_used)
    def _():
        def body(r, carry):
            row_copy(i % 2, r, 0).wait()
            return carry
        lax.fori_loop(0, rows, body, 0, unroll=8)


def _gather_idx_specs(rows, n_blocks):
    first = lambda i, *_: (0, 0, 0)
    nxt = lambda i, *_: (jnp.minimum(i + 1, n_blocks - 1), 0, 0)
    return [pl.BlockSpec((1, 1, rows), first, memory_space=pltpu.SMEM),
            pl.BlockSpec((1, 1, rows), nxt, memory_space=pltpu.SMEM)]


def _gather_rows_kernel(nu_ref, idx_first_ref, idx_next_ref, src_ref, o_ref, buf, sem, *, rows):
    i = pl.program_id(0)
    used = i < nu_ref[0]
    _pipelined_row_gather(i, nu_ref[0], idx_first_ref, idx_next_ref, src_ref, buf, sem, rows)

    @pl.when(used)
    def _():
        o_ref[...] = buf[i % 2].astype(o_ref.dtype)

    @pl.when(jnp.logical_not(used))
    def _():
        o_ref[...] = jnp.zeros(o_ref.shape, o_ref.dtype)


def gather_rows(src, idx, n_used, rows, out_dtype):
    n_blocks = idx.shape[0] // rows
    D = src.shape[1]
    idx3 = idx.reshape(n_blocks, 1, rows)
    return pl.pallas_call(
        functools.partial(_gather_rows_kernel, rows=rows),
        grid_spec=pltpu.PrefetchScalarGridSpec(
            num_scalar_prefetch=1,
            grid=(n_blocks,),
            in_specs=_gather_idx_specs(rows, n_blocks) + [pl.BlockSpec(memory_space=pl.ANY)],
            out_specs=pl.BlockSpec((rows, D), lambda i, nu: (i, 0)),
            scratch_shapes=[pltpu.VMEM((2, rows, D), src.dtype), pltpu.SemaphoreType.DMA((2,))],
        ),
        out_shape=jax.ShapeDtypeStruct((n_blocks * rows, D), out_dtype),
        compiler_params=_params(1),
        name="gather_rows",
    )(n_used, idx3, idx3, src)


def _combine_ln_kernel(idx_first_ref, idx_next_ref, y_ref, x_ref, gate_ref, g_ref, b_ref, o_ref, ob_ref,
                       buf, sem, *, tm):
    i = pl.program_id(0)
    _pipelined_row_gather(i, pl.num_programs(0), idx_first_ref, idx_next_ref, y_ref, buf, sem, TOP_K * tm)
    gate = gate_ref[...]
    y = buf[i % 2]
    f = y[:tm] * gate[:, 0:1] + y[tm:] * gate[:, 1:2]
    out = _layer_norm(ALPHA * x_ref[...] + f, g_ref[...], b_ref[...])
    o_ref[...] = out
    ob_ref[...] = out.astype(bf16)


def combine_ln(x, yb, dest, gate_t, g, b, tm):
    M = x.shape[0]
    n_tiles = M // tm
    idx3 = jnp.swapaxes(dest.reshape(n_tiles, tm, TOP_K), 1, 2).reshape(n_tiles, 1, TOP_K * tm)
    row = lambda i: (i, 0)
    const = lambda i: (0, 0)
    return pl.pallas_call(
        functools.partial(_combine_ln_kernel, tm=tm),
        grid=(n_tiles,),
        in_specs=_gather_idx_specs(TOP_K * tm, n_tiles) + [
            pl.BlockSpec(memory_space=pl.ANY),
            pl.BlockSpec((tm, D_MODEL), row), pl.BlockSpec((tm, TOP_K), row),
            pl.BlockSpec((1, D_MODEL), const), pl.BlockSpec((1, D_MODEL), const)],
        out_specs=[pl.BlockSpec((tm, D_MODEL), row), pl.BlockSpec((tm, D_MODEL), row)],
        out_shape=[jax.ShapeDtypeStruct((M, D_MODEL), f32),
                   jax.ShapeDtypeStruct((M, D_MODEL), bf16)],
        scratch_shapes=[pltpu.VMEM((2, TOP_K * tm, D_MODEL), f32), pltpu.SemaphoreType.DMA((2,))],
        compiler_params=_params(1),
        name="combine_ln",
    )(idx3, idx3, yb, x, gate_t, g, b)


def _router_kernel(x_ref, wh_ref, wl_ref, bias_ref, e_ref, gate_ref):
    x = x_ref[...]
    xh = x.astype(bf16)
    xl = (x - xh.astype(f32)).astype(bf16)
    wh, wl = wh_ref[...], wl_ref[...]
    logits = _dot_nt(wh, xh) + (_dot_nt(wh, xl) + _dot_nt(wl, xh))
    scores = jax.nn.sigmoid(logits)
    sel = scores + bias_ref[...][:, 0:1]
    n = EXPERTS_PER_GROUP
    rows = [sel[e:e + 1, :] for e in range(N_EXPERTS)]
    srows = [scores[e:e + 1, :] for e in range(N_EXPERTS)]
    gscore = []
    for g in range(N_GROUPS):
        v = rows[g * n:(g + 1) * n]
        best = None
        for a in range(n):
            for b in range(a + 1, n):
                pair = v[a] + v[b]
                best = pair if best is None else jnp.maximum(best, pair)
        gscore.append(best)
    g_idx = jnp.zeros_like(gscore[0], dtype=i32)
    best = gscore[0]
    for g in range(1, N_GROUPS):
        take = gscore[g] > best
        g_idx = jnp.where(take, g, g_idx)
        best = jnp.where(take, gscore[g], best)
    ing, sg = [], []
    for a in range(n):
        va, sa = rows[a], srows[a]
        for g in range(1, N_GROUPS):
            va = jnp.where(g_idx == g, rows[g * n + a], va)
            sa = jnp.where(g_idx == g, srows[g * n + a], sa)
        ing.append(va)
        sg.append(sa)
    l_idx = [jnp.zeros_like(g_idx), jnp.zeros_like(g_idx)]
    w = [jnp.zeros_like(best), jnp.zeros_like(best)]
    for a in range(n):
        rank = jnp.zeros_like(g_idx)
        for b in range(n):
            if b == a:
                continue
            ahead = (ing[b] > ing[a]) | ((ing[b] == ing[a]) & (b < a))
            rank = rank + ahead.astype(i32)
        for k in range(TOP_K):
            hit = rank == k
            l_idx[k] = jnp.where(hit, a, l_idx[k])
            w[k] = jnp.where(hit, sg[a], w[k])
    tot = w[0] + w[1]
    e_ref[...] = jnp.concatenate([g_idx * n + l_idx[0], g_idx * n + l_idx[1]], axis=0)
    gate_ref[...] = jnp.concatenate([w[0] / tot, w[1] / tot], axis=0)


def router(x, w_router_t, router_bias, tm):
    M = x.shape[0]
    w_hi = w_router_t.astype(bf16)
    w_lo = (w_router_t - w_hi.astype(f32)).astype(bf16)
    return pl.pallas_call(
        _router_kernel,
        grid=(M // tm,),
        in_specs=[pl.BlockSpec((tm, D_MODEL), lambda i: (i, 0)),
                  pl.BlockSpec((N_EXPERTS, D_MODEL), lambda i: (0, 0)),
                  pl.BlockSpec((N_EXPERTS, D_MODEL), lambda i: (0, 0)),
                  pl.BlockSpec((N_EXPERTS, 128), lambda i: (0, 0))],
        out_specs=[pl.BlockSpec((TOP_K, tm), lambda i: (0, i)),
                   pl.BlockSpec((TOP_K, tm), lambda i: (0, i))],
        out_shape=[jax.ShapeDtypeStruct((TOP_K, M), i32),
                   jax.ShapeDtypeStruct((TOP_K, M), f32)],
        compiler_params=_params(1),
        name="router",
    )(x, w_hi, w_lo, router_bias)


def _moe_up_kernel(be_ref, nu_ref, x_ref, wg_ref, wu_ref, h_ref):
    used = pl.program_id(1) < nu_ref[0]

    @pl.when(used)
    def _():
        x = x_ref[...]
        g = _dot(x, wg_ref[0, 0].astype(bf16))
        u = _dot(x, wu_ref[0, 0].astype(bf16))
        h_ref[...] = (g * jax.nn.sigmoid(g) * u).astype(bf16)

    @pl.when(jnp.logical_not(used))
    def _():
        h_ref[...] = jnp.zeros(h_ref.shape, h_ref.dtype)


def _moe_down_kernel(be_ref, nu_ref, h_ref, wd_ref, y_ref):
    used = pl.program_id(1) < nu_ref[0]

    @pl.when(used)
    def _():
        y_ref[...] = _dot(h_ref[...], wd_ref[0, 0].astype(bf16))

    @pl.when(jnp.logical_not(used))
    def _():
        y_ref[...] = jnp.zeros(y_ref.shape, y_ref.dtype)


def moe_ffn_sorted(xs, blk_e, n_used, w_gu, w_dn, l, tn_up=512, tn_dn=2048):
    R = xs.shape[0]
    n_blocks = R // MOE_TM
    nj = D_EXPERT // tn_up
    blk = lambda j, i, be, nu: jnp.minimum(i, nu[0] - 1)
    h = pl.pallas_call(
        _moe_up_kernel,
        grid_spec=pltpu.PrefetchScalarGridSpec(
            num_scalar_prefetch=2,
            grid=(nj, n_blocks),
            in_specs=[pl.BlockSpec((MOE_TM, D_MODEL), lambda j, i, be, nu: (blk(j, i, be, nu), 0)),
                      pl.BlockSpec((1, 1, D_MODEL, tn_up), lambda j, i, be, nu: (l, be[i], 0, j)),
                      pl.BlockSpec((1, 1, D_MODEL, tn_up), lambda j, i, be, nu: (l, be[i], 0, nj + j))],
            out_specs=pl.BlockSpec((MOE_TM, tn_up), lambda j, i, be, nu: (i, j)),
        ),
        out_shape=jax.ShapeDtypeStruct((R, D_EXPERT), bf16),
        compiler_params=_params(2),
        name="moe_up",
    )(blk_e, n_used, xs, w_gu, w_gu)
    nj2 = D_MODEL // tn_dn
    return pl.pallas_call(
        _moe_down_kernel,
        grid_spec=pltpu.PrefetchScalarGridSpec(
            num_scalar_prefetch=2,
            grid=(nj2, n_blocks),
            in_specs=[pl.BlockSpec((MOE_TM, D_EXPERT), lambda j, i, be, nu: (blk(j, i, be, nu), 0)),
                      pl.BlockSpec((1, 1, D_EXPERT, tn_dn), lambda j, i, be, nu: (l, be[i], 0, j))],
            out_specs=pl.BlockSpec((MOE_TM, tn_dn), lambda j, i, be, nu: (i, j)),
        ),
        out_shape=jax.ShapeDtypeStruct((R, D_MODEL), f32),
        compiler_params=_params(2),
        name="moe_down",
    )(blk_e, n_used, h, w_dn)


def moe_dispatch(e_idx):
    T = e_idx.shape[1]
    A = T * TOP_K
    flat_e = e_idx.T.reshape(A)
    onehot = (flat_e[:, None] == jnp.arange(N_EXPERTS, dtype=i32)[None, :]).astype(i32)
    csum = jnp.cumsum(onehot, axis=0)
    rank = jnp.take_along_axis(csum, flat_e[:, None], axis=1)[:, 0] - 1
    counts = csum[-1]
    padded = (counts + MOE_TM - 1) // MOE_TM * MOE_TM
    pad_ends = jnp.cumsum(padded)
    pad_starts = pad_ends - padded
    dest = pad_starts[flat_e] + rank
    n_blocks = -(-A // MOE_TM) + N_EXPERTS
    tok = jnp.arange(A, dtype=i32) // TOP_K
    row_tok = (jnp.arange(n_blocks * MOE_TM, dtype=i32) % T).at[dest].set(tok)
    n_used = (pad_ends[-1] // MOE_TM).astype(i32)
    blk_start = jnp.minimum(jnp.arange(n_blocks, dtype=i32), n_used - 1) * MOE_TM
    blk_e = jnp.minimum(jnp.searchsorted(pad_ends, blk_start, side='right'), N_EXPERTS - 1).astype(i32)
    return dest.reshape(T, TOP_K), row_tok, blk_e, n_used.reshape(1)


def _t5_bucket(rel):
    n = jnp.maximum(-rel, 0)
    max_exact = NUM_BUCKETS // 2
    nf = jnp.maximum(n, 1).astype(f32)
    large = max_exact + (jnp.log(nf / max_exact) / math.log(MAX_DISTANCE / max_exact)
                         * (NUM_BUCKETS - max_exact)).astype(i32)
    large = jnp.minimum(large, NUM_BUCKETS - 1)
    return jnp.where(n < max_exact, n, large)


def _bias_tile(table, rel, mask, shift=None):
    t = table.astype(f32)
    if shift is not None:
        t = t - shift[None, :]
    bucket = jnp.where(mask, _t5_bucket(rel), -1)[None]
    b = jnp.full((table.shape[1],) + rel.shape, NEG_INF, f32)
    for i in range(NUM_BUCKETS):
        b = jnp.where(bucket == i, t[i][:, None, None], b)
    return b


def _group_rows(b, comps):
    H, Tq, Tk = b.shape
    b = b.reshape(KV_HEADS, REP, 1, Tq, Tk)
    b = jnp.broadcast_to(b, (KV_HEADS, REP, comps, Tq, Tk))
    return b.reshape(KV_HEADS, REP * comps * Tq, Tk)


def _mem_attn_kernel(q_ref, k_ref, v_ref, o_ref, *, interleaved):
    scale = HEAD_DIM ** -0.5
    for h in range(MEM_HEADS):
        sl = slice(h * HEAD_DIM, (h + 1) * HEAD_DIM)
        if interleaved:
            k = k_ref[0, 0, pl.ds(h, MEM_LEN, stride=MEM_HEADS), :]
            v = v_ref[0, 0, pl.ds(h, MEM_LEN, stride=MEM_HEADS), :]
        else:
            k = k_ref[0, :, sl]
            v = v_ref[0, :, sl]
        s = _dot_nt(q_ref[:, sl].astype(bf16), k.astype(bf16)) * scale
        m = jnp.max(s, -1, keepdims=True)
        e = jnp.exp(s - m)
        p = e * (1.0 / jnp.sum(e, -1, keepdims=True))
        o_ref[:, sl] = _dot(p.astype(bf16), v.astype(bf16)).astype(o_ref.dtype)


def mem_attend(p, row_block0, n_seq, tq, tiles_per_seq, mem_k, mem_v, k_map, v_map, kv_block):
    rows = n_seq * tiles_per_seq * tq
    return pl.pallas_call(
        functools.partial(_mem_attn_kernel, interleaved=len(kv_block) == 4),
        grid=(n_seq, tiles_per_seq),
        in_specs=[pl.BlockSpec((tq, MEM_W), lambda n, i: (row_block0 + n * tiles_per_seq + i, QM_COL // MEM_HEADS)),
                  pl.BlockSpec(kv_block, k_map), pl.BlockSpec(kv_block, v_map)],
        out_specs=pl.BlockSpec((tq, MEM_W), lambda n, i: (n * tiles_per_seq + i, 0)),
        out_shape=jax.ShapeDtypeStruct((rows, MEM_W), f32 if tq < 16 else bf16),
        compiler_params=_params(2),
        name="mem_attn",
    )(p, mem_k, mem_v)


def _softmax_sink_pv(s, sink_col, v):
    m = jnp.maximum(jnp.max(s, -1, keepdims=True), sink_col)
    e = jnp.exp(s - m)
    den = jnp.sum(e, -1, keepdims=True) + jnp.exp(sink_col - m)
    p = e * (1.0 / den)
    return _dot(p.astype(bf16), v)


def _sink_col(sink_ref, g, tq):
    return jnp.concatenate([jnp.full((tq, 1), sink_ref[REP * g + r], f32) for r in range(REP)], axis=0)


def _swa_prompt_kernel(q_ref, kc_ref, kp_ref, vc_ref, vp_ref, bias_ref, sink_ref, o_ref):
    i = pl.program_id(1)
    scale = HEAD_DIM ** -0.5
    for g in range(KV_HEADS):
        sl = slice(g * HEAD_DIM, (g + 1) * HEAD_DIM)
        heads = [slice((REP * g + r) * HEAD_DIM, (REP * g + r + 1) * HEAD_DIM) for r in range(REP)]
        kk = jnp.concatenate([kp_ref[:, sl], kc_ref[:, sl]], axis=0).astype(bf16)
        vt = jnp.concatenate([vp_ref[:, sl], vc_ref[:, sl]], axis=0).T.astype(bf16)
        qt = jnp.concatenate([q_ref[:, h].T for h in heads], axis=1).astype(bf16)
        s = _dot(kk, qt) * scale + bias_ref[g]
        key = lax.broadcasted_iota(i32, s.shape, 0)
        s = jnp.where((key >= WINDOW) | (i > 0), s, NEG_INF)
        sink = jnp.concatenate([jnp.full((1, WINDOW), sink_ref[REP * g + r], f32) for r in range(REP)], axis=1)
        m = jnp.maximum(jnp.max(s, axis=0, keepdims=True), sink)
        e = jnp.exp(s - m)
        den = jnp.sum(e, axis=0, keepdims=True) + jnp.exp(sink - m)
        o = _dot(vt, (e * (1.0 / den)).astype(bf16))
        for r, h in enumerate(heads):
            o_ref[:, h] = o[:, r * WINDOW:(r + 1) * WINDOW].T.astype(o_ref.dtype)


def swa_prompt(p, B, S, bias, sink):
    nb = S // WINDOW
    cur = lambda c: (lambda b, i: (b * nb + i, c))
    prev = lambda c: (lambda b, i: (b * nb + jnp.maximum(i - 1, 0), c))
    kc, vc = K_COL // KV_HEADS, V_COL // KV_HEADS
    return pl.pallas_call(
        _swa_prompt_kernel,
        grid=(B, nb),
        in_specs=[pl.BlockSpec((WINDOW, Q_W), cur(0)),
                  pl.BlockSpec((WINDOW, KV_W), cur(kc)), pl.BlockSpec((WINDOW, KV_W), prev(kc)),
                  pl.BlockSpec((WINDOW, KV_W), cur(vc)), pl.BlockSpec((WINDOW, KV_W), prev(vc)),
                  pl.BlockSpec((KV_HEADS, 2 * WINDOW, REP * WINDOW), lambda b, i: (0, 0, 0)),
                  pl.BlockSpec(memory_space=pltpu.SMEM)],
        out_specs=pl.BlockSpec((WINDOW, Q_W), lambda b, i: (b * nb + i, 0)),
        out_shape=jax.ShapeDtypeStruct((B * S, Q_W), bf16),
        compiler_params=_params(2),
        name="swa_prompt",
    )(p, p, p, p, p, bias, sink)


def _swa_sample_kernel(q_ref, kn_ref, vn_ref, wk_ref, wv_ref, bias_ref, sink_ref, o_ref, nk_ref, nv_ref, *, T):
    scale = HEAD_DIM ** -0.5
    wk = wk_ref[...].reshape(WINDOW, KV_W)
    wv = wv_ref[...].reshape(WINDOW, KV_W)
    kn = kn_ref[...]
    vn = vn_ref[...]
    pad = jnp.zeros((WINDOW - T, HEAD_DIM), f32)
    for g in range(KV_HEADS):
        sl = slice(g * HEAD_DIM, (g + 1) * HEAD_DIM)
        kk = jnp.concatenate([wk[:, sl], kn[:, sl], pad], axis=0).astype(bf16)
        vv = jnp.concatenate([wv[:, sl], vn[:, sl], pad], axis=0).astype(bf16)
        q3 = jnp.concatenate([q_ref[:, (REP * g + r) * HEAD_DIM:(REP * g + r + 1) * HEAD_DIM]
                              for r in range(REP)], axis=0).astype(bf16)
        s = _dot_nt(q3, kk) * scale + bias_ref[g]
        o = _softmax_sink_pv(s, _sink_col(sink_ref, g, T), vv)
        for r in range(REP):
            o_ref[:, (REP * g + r) * HEAD_DIM:(REP * g + r + 1) * HEAD_DIM] = o[r * T:(r + 1) * T]
    nk_ref[0, 0:WINDOW - T, :] = wk[T:, :]
    nk_ref[0, WINDOW - T:, :] = kn
    nv_ref[0, 0:WINDOW - T, :] = wv[T:, :]
    nv_ref[0, WINDOW - T:, :] = vn


def swa_sample(p, row_block0, DB, T, win_k, win_v, li, bias, sink):
    kc, vc = K_COL // KV_HEADS, V_COL // KV_HEADS
    win = lambda n: (li, n, 0, 0)
    return pl.pallas_call(
        functools.partial(_swa_sample_kernel, T=T),
        grid=(DB,),
        in_specs=[pl.BlockSpec((T, Q_W), lambda n: (row_block0 + n, 0)),
                  pl.BlockSpec((T, KV_W), lambda n: (row_block0 + n, kc)),
                  pl.BlockSpec((T, KV_W), lambda n: (row_block0 + n, vc)),
                  pl.BlockSpec((1, 1, WINDOW, KV_W), win), pl.BlockSpec((1, 1, WINDOW, KV_W), win),
                  pl.BlockSpec((KV_HEADS, REP * T, 2 * WINDOW), lambda n: (0, 0, 0)),
                  pl.BlockSpec(memory_space=pltpu.SMEM)],
        out_specs=[pl.BlockSpec((T, Q_W), lambda n: (n, 0)),
                   pl.BlockSpec((1, WINDOW, KV_W), lambda n: (n, 0, 0)),
                   pl.BlockSpec((1, WINDOW, KV_W), lambda n: (n, 0, 0))],
        out_shape=[jax.ShapeDtypeStruct((DB * T, Q_W), f32),
                   jax.ShapeDtypeStruct((DB, WINDOW, KV_W), f32),
                   jax.ShapeDtypeStruct((DB, WINDOW, KV_W), f32)],
        compiler_params=_params(1),
        name="swa_sample",
    )(p, p, p, win_k, win_v, bias, sink)


def _split_components(q, scale):
    lane = lax.broadcasted_iota(i32, q.shape, 1)
    qs = q * scale
    return [jnp.where(lane < DIFF_DIM, qs, 0.0), jnp.where(lane >= DIFF_DIM, qs, 0.0)]


def _diff_finish(o0, o1, lam, gain, lam_init, axis):
    o = o0 - lam * o1
    return o * lax.rsqrt(jnp.mean(o * o, axis, keepdims=True) + RMS_EPS) * gain * (1.0 - lam_init)


def _diff_prompt_kernel(q_ref, k_ref, v_ref, bias_ref, lam_ref, gain_ref, o_ref,
                        kb_scr, vt_scr, qpt_scr, m_scr, l_scr, acc_scr, *, lam_init):
    i = pl.program_id(2)
    tq, tk = DIFF_TQ, DIFF_TK
    n_chunks = k_ref.shape[0] // tk

    @pl.when(i == 0)
    def _():
        for c in range(n_chunks):
            kb_scr[c] = k_ref[c * tk:(c + 1) * tk, :].astype(bf16)
            vt_scr[c] = v_ref[c * tk:(c + 1) * tk, :].T.astype(bf16)

    pieces = []
    for r in range(REP):
        for part in _split_components(q_ref[:, r * HEAD_DIM:(r + 1) * HEAD_DIM], DIFF_DIM ** -0.5 * LOG2_E):
            pieces.append(part.T)
    qpt_scr[...] = jnp.concatenate(pieces, axis=1).astype(bf16)
    m_scr[...] = jnp.full(m_scr.shape, NEG_INF, f32)
    l_scr[...] = jnp.zeros(l_scr.shape, f32)
    acc_scr[...] = jnp.zeros(acc_scr.shape, f32)

    def chunk(c, n, bias):
        if n == 1:
            kb, vt = kb_scr[c], vt_scr[c]
        else:
            kb = kb_scr[pl.ds(c, n)].reshape(n * tk, HEAD_DIM)
            vt = jnp.concatenate([vt_scr[c + d] for d in range(n)], axis=1)
        s = _dot(kb, qpt_scr[...])
        if bias is not None:
            s = s + bias
        m_old = m_scr[...]
        m_new = jnp.maximum(m_old, jnp.max(s, axis=0, keepdims=True))
        p = jnp.exp2(s - m_new)
        alpha = jnp.exp2(m_old - m_new)
        l_scr[...] = alpha * l_scr[...] + jnp.sum(p, axis=0, keepdims=True)
        acc_scr[...] = alpha * acc_scr[...] + _dot(vt, p.astype(bf16))
        m_scr[...] = m_new

    step = tq // tk
    for d in range(step):
        chunk(i * step + d, 1, bias_ref[0, tq + d * tk:tq + (d + 1) * tk, :])

    @pl.when(i > 0)
    def _():
        for d in range(step):
            chunk((i - 1) * step + d, 1, bias_ref[0, d * tk:(d + 1) * tk, :])

    n_far = jnp.maximum(i - 1, 0) * step
    n_group = n_far // DIFF_FAR_CHUNKS

    def far_group(j, carry):
        chunk(j * DIFF_FAR_CHUNKS, DIFF_FAR_CHUNKS, None)
        return carry

    def far_single(c, carry):
        chunk(c, 1, None)
        return carry

    lax.fori_loop(0, n_group, far_group, 0)
    lax.fori_loop(n_group * DIFF_FAR_CHUNKS, n_far, far_single, 0)

    acc = acc_scr[...] * (1.0 / l_scr[...])
    lam = lam_ref[0]
    for r in range(REP):
        o = _diff_finish(acc[:, (2 * r) * tq:(2 * r + 1) * tq], acc[:, (2 * r + 1) * tq:(2 * r + 2) * tq],
                         lam, gain_ref[...], lam_init, 0)
        o_ref[:, r * HEAD_DIM:(r + 1) * HEAD_DIM] = o.T.astype(o_ref.dtype)


def diff_prompt(p, B, S, bias_t, lam, gain_col, lam_init):
    nq = S // DIFF_TQ
    cols = 2 * REP * DIFF_TQ
    n_chunks = S // DIFF_TK
    return pl.pallas_call(
        functools.partial(_diff_prompt_kernel, lam_init=lam_init),
        grid=(B, KV_HEADS, nq),
        in_specs=[pl.BlockSpec((DIFF_TQ, REP * HEAD_DIM), lambda b, g, i: (b * nq + i, g)),
                  pl.BlockSpec((S, HEAD_DIM), lambda b, g, i: (b, K_COL + g)),
                  pl.BlockSpec((S, HEAD_DIM), lambda b, g, i: (b, V_COL + g)),
                  pl.BlockSpec((1, 2 * DIFF_TQ, cols), lambda b, g, i: (g, 0, 0)),
                  pl.BlockSpec(memory_space=pltpu.SMEM),
                  pl.BlockSpec((HEAD_DIM, 1), lambda b, g, i: (0, 0))],
        out_specs=pl.BlockSpec((DIFF_TQ, REP * HEAD_DIM), lambda b, g, i: (b * nq + i, g)),
        out_shape=jax.ShapeDtypeStruct((B * S, Q_W), bf16),
        scratch_shapes=[pltpu.VMEM((n_chunks, DIFF_TK, HEAD_DIM), bf16),
                        pltpu.VMEM((n_chunks, HEAD_DIM, DIFF_TK), bf16),
                        pltpu.VMEM((HEAD_DIM, cols), bf16),
                        pltpu.VMEM((1, cols), f32),
                        pltpu.VMEM((1, cols), f32),
                        pltpu.VMEM((HEAD_DIM, cols), f32)],
        compiler_params=_params(3),
        name="diff_prompt",
    )(p, p, p, bias_t, lam, gain_col)


def _diff_sample_kernel(pt_ref, q_ref, kn_ref, vn_ref, *rest, T, lam_init, n_steps):
    npg = PAGES_PER_STEP
    k_refs, v_refs = rest[:npg], rest[npg:2 * npg]
    bias_last_ref, bias_new_ref, lam_ref, gain_ref, o_ref, qp_scr, m_scr, l_scr, acc_scr = rest[2 * npg:]
    s_id = pl.program_id(1)
    rows = 2 * REP * T

    @pl.when(s_id == 0)
    def _():
        pieces = []
        for h in range(N_HEADS):
            pieces += _split_components(q_ref[:, h * HEAD_DIM:(h + 1) * HEAD_DIM], DIFF_DIM ** -0.5)
        qp_scr[...] = jnp.concatenate(pieces, axis=0)
        m_scr[...] = jnp.full(m_scr.shape, NEG_INF, f32)
        l_scr[...] = jnp.zeros(l_scr.shape, f32)
        acc_scr[...] = jnp.zeros(acc_scr.shape, f32)

    def update(s, vs):
        m_old = m_scr[...]
        m_new = jnp.maximum(m_old, jnp.max(s, -1, keepdims=True))
        p = jnp.exp(s - m_new)
        alpha = jnp.exp(m_old - m_new)
        l_scr[...] = alpha * l_scr[...] + jnp.sum(p, -1, keepdims=True)
        pb = p.astype(bf16)
        pv = jnp.concatenate([_dot(pb[g * rows:(g + 1) * rows], vs[g]) for g in range(KV_HEADS)], axis=0)
        acc_scr[...] = alpha * acc_scr[...] + pv
        m_scr[...] = m_new

    def scores(ks):
        qp = qp_scr[...].astype(bf16)
        return jnp.concatenate([_dot_nt(qp[g * rows:(g + 1) * rows], ks[g]) for g in range(KV_HEADS)], axis=0)

    def head_rows(refs, g):
        return jnp.concatenate([r[0, 0, pl.ds(g, PAGE_SIZE, stride=KV_HEADS), :] for r in refs],
                               axis=0).astype(bf16)

    last = s_id == n_steps - 1
    s = scores([head_rows(k_refs, g) for g in range(KV_HEADS)])
    tail = s[:, (npg - 1) * PAGE_SIZE:] + jnp.where(last, bias_last_ref[...], 0.0)
    s = jnp.concatenate([s[:, :(npg - 1) * PAGE_SIZE], tail], axis=1)
    update(s, [head_rows(v_refs, g) for g in range(KV_HEADS)])

    @pl.when(last)
    def _():
        pad = jnp.zeros((PAGE_SIZE - T, HEAD_DIM), f32)
        kn = [jnp.concatenate([kn_ref[:, g * HEAD_DIM:(g + 1) * HEAD_DIM], pad], axis=0).astype(bf16)
              for g in range(KV_HEADS)]
        vn = [jnp.concatenate([vn_ref[:, g * HEAD_DIM:(g + 1) * HEAD_DIM], pad], axis=0).astype(bf16)
              for g in range(KV_HEADS)]
        update(scores(kn) + bias_new_ref[...], vn)
        acc = acc_scr[...] * (1.0 / l_scr[...])
        lam = lam_ref[0]
        for h in range(N_HEADS):
            o = _diff_finish(acc[(2 * h) * T:(2 * h + 1) * T], acc[(2 * h + 1) * T:(2 * h + 2) * T],
                             lam, gain_ref[...], lam_init, -1)
            o_ref[:, h * HEAD_DIM:(h + 1) * HEAD_DIM] = o


def diff_sample(p, row_block0, DB, T, cache_k, cache_v, li, page_table, bias_last, bias_new, lam, gain, lam_init):
    n_pages = PAST_LEN // PAGE_SIZE
    npg = PAGES_PER_STEP
    n_steps = n_pages // npg
    rows = 2 * N_HEADS * T
    kc, vc = K_COL // KV_HEADS, V_COL // KV_HEADS

    def page(j):
        return lambda n, s, pt: (li, pt[n * n_pages + s * npg + j], 0, 0)

    page_block = (1, 1, PAGE_SIZE * KV_HEADS, HEAD_DIM)
    const2 = lambda n, s, pt: (0, 0)
    in_specs = ([pl.BlockSpec((T, Q_W), lambda n, s, pt: (row_block0 + n, 0)),
                 pl.BlockSpec((T, KV_W), lambda n, s, pt: (row_block0 + n, kc)),
                 pl.BlockSpec((T, KV_W), lambda n, s, pt: (row_block0 + n, vc))]
                + [pl.BlockSpec(page_block, page(j)) for j in range(npg)]
                + [pl.BlockSpec(page_block, page(j)) for j in range(npg)]
                + [pl.BlockSpec((rows, PAGE_SIZE), const2),
                   pl.BlockSpec((rows, PAGE_SIZE), const2),
                   pl.BlockSpec(memory_space=pltpu.SMEM),
                   pl.BlockSpec((1, HEAD_DIM), const2)])
    return pl.pallas_call(
        functools.partial(_diff_sample_kernel, T=T, lam_init=lam_init, n_steps=n_steps),
        grid_spec=pltpu.PrefetchScalarGridSpec(
            num_scalar_prefetch=1,
            grid=(DB, n_steps),
            in_specs=in_specs,
            out_specs=pl.BlockSpec((T, Q_W), lambda n, s, pt: (n, 0)),
            scratch_shapes=[pltpu.VMEM((rows, HEAD_DIM), f32),
                            pltpu.VMEM((rows, 1), f32),
                            pltpu.VMEM((rows, 1), f32),
                            pltpu.VMEM((rows, HEAD_DIM), f32)],
        ),
        out_shape=jax.ShapeDtypeStruct((DB * T, Q_W), f32),
        compiler_params=_params(2),
        name="diff_sample",
    )(page_table, p, p, p, *([cache_k] * npg), *([cache_v] * npg), bias_last, bias_new, lam, gain)


def _row_tile(total, candidates):
    for t in candidates:
        if total % t == 0:
            return t
    raise ValueError(f"no row tile for {total} rows")


def kernel(x_prompt, x_sample, cache_win_k, cache_win_v, cache_diff_k, cache_diff_v, cache_mem_k, cache_mem_v,
           page_table, mem_prompt, w_in, w_mem_kv, w_o, rel_bias, sinks, lam_q1, lam_k1, lam_q2, lam_k2, subln_g,
           ln1_g, ln1_b, ln2_g, ln2_b, w_router, router_bias, w_gate_up, w_down):
    B, S, _ = x_prompt.shape
    DB, T, _ = x_sample.shape
    TP, TS = B * S, DB * T
    TT = TP + TS
    assert S % DIFF_TQ == 0 and S % WINDOW == 0 and TP % T == 0 and T % 8 == 0
    n_swa, n_pool = cache_win_k.shape[0], cache_diff_k.shape[1]
    sample_block0 = TP // T

    x = jnp.concatenate([x_prompt.reshape(TP, D_MODEL), x_sample.reshape(TS, D_MODEL)], axis=0)
    xb = x.astype(bf16)
    w_in_b = w_in.astype(bf16)
    w_o_b = w_o.astype(bf16)
    w_mem_b = w_mem_kv.astype(bf16)
    mem_b = mem_prompt.reshape(B * MEM_LEN, D_MODEL).astype(bf16)
    w_router_t = w_router.T
    router_bias_b = jnp.broadcast_to(router_bias.astype(f32)[:, None], (N_EXPERTS, 128))
    win_k = cache_win_k.reshape(n_swa, DB, WINDOW, KV_W)
    win_v = cache_win_v.reshape(n_swa, DB, WINDOW, KV_W)
    pool_k = cache_diff_k.reshape(-1, n_pool, PAGE_SIZE * KV_HEADS, HEAD_DIM)
    pool_v = cache_diff_v.reshape(-1, n_pool, PAGE_SIZE * KV_HEADS, HEAD_DIM)
    cmem_k = cache_mem_k.reshape(DEPTH, DB, MEM_LEN * MEM_HEADS, HEAD_DIM)
    cmem_v = cache_mem_v.reshape(DEPTH, DB, MEM_LEN * MEM_HEADS, HEAD_DIM)
    pt_flat = page_table.reshape(-1).astype(i32)

    far = rel_bias[NUM_BUCKETS - 1].astype(f32)
    qo = jnp.arange(WINDOW)
    rel = (jnp.arange(2 * WINDOW) - WINDOW)[None, :] - qo[:, None]
    bias_swa_p = jnp.swapaxes(_group_rows(_bias_tile(rel_bias, rel, (rel <= 0) & (rel > -WINDOW)), 1), 1, 2)
    tt = jnp.arange(T)
    kpos = jnp.concatenate([jnp.arange(WINDOW) - WINDOW, tt, jnp.full((WINDOW - T,), T)])
    rel = kpos[None, :] - tt[:, None]
    bias_swa_s = _group_rows(_bias_tile(rel_bias, rel, (rel <= 0) & (rel > -WINDOW)), 1)
    qo = jnp.arange(DIFF_TQ)
    rel = (jnp.arange(2 * DIFF_TQ) - DIFF_TQ)[None, :] - qo[:, None]
    bias_diff_p = jnp.swapaxes(_group_rows(_bias_tile(rel_bias, rel, rel <= 0, far), 2), 1, 2) * LOG2_E
    rel = (jnp.arange(PAGE_SIZE) - PAGE_SIZE)[None, :] - tt[:, None]
    bias_diff_last = _group_rows(_bias_tile(rel_bias, rel, rel <= 0, far), 2).reshape(-1, PAGE_SIZE)
    kpos = jnp.concatenate([tt, jnp.full((PAGE_SIZE - T,), T)])
    rel = kpos[None, :] - tt[:, None]
    bias_diff_new = _group_rows(_bias_tile(rel_bias, rel, rel <= 0, far), 2).reshape(-1, PAGE_SIZE)

    tm_mm = _row_tile(TT, (1056, 1024, 512, 256))
    tm_ln = _row_tile(TT, (352, 256, 128))
    tm_wo = _row_tile(math.gcd(TP, TS), (256, 128))
    tm_rt = _row_tile(TT, (1408, 1024, 512, 256, 128))
    tq_mem = _row_tile(S, (512, 256, 128))

    win_k_p, win_v_p, win_k_s, win_v_s = [], [], [], []
    diff_k_p, diff_v_p, diff_k_s, diff_v_s = [], [], [], []
    mem_k_p, mem_v_p = [], []
    for l in range(DEPTH):
        i = l // 2
        p = matmul(xb, w_in_b[l], tm_mm, 512)
        k_p = p[:TP, Q_W:Q_W + KV_W].reshape(B, S, KV_HEADS, HEAD_DIM)
        v_p = p[:TP, Q_W + KV_W:Q_W + 2 * KV_W].reshape(B, S, KV_HEADS, HEAD_DIM)
        if l % 2 == 0:
            sink = sinks[i].astype(f32)
            self_p = swa_prompt(p, B, S, bias_swa_p, sink)
            self_s, nk, nv = swa_sample(p, sample_block0, DB, T, win_k, win_v, i, bias_swa_s, sink)
            win_k_p.append(k_p[:, S - WINDOW:])
            win_v_p.append(v_p[:, S - WINDOW:])
            win_k_s.append(nk.reshape(DB, WINDOW, KV_HEADS, HEAD_DIM))
            win_v_s.append(nv.reshape(DB, WINDOW, KV_HEADS, HEAD_DIM))
        else:
            lam_init = 0.8 - 0.6 * math.exp(-0.3 * l)
            lam = (jnp.exp(jnp.sum(lam_q1[i].astype(f32) * lam_k1[i].astype(f32)))
                   - jnp.exp(jnp.sum(lam_q2[i].astype(f32) * lam_k2[i].astype(f32))) + lam_init).reshape(1)
            gain = subln_g[i].astype(f32).reshape(1, HEAD_DIM)
            self_p = diff_prompt(p, B, S, bias_diff_p, lam, gain.reshape(HEAD_DIM, 1), lam_init)
            self_s = diff_sample(p, sample_block0, DB, T, pool_k, pool_v, i, pt_flat,
                                 bias_diff_last, bias_diff_new, lam, gain, lam_init)
            diff_k_p.append(k_p)
            diff_v_p.append(v_p)
            diff_k_s.append(p[TP:, Q_W:Q_W + KV_W].reshape(DB, T, KV_HEADS, HEAD_DIM))
            diff_v_s.append(p[TP:, Q_W + KV_W:Q_W + 2 * KV_W].reshape(DB, T, KV_HEADS, HEAD_DIM))
        mkv = matmul(mem_b, w_mem_b[l], B * MEM_LEN, 512).reshape(B, MEM_LEN, 2 * MEM_W)
        mem_k_p.append(mkv[:, :, :MEM_W].reshape(B, MEM_LEN, MEM_HEADS, HEAD_DIM))
        mem_v_p.append(mkv[:, :, MEM_W:].reshape(B, MEM_LEN, MEM_HEADS, HEAD_DIM))
        cross_p = mem_attend(p, 0, B, tq_mem, S // tq_mem, mkv, mkv,
                             lambda n, t: (n, 0, 0), lambda n, t: (n, 0, 1), (1, MEM_LEN, MEM_W))
        cross_s = mem_attend(p, sample_block0, DB, T, 1, cmem_k, cmem_v,
                             lambda n, t: (l, n, 0, 0), lambda n, t: (l, n, 0, 0),
                             (1, 1, MEM_LEN * MEM_HEADS, HEAD_DIM))
        x, xb = wo_ln(self_p, cross_p, self_s, cross_s, x, w_o_b[l, :Q_W], w_o_b[l, Q_W:],
                      ln1_g[l].reshape(1, D_MODEL), ln1_b[l].reshape(1, D_MODEL), tm_wo)
        e_idx, gate = router(x, w_router_t, router_bias_b, tm_rt)
        dest, row_tok, blk_e, n_used = moe_dispatch(e_idx)
        xs = gather_rows(x, row_tok, n_used, MOE_TM, bf16)
        yb = moe_ffn_sorted(xs, blk_e, n_used, w_gate_up, w_down, l)
        x, xb = combine_ln(x, yb, dest, gate.T,
                           ln2_g[l].reshape(1, D_MODEL), ln2_b[l].reshape(1, D_MODEL), tm_ln)
    return (x[:TP].reshape(B, S, D_MODEL), x[TP:].reshape(DB, T, D_MODEL),
            jnp.stack(win_k_p), jnp.stack(win_v_p), jnp.stack(win_k_s), jnp.stack(win_v_s),
            jnp.stack(diff_k_p), jnp.stack(diff_v_p), jnp.stack(diff_k_s), jnp.stack(diff_v_s),
            jnp.stack(mem_k_p), jnp.stack(mem_v_p))
```

```python
import functools
import math

import jax
import jax.numpy as jnp
from jax import lax
from jax.experimental import pallas as pl
from jax.experimental.pallas import tpu as pltpu

f32 = jnp.float32
bf16 = jnp.bfloat16
i32 = jnp.int32

D_MODEL = 2048
DEPTH = 4
PAST_LEN = 16384
PAGE_SIZE = 128
HEAD_DIM = 128
N_HEADS = 12
KV_HEADS = 4
REP = N_HEADS // KV_HEADS
DIFF_DIM = HEAD_DIM // 2
MEM_HEADS = 4
MEM_LEN = 256
WINDOW = 128
NUM_BUCKETS = 32
MAX_DISTANCE = 128
N_EXPERTS = 16
N_GROUPS = 4
EXPERTS_PER_GROUP = N_EXPERTS // N_GROUPS
TOP_K = 2
D_EXPERT = D_MODEL // 2
Q_W = N_HEADS * HEAD_DIM
KV_W = KV_HEADS * HEAD_DIM
MEM_W = MEM_HEADS * HEAD_DIM
IN_W = Q_W + 2 * KV_W + MEM_W
ALPHA = (2.0 * DEPTH) ** 0.25
LN_EPS = 1e-5
RMS_EPS = 1e-5
NEG_INF = -1e30
LOG2_E = math.log2(math.e)

VMEM_LIMIT_BYTES = 56 * 1024 * 1024

K_COL = Q_W // HEAD_DIM
V_COL = (Q_W + KV_W) // HEAD_DIM
QM_COL = (Q_W + 2 * KV_W) // HEAD_DIM

DIFF_TQ = 256
DIFF_TK = 256
DIFF_FAR_CHUNKS = 2
PAGES_PER_STEP = 16
MOE_TM = 512


def _params(n_axes):
    return pltpu.CompilerParams(dimension_semantics=("arbitrary",) * n_axes,
                                vmem_limit_bytes=VMEM_LIMIT_BYTES)


def _dot_nt(a, b):
    return lax.dot_general(a, b, (((1,), (1,)), ((), ())), preferred_element_type=f32)


def _dot(a, b):
    return jnp.dot(a, b, preferred_element_type=f32)


def _mm_kernel(a_ref, b_ref, o_ref):
    o_ref[...] = _dot(a_ref[...], b_ref[...])


def matmul(a, b, tm, tn):
    M, K = a.shape
    N = b.shape[1]
    return pl.pallas_call(
        _mm_kernel,
        grid=(M // tm, N // tn),
        in_specs=[pl.BlockSpec((tm, K), lambda i, j: (i, 0)),
                  pl.BlockSpec((K, tn), lambda i, j: (0, j))],
        out_specs=pl.BlockSpec((tm, tn), lambda i, j: (i, j)),
        out_shape=jax.ShapeDtypeStruct((M, N), f32),
        compiler_params=_params(2),
        name="matmul",
    )(a, b)


def _in_proj_kernel(a_ref, b_ref, o_ref, k_ref, v_ref):
    acc = _dot(a_ref[...], b_ref[...])
    o_ref[...] = acc
    tm = acc.shape[0]
    j = pl.program_id(1)
    for tile, ref in ((Q_W // KV_W, k_ref), ((Q_W + KV_W) // KV_W, v_ref)):
        @pl.when(j == tile)
        def _():
            for h in range(KV_HEADS):
                ref[pl.ds(h, tm, stride=KV_HEADS), :] = acc[:, h * HEAD_DIM:(h + 1) * HEAD_DIM]


def in_proj(a, b, tm):
    M, K = a.shape
    kv_shape = jax.ShapeDtypeStruct((M * KV_HEADS, HEAD_DIM), f32)
    kv_spec = pl.BlockSpec((tm * KV_HEADS, HEAD_DIM), lambda i, j: (i, 0))
    return pl.pallas_call(
        _in_proj_kernel,
        grid=(M // tm, IN_W // KV_W),
        in_specs=[pl.BlockSpec((tm, K), lambda i, j: (i, 0)),
                  pl.BlockSpec((K, KV_W), lambda i, j: (0, j))],
        out_specs=[pl.BlockSpec((tm, KV_W), lambda i, j: (i, j)), kv_spec, kv_spec],
        out_shape=[jax.ShapeDtypeStruct((M, IN_W), f32), kv_shape, kv_shape],
        compiler_params=_params(2),
        name="in_proj",
    )(a, b)


def _layer_norm(y, g, b):
    mu = jnp.mean(y, -1, keepdims=True)
    yc = y - mu
    var = jnp.mean(yc * yc, -1, keepdims=True)
    return yc * lax.rsqrt(var + LN_EPS) * g + b


def _pack_bf16_pairs(x):
    u = lax.bitcast_convert_type(x, jnp.uint32)
    r = (u + (jnp.uint32(0x7FFF) + ((u >> 16) & jnp.uint32(1)))) >> 16
    half = x.shape[1] // 2
    return r[:, :half] | (r[:, half:] << 16)


def _unpack_bf16_pairs(p):
    lo = lax.bitcast_convert_type(p << 16, f32)
    hi = lax.bitcast_convert_type(p & jnp.uint32(0xFFFF0000), f32)
    return jnp.concatenate([lo, hi], axis=1).astype(bf16)


def _wo_ln_kernel(msp_ref, mcp_ref, mss_ref, mcs_ref, x_ref, ws_ref, wc_ref, g_ref, b_ref, o_ref, op_ref,
                  *, n_prompt_tiles):
    def body(ms, mc):
        f = _dot(ms, ws_ref[...]) + _dot(mc, wc_ref[...])
        out = _layer_norm(ALPHA * x_ref[...] + f, g_ref[...], b_ref[...])
        o_ref[...] = out
        op_ref[...] = _pack_bf16_pairs(out)

    is_prompt = pl.program_id(0) < n_prompt_tiles

    @pl.when(is_prompt)
    def _():
        body(msp_ref[...], mcp_ref[...])

    @pl.when(jnp.logical_not(is_prompt))
    def _():
        body(mss_ref[...].astype(bf16), mcs_ref[...].astype(bf16))


def wo_ln(self_p, cross_p, self_s, cross_s, x, w_self, w_cross, g, b, tm):
    M = x.shape[0]
    n_p = self_p.shape[0] // tm
    assert self_p.shape[0] % tm == 0 and self_s.shape[0] % tm == 0
    row = lambda i: (i, 0)
    prow = lambda i: (jnp.minimum(i, n_p - 1), 0)
    srow = lambda i: (jnp.maximum(i - n_p, 0), 0)
    const = lambda i: (0, 0)
    return pl.pallas_call(
        functools.partial(_wo_ln_kernel, n_prompt_tiles=n_p),
        grid=(M // tm,),
        in_specs=[pl.BlockSpec((tm, Q_W), prow), pl.BlockSpec((tm, MEM_W), prow),
                  pl.BlockSpec((tm, Q_W), srow), pl.BlockSpec((tm, MEM_W), srow),
                  pl.BlockSpec((tm, D_MODEL), row),
                  pl.BlockSpec((Q_W, D_MODEL), const), pl.BlockSpec((MEM_W, D_MODEL), const),
                  pl.BlockSpec((1, D_MODEL), const), pl.BlockSpec((1, D_MODEL), const)],
        out_specs=[pl.BlockSpec((tm, D_MODEL), row), pl.BlockSpec((tm, D_MODEL // 2), row)],
        out_shape=[jax.ShapeDtypeStruct((M, D_MODEL), f32),
                   jax.ShapeDtypeStruct((M, D_MODEL // 2), jnp.uint32)],
        compiler_params=_params(1),
        name="wo_ln",
    )(self_p, cross_p, self_s, cross_s, x, w_self, w_cross, g, b)


def _pipelined_row_gather(i, n_used, idx_first_ref, idx_next_ref, src_ref, buf, sem, rows):
    def row_copy(slot, r, src_row):
        return pltpu.make_async_copy(src_ref.at[pl.ds(src_row, 1)], buf.at[slot, pl.ds(r, 1)], sem.at[slot])

    def start_block(slot, idx_ref):
        def body(h, carry):
            for q in range(2):
                r = 2 * h + q
                row_copy(slot, r, idx_ref[0, 0, r]).start(priority=q)
            return carry
        lax.fori_loop(0, rows // 2, body, 0, unroll=4)

    @pl.when(i == 0)
    def _():
        start_block(0, idx_first_ref)

    @pl.when(i + 1 < n_used)
    def _():
        start_block((i + 1) % 2, idx_next_ref)

    @pl.when(i < n_used)
    def _():
        def body(r, carry):
            row_copy(i % 2, r, 0).wait()
            return carry
        lax.fori_loop(0, rows, body, 0, unroll=8)


def _gather_idx_specs(rows, n_blocks):
    first = lambda i, *_: (0, 0, 0)
    nxt = lambda i, *_: (jnp.minimum(i + 1, n_blocks - 1), 0, 0)
    return [pl.BlockSpec((1, 1, rows), first, memory_space=pltpu.SMEM),
            pl.BlockSpec((1, 1, rows), nxt, memory_space=pltpu.SMEM)]


def _gather_rows_kernel(nu_ref, idx_first_ref, idx_next_ref, src_ref, o_ref, buf, sem, *, rows):
    i = pl.program_id(0)
    used = i < nu_ref[0]
    _pipelined_row_gather(i, nu_ref[0], idx_first_ref, idx_next_ref, src_ref, buf, sem, rows)

    @pl.when(used)
    def _():
        o_ref[...] = buf[i % 2].astype(o_ref.dtype)

    @pl.when(jnp.logical_not(used))
    def _():
        o_ref[...] = jnp.zeros(o_ref.shape, o_ref.dtype)


def gather_rows(src, idx, n_used, rows, out_dtype):
    n_blocks = idx.shape[0] // rows
    D = src.shape[1]
    idx3 = idx.reshape(n_blocks, 1, rows)
    return pl.pallas_call(
        functools.partial(_gather_rows_kernel, rows=rows),
        grid_spec=pltpu.PrefetchScalarGridSpec(
            num_scalar_prefetch=1,
            grid=(n_blocks,),
            in_specs=_gather_idx_specs(rows, n_blocks) + [pl.BlockSpec(memory_space=pl.ANY)],
            out_specs=pl.BlockSpec((rows, D), lambda i, nu: (i, 0)),
            scratch_shapes=[pltpu.VMEM((2, rows, D), src.dtype), pltpu.SemaphoreType.DMA((2,))],
        ),
        out_shape=jax.ShapeDtypeStruct((n_blocks * rows, D), out_dtype),
        compiler_params=_params(1),
        name="gather_rows",
    )(n_used, idx3, idx3, src)


def _combine_ln_kernel(idx_first_ref, idx_next_ref, y_ref, x_ref, gate_ref, g_ref, b_ref, o_ref, ob_ref,
                       buf, sem, *, tm):
    i = pl.program_id(0)
    _pipelined_row_gather(i, pl.num_programs(0), idx_first_ref, idx_next_ref, y_ref, buf, sem, TOP_K * tm)
    gate = gate_ref[...]
    y = buf[i % 2]
    f = y[:tm] * gate[:, 0:1] + y[tm:] * gate[:, 1:2]
    out = _layer_norm(ALPHA * x_ref[...] + f, g_ref[...], b_ref[...])
    o_ref[...] = out
    ob_ref[...] = out.astype(bf16)


def combine_ln(x, yb, dest, gate_t, g, b, tm):
    M = x.shape[0]
    n_tiles = M // tm
    idx3 = jnp.swapaxes(dest.reshape(n_tiles, tm, TOP_K), 1, 2).reshape(n_tiles, 1, TOP_K * tm)
    row = lambda i: (i, 0)
    const = lambda i: (0, 0)
    return pl.pallas_call(
        functools.partial(_combine_ln_kernel, tm=tm),
        grid=(n_tiles,),
        in_specs=_gather_idx_specs(TOP_K * tm, n_tiles) + [
            pl.BlockSpec(memory_space=pl.ANY),
            pl.BlockSpec((tm, D_MODEL), row), pl.BlockSpec((tm, TOP_K), row),
            pl.BlockSpec((1, D_MODEL), const), pl.BlockSpec((1, D_MODEL), const)],
        out_specs=[pl.BlockSpec((tm, D_MODEL), row), pl.BlockSpec((tm, D_MODEL), row)],
        out_shape=[jax.ShapeDtypeStruct((M, D_MODEL), f32),
                   jax.ShapeDtypeStruct((M, D_MODEL), bf16)],
        scratch_shapes=[pltpu.VMEM((2, TOP_K * tm, D_MODEL), f32), pltpu.SemaphoreType.DMA((2,))],
        compiler_params=_params(1),
        name="combine_ln",
    )(idx3, idx3, yb, x, gate_t, g, b)


def _router_kernel(x_ref, wh_ref, wl_ref, bias_ref, e_ref, gate_ref):
    x = x_ref[...]
    xh = x.astype(bf16)
    xl = (x - xh.astype(f32)).astype(bf16)
    wh, wl = wh_ref[...], wl_ref[...]
    logits = _dot_nt(wh, xh) + (_dot_nt(wh, xl) + _dot_nt(wl, xh))
    scores = jax.nn.sigmoid(logits)
    sel = scores + bias_ref[...][:, 0:1]
    n = EXPERTS_PER_GROUP
    rows = [sel[e:e + 1, :] for e in range(N_EXPERTS)]
    srows = [scores[e:e + 1, :] for e in range(N_EXPERTS)]
    gscore = []
    for g in range(N_GROUPS):
        v = rows[g * n:(g + 1) * n]
        best = None
        for a in range(n):
            for b in range(a + 1, n):
                pair = v[a] + v[b]
                best = pair if best is None else jnp.maximum(best, pair)
        gscore.append(best)
    g_idx = jnp.zeros_like(gscore[0], dtype=i32)
    best = gscore[0]
    for g in range(1, N_GROUPS):
        take = gscore[g] > best
        g_idx = jnp.where(take, g, g_idx)
        best = jnp.where(take, gscore[g], best)
    ing, sg = [], []
    for a in range(n):
        va, sa = rows[a], srows[a]
        for g in range(1, N_GROUPS):
            va = jnp.where(g_idx == g, rows[g * n + a], va)
            sa = jnp.where(g_idx == g, srows[g * n + a], sa)
        ing.append(va)
        sg.append(sa)
    l_idx = [jnp.zeros_like(g_idx), jnp.zeros_like(g_idx)]
    w = [jnp.zeros_like(best), jnp.zeros_like(best)]
    for a in range(n):
        rank = jnp.zeros_like(g_idx)
        for b in range(n):
            if b == a:
                continue
            ahead = (ing[b] > ing[a]) | ((ing[b] == ing[a]) & (b < a))
            rank = rank + ahead.astype(i32)
        for k in range(TOP_K):
            hit = rank == k
            l_idx[k] = jnp.where(hit, a, l_idx[k])
            w[k] = jnp.where(hit, sg[a], w[k])
    tot = w[0] + w[1]
    e_ref[...] = jnp.concatenate([g_idx * n + l_idx[0], g_idx * n + l_idx[1]], axis=0)
    gate_ref[...] = jnp.concatenate([w[0] / tot, w[1] / tot], axis=0)


def router(x, w_router_t, router_bias, tm):
    M = x.shape[0]
    w_hi = w_router_t.astype(bf16)
    w_lo = (w_router_t - w_hi.astype(f32)).astype(bf16)
    return pl.pallas_call(
        _router_kernel,
        grid=(M // tm,),
        in_specs=[pl.BlockSpec((tm, D_MODEL), lambda i: (i, 0)),
                  pl.BlockSpec((N_EXPERTS, D_MODEL), lambda i: (0, 0)),
                  pl.BlockSpec((N_EXPERTS, D_MODEL), lambda i: (0, 0)),
                  pl.BlockSpec((N_EXPERTS, 128), lambda i: (0, 0))],
        out_specs=[pl.BlockSpec((TOP_K, tm), lambda i: (0, i)),
                   pl.BlockSpec((TOP_K, tm), lambda i: (0, i))],
        out_shape=[jax.ShapeDtypeStruct((TOP_K, M), i32),
                   jax.ShapeDtypeStruct((TOP_K, M), f32)],
        compiler_params=_params(1),
        name="router",
    )(x, w_hi, w_lo, router_bias)


def _moe_up_kernel(be_ref, nu_ref, x_ref, wg_ref, wu_ref, h_ref):
    used = pl.program_id(1) < nu_ref[0]

    @pl.when(used)
    def _():
        x = _unpack_bf16_pairs(x_ref[...])
        g = _dot(x, wg_ref[0, 0].astype(bf16))
        u = _dot(x, wu_ref[0, 0].astype(bf16))
        h_ref[...] = (g * jax.nn.sigmoid(g) * u).astype(bf16)

    @pl.when(jnp.logical_not(used))
    def _():
        h_ref[...] = jnp.zeros(h_ref.shape, h_ref.dtype)


def _moe_down_kernel(be_ref, nu_ref, h_ref, wd_ref, y_ref):
    used = pl.program_id(1) < nu_ref[0]

    @pl.when(used)
    def _():
        y_ref[...] = _dot(h_ref[...], wd_ref[0, 0].astype(bf16))

    @pl.when(jnp.logical_not(used))
    def _():
        y_ref[...] = jnp.zeros(y_ref.shape, y_ref.dtype)


def moe_ffn_sorted(xs, blk_e, n_used, w_gu, w_dn, l, tn_up=512, tn_dn=2048):
    R = xs.shape[0]
    n_blocks = R // MOE_TM
    nj = D_EXPERT // tn_up
    blk = lambda j, i, be, nu: jnp.minimum(i, nu[0] - 1)
    h = pl.pallas_call(
        _moe_up_kernel,
        grid_spec=pltpu.PrefetchScalarGridSpec(
            num_scalar_prefetch=2,
            grid=(nj, n_blocks),
            in_specs=[pl.BlockSpec((MOE_TM, D_MODEL // 2), lambda j, i, be, nu: (blk(j, i, be, nu), 0)),
                      pl.BlockSpec((1, 1, D_MODEL, tn_up), lambda j, i, be, nu: (l, be[i], 0, j)),
                      pl.BlockSpec((1, 1, D_MODEL, tn_up), lambda j, i, be, nu: (l, be[i], 0, nj + j))],
            out_specs=pl.BlockSpec((MOE_TM, tn_up), lambda j, i, be, nu: (i, j)),
        ),
        out_shape=jax.ShapeDtypeStruct((R, D_EXPERT), bf16),
        compiler_params=_params(2),
        name="moe_up",
    )(blk_e, n_used, xs, w_gu, w_gu)
    nj2 = D_MODEL // tn_dn
    return pl.pallas_call(
        _moe_down_kernel,
        grid_spec=pltpu.PrefetchScalarGridSpec(
            num_scalar_prefetch=2,
            grid=(nj2, n_blocks),
            in_specs=[pl.BlockSpec((MOE_TM, D_EXPERT), lambda j, i, be, nu: (blk(j, i, be, nu), 0)),
                      pl.BlockSpec((1, 1, D_EXPERT, tn_dn), lambda j, i, be, nu: (l, be[i], 0, j))],
            out_specs=pl.BlockSpec((MOE_TM, tn_dn), lambda j, i, be, nu: (i, j)),
        ),
        out_shape=jax.ShapeDtypeStruct((R, D_MODEL), f32),
        compiler_params=_params(2),
        name="moe_down",
    )(blk_e, n_used, h, w_dn)


def moe_dispatch(e_idx):
    T = e_idx.shape[1]
    A = T * TOP_K
    flat_e = e_idx.T.reshape(A)
    onehot = (flat_e[:, None] == jnp.arange(N_EXPERTS, dtype=i32)[None, :]).astype(i32)
    csum = jnp.cumsum(onehot, axis=0)
    rank = jnp.take_along_axis(csum, flat_e[:, None], axis=1)[:, 0] - 1
    counts = csum[-1]
    padded = (counts + MOE_TM - 1) // MOE_TM * MOE_TM
    pad_ends = jnp.cumsum(padded)
    pad_starts = pad_ends - padded
    dest = pad_starts[flat_e] + rank
    n_blocks = -(-A // MOE_TM) + N_EXPERTS
    tok = jnp.arange(A, dtype=i32) // TOP_K
    row_tok = (jnp.arange(n_blocks * MOE_TM, dtype=i32) % T).at[dest].set(tok)
    n_used = (pad_ends[-1] // MOE_TM).astype(i32)
    blk_start = jnp.minimum(jnp.arange(n_blocks, dtype=i32), n_used - 1) * MOE_TM
    blk_e = jnp.minimum(jnp.searchsorted(pad_ends, blk_start, side='right'), N_EXPERTS - 1).astype(i32)
    return dest.reshape(T, TOP_K), row_tok, blk_e, n_used.reshape(1)


def _t5_bucket(rel):
    n = jnp.maximum(-rel, 0)
    max_exact = NUM_BUCKETS // 2
    nf = jnp.maximum(n, 1).astype(f32)
    large = max_exact + (jnp.log(nf / max_exact) / math.log(MAX_DISTANCE / max_exact)
                         * (NUM_BUCKETS - max_exact)).astype(i32)
    large = jnp.minimum(large, NUM_BUCKETS - 1)
    return jnp.where(n < max_exact, n, large)


def _bias_tile(table, rel, mask, shift=None):
    t = table.astype(f32)
    if shift is not None:
        t = t - shift[None, :]
    bucket = jnp.where(mask, _t5_bucket(rel), -1)[None]
    b = jnp.full((table.shape[1],) + rel.shape, NEG_INF, f32)
    for i in range(NUM_BUCKETS):
        b = jnp.where(bucket == i, t[i][:, None, None], b)
    return b


def _group_rows(b, comps):
    H, Tq, Tk = b.shape
    b = b.reshape(KV_HEADS, REP, 1, Tq, Tk)
    b = jnp.broadcast_to(b, (KV_HEADS, REP, comps, Tq, Tk))
    return b.reshape(KV_HEADS, REP * comps * Tq, Tk)


def _mem_attn_kernel(q_ref, k_ref, v_ref, o_ref, *, interleaved):
    scale = HEAD_DIM ** -0.5
    for h in range(MEM_HEADS):
        sl = slice(h * HEAD_DIM, (h + 1) * HEAD_DIM)
        if interleaved:
            k = k_ref[0, 0, pl.ds(h, MEM_LEN, stride=MEM_HEADS), :]
            v = v_ref[0, 0, pl.ds(h, MEM_LEN, stride=MEM_HEADS), :]
        else:
            k = k_ref[0, :, sl]
            v = v_ref[0, :, sl]
        s = _dot_nt(q_ref[:, sl].astype(bf16), k.astype(bf16)) * scale
        m = jnp.max(s, -1, keepdims=True)
        e = jnp.exp(s - m)
        p = e * (1.0 / jnp.sum(e, -1, keepdims=True))
        o_ref[:, sl] = _dot(p.astype(bf16), v.astype(bf16)).astype(o_ref.dtype)


def mem_attend(p, row_block0, n_seq, tq, tiles_per_seq, mem_k, mem_v, k_map, v_map, kv_block):
    rows = n_seq * tiles_per_seq * tq
    return pl.pallas_call(
        functools.partial(_mem_attn_kernel, interleaved=len(kv_block) == 4),
        grid=(n_seq, tiles_per_seq),
        in_specs=[pl.BlockSpec((tq, MEM_W), lambda n, i: (row_block0 + n * tiles_per_seq + i, QM_COL // MEM_HEADS)),
                  pl.BlockSpec(kv_block, k_map), pl.BlockSpec(kv_block, v_map)],
        out_specs=pl.BlockSpec((tq, MEM_W), lambda n, i: (n * tiles_per_seq + i, 0)),
        out_shape=jax.ShapeDtypeStruct((rows, MEM_W), f32 if tq < 16 else bf16),
        compiler_params=_params(2),
        name="mem_attn",
    )(p, mem_k, mem_v)


def _softmax_sink_pv(s, sink_col, v):
    m = jnp.maximum(jnp.max(s, -1, keepdims=True), sink_col)
    e = jnp.exp(s - m)
    den = jnp.sum(e, -1, keepdims=True) + jnp.exp(sink_col - m)
    p = e * (1.0 / den)
    return _dot(p.astype(bf16), v)


def _sink_col(sink_ref, g, tq):
    return jnp.concatenate([jnp.full((tq, 1), sink_ref[REP * g + r], f32) for r in range(REP)], axis=0)


def _swa_prompt_kernel(q_ref, kc_ref, kp_ref, vc_ref, vp_ref, bias_ref, sink_ref, o_ref):
    i = pl.program_id(1)
    scale = HEAD_DIM ** -0.5
    for g in range(KV_HEADS):
        sl = slice(g * HEAD_DIM, (g + 1) * HEAD_DIM)
        heads = [slice((REP * g + r) * HEAD_DIM, (REP * g + r + 1) * HEAD_DIM) for r in range(REP)]
        kk = jnp.concatenate([kp_ref[:, sl], kc_ref[:, sl]], axis=0).astype(bf16)
        vt = jnp.concatenate([vp_ref[:, sl], vc_ref[:, sl]], axis=0).T.astype(bf16)
        qt = jnp.concatenate([q_ref[:, h].T for h in heads], axis=1).astype(bf16)
        s = _dot(kk, qt) * scale + bias_ref[g]
        key = lax.broadcasted_iota(i32, s.shape, 0)
        s = jnp.where((key >= WINDOW) | (i > 0), s, NEG_INF)
        sink = jnp.concatenate([jnp.full((1, WINDOW), sink_ref[REP * g + r], f32) for r in range(REP)], axis=1)
        m = jnp.maximum(jnp.max(s, axis=0, keepdims=True), sink)
        e = jnp.exp(s - m)
        den = jnp.sum(e, axis=0, keepdims=True) + jnp.exp(sink - m)
        o = _dot(vt, (e * (1.0 / den)).astype(bf16))
        for r, h in enumerate(heads):
            o_ref[:, h] = o[:, r * WINDOW:(r + 1) * WINDOW].T.astype(o_ref.dtype)


def swa_prompt(p, B, S, bias, sink):
    nb = S // WINDOW
    cur = lambda c: (lambda b, i: (b * nb + i, c))
    prev = lambda c: (lambda b, i: (b * nb + jnp.maximum(i - 1, 0), c))
    kc, vc = K_COL // KV_HEADS, V_COL // KV_HEADS
    return pl.pallas_call(
        _swa_prompt_kernel,
        grid=(B, nb),
        in_specs=[pl.BlockSpec((WINDOW, Q_W), cur(0)),
                  pl.BlockSpec((WINDOW, KV_W), cur(kc)), pl.BlockSpec((WINDOW, KV_W), prev(kc)),
                  pl.BlockSpec((WINDOW, KV_W), cur(vc)), pl.BlockSpec((WINDOW, KV_W), prev(vc)),
                  pl.BlockSpec((KV_HEADS, 2 * WINDOW, REP * WINDOW), lambda b, i: (0, 0, 0)),
                  pl.BlockSpec(memory_space=pltpu.SMEM)],
        out_specs=pl.BlockSpec((WINDOW, Q_W), lambda b, i: (b * nb + i, 0)),
        out_shape=jax.ShapeDtypeStruct((B * S, Q_W), bf16),
        compiler_params=_params(2),
        name="swa_prompt",
    )(p, p, p, p, p, bias, sink)


def _swa_sample_kernel(q_ref, kn_ref, vn_ref, wk_ref, wv_ref, bias_ref, sink_ref, o_ref, nk_ref, nv_ref, *, T):
    scale = HEAD_DIM ** -0.5
    wk = wk_ref[...].reshape(WINDOW, KV_W)
    wv = wv_ref[...].reshape(WINDOW, KV_W)
    kn = kn_ref[...]
    vn = vn_ref[...]
    pad = jnp.zeros((WINDOW - T, HEAD_DIM), f32)
    for g in range(KV_HEADS):
        sl = slice(g * HEAD_DIM, (g + 1) * HEAD_DIM)
        kk = jnp.concatenate([wk[:, sl], kn[:, sl], pad], axis=0).astype(bf16)
        vv = jnp.concatenate([wv[:, sl], vn[:, sl], pad], axis=0).astype(bf16)
        q3 = jnp.concatenate([q_ref[:, (REP * g + r) * HEAD_DIM:(REP * g + r + 1) * HEAD_DIM]
                              for r in range(REP)], axis=0).astype(bf16)
        s = _dot_nt(q3, kk) * scale + bias_ref[g]
        o = _softmax_sink_pv(s, _sink_col(sink_ref, g, T), vv)
        for r in range(REP):
            o_ref[:, (REP * g + r) * HEAD_DIM:(REP * g + r + 1) * HEAD_DIM] = o[r * T:(r + 1) * T]
    nk_ref[0, 0:WINDOW - T, :] = wk[T:, :]
    nk_ref[0, WINDOW - T:, :] = kn
    nv_ref[0, 0:WINDOW - T, :] = wv[T:, :]
    nv_ref[0, WINDOW - T:, :] = vn


def swa_sample(p, row_block0, DB, T, win_k, win_v, li, bias, sink):
    kc, vc = K_COL // KV_HEADS, V_COL // KV_HEADS
    win = lambda n: (li, n, 0, 0)
    return pl.pallas_call(
        functools.partial(_swa_sample_kernel, T=T),
        grid=(DB,),
        in_specs=[pl.BlockSpec((T, Q_W), lambda n: (row_block0 + n, 0)),
                  pl.BlockSpec((T, KV_W), lambda n: (row_block0 + n, kc)),
                  pl.BlockSpec((T, KV_W), lambda n: (row_block0 + n, vc)),
                  pl.BlockSpec((1, 1, WINDOW, KV_W), win), pl.BlockSpec((1, 1, WINDOW, KV_W), win),
                  pl.BlockSpec((KV_HEADS, REP * T, 2 * WINDOW), lambda n: (0, 0, 0)),
                  pl.BlockSpec(memory_space=pltpu.SMEM)],
        out_specs=[pl.BlockSpec((T, Q_W), lambda n: (n, 0)),
                   pl.BlockSpec((1, WINDOW, KV_W), lambda n: (n, 0, 0)),
                   pl.BlockSpec((1, WINDOW, KV_W), lambda n: (n, 0, 0))],
        out_shape=[jax.ShapeDtypeStruct((DB * T, Q_W), f32),
                   jax.ShapeDtypeStruct((DB, WINDOW, KV_W), f32),
                   jax.ShapeDtypeStruct((DB, WINDOW, KV_W), f32)],
        compiler_params=_params(1),
        name="swa_sample",
    )(p, p, p, win_k, win_v, bias, sink)


def _split_components(q, scale):
    lane = lax.broadcasted_iota(i32, q.shape, 1)
    qs = q * scale
    return [jnp.where(lane < DIFF_DIM, qs, 0.0), jnp.where(lane >= DIFF_DIM, qs, 0.0)]


def _diff_finish(o0, o1, lam, gain, lam_init, axis):
    o = o0 - lam * o1
    return o * lax.rsqrt(jnp.mean(o * o, axis, keepdims=True) + RMS_EPS) * gain * (1.0 - lam_init)


def _diff_prompt_kernel(q_ref, k_ref, v_ref, bias_ref, lam_ref, gain_ref, o_ref,
                        kb_scr, vt_scr, qpt_scr, m_scr, l_scr, acc_scr, *, lam_init):
    i = pl.program_id(2)
    tq, tk = DIFF_TQ, DIFF_TK
    n_chunks = k_ref.shape[0] // tk

    @pl.when(i == 0)
    def _():
        for c in range(n_chunks):
            kb_scr[c] = k_ref[c * tk:(c + 1) * tk, :].astype(bf16)
            vt_scr[c] = v_ref[c * tk:(c + 1) * tk, :].T.astype(bf16)

    pieces = []
    for r in range(REP):
        for part in _split_components(q_ref[:, r * HEAD_DIM:(r + 1) * HEAD_DIM], DIFF_DIM ** -0.5 * LOG2_E):
            pieces.append(part.T)
    qpt_scr[...] = jnp.concatenate(pieces, axis=1).astype(bf16)
    m_scr[...] = jnp.full(m_scr.shape, NEG_INF, f32)
    l_scr[...] = jnp.zeros(l_scr.shape, f32)
    acc_scr[...] = jnp.zeros(acc_scr.shape, f32)

    def chunk(c, n, bias):
        if n == 1:
            kb, vt = kb_scr[c], vt_scr[c]
        else:
            kb = kb_scr[pl.ds(c, n)].reshape(n * tk, HEAD_DIM)
            vt = jnp.concatenate([vt_scr[c + d] for d in range(n)], axis=1)
        s = _dot(kb, qpt_scr[...])
        if bias is not None:
            s = s + bias
        m_old = m_scr[...]
        m_new = jnp.maximum(m_old, jnp.max(s, axis=0, keepdims=True))
        p = jnp.exp2(s - m_new)
        alpha = jnp.exp2(m_old - m_new)
        l_scr[...] = alpha * l_scr[...] + jnp.sum(p, axis=0, keepdims=True)
        acc_scr[...] = alpha * acc_scr[...] + _dot(vt, p.astype(bf16))
        m_scr[...] = m_new

    step = tq // tk
    for d in range(step):
        chunk(i * step + d, 1, bias_ref[0, tq + d * tk:tq + (d + 1) * tk, :])

    @pl.when(i > 0)
    def _():
        for d in range(step):
            chunk((i - 1) * step + d, 1, bias_ref[0, d * tk:(d + 1) * tk, :])

    n_far = jnp.maximum(i - 1, 0) * step
    n_group = n_far // DIFF_FAR_CHUNKS

    def far_group(j, carry):
        chunk(j * DIFF_FAR_CHUNKS, DIFF_FAR_CHUNKS, None)
        return carry

    def far_single(c, carry):
        chunk(c, 1, None)
        return carry

    lax.fori_loop(0, n_group, far_group, 0)
    lax.fori_loop(n_group * DIFF_FAR_CHUNKS, n_far, far_single, 0)

    acc = acc_scr[...] * (1.0 / l_scr[...])
    lam = lam_ref[0]
    for r in range(REP):
        o = _diff_finish(acc[:, (2 * r) * tq:(2 * r + 1) * tq], acc[:, (2 * r + 1) * tq:(2 * r + 2) * tq],
                         lam, gain_ref[...], lam_init, 0)
        o_ref[:, r * HEAD_DIM:(r + 1) * HEAD_DIM] = o.T.astype(o_ref.dtype)


def diff_prompt(p, B, S, bias_t, lam, gain_col, lam_init):
    nq = S // DIFF_TQ
    cols = 2 * REP * DIFF_TQ
    n_chunks = S // DIFF_TK
    return pl.pallas_call(
        functools.partial(_diff_prompt_kernel, lam_init=lam_init),
        grid=(B, KV_HEADS, nq),
        in_specs=[pl.BlockSpec((DIFF_TQ, REP * HEAD_DIM), lambda b, g, i: (b * nq + i, g)),
                  pl.BlockSpec((S, HEAD_DIM), lambda b, g, i: (b, K_COL + g)),
                  pl.BlockSpec((S, HEAD_DIM), lambda b, g, i: (b, V_COL + g)),
                  pl.BlockSpec((1, 2 * DIFF_TQ, cols), lambda b, g, i: (g, 0, 0)),
                  pl.BlockSpec(memory_space=pltpu.SMEM),
                  pl.BlockSpec((HEAD_DIM, 1), lambda b, g, i: (0, 0))],
        out_specs=pl.BlockSpec((DIFF_TQ, REP * HEAD_DIM), lambda b, g, i: (b * nq + i, g)),
        out_shape=jax.ShapeDtypeStruct((B * S, Q_W), bf16),
        scratch_shapes=[pltpu.VMEM((n_chunks, DIFF_TK, HEAD_DIM), bf16),
                        pltpu.VMEM((n_chunks, HEAD_DIM, DIFF_TK), bf16),
                        pltpu.VMEM((HEAD_DIM, cols), bf16),
                        pltpu.VMEM((1, cols), f32),
                        pltpu.VMEM((1, cols), f32),
                        pltpu.VMEM((HEAD_DIM, cols), f32)],
        compiler_params=_params(3),
        name="diff_prompt",
    )(p, p, p, bias_t, lam, gain_col)


def _diff_sample_kernel(pt_ref, q_ref, kn_ref, vn_ref, *rest, T, lam_init, n_steps):
    npg = PAGES_PER_STEP
    k_refs, v_refs = rest[:npg], rest[npg:2 * npg]
    bias_last_ref, bias_new_ref, lam_ref, gain_ref, o_ref, qp_scr, m_scr, l_scr, acc_scr = rest[2 * npg:]
    s_id = pl.program_id(1)
    rows = 2 * REP * T

    @pl.when(s_id == 0)
    def _():
        pieces = []
        for h in range(N_HEADS):
            pieces += _split_components(q_ref[:, h * HEAD_DIM:(h + 1) * HEAD_DIM], DIFF_DIM ** -0.5)
        qp_scr[...] = jnp.concatenate(pieces, axis=0)
        m_scr[...] = jnp.full(m_scr.shape, NEG_INF, f32)
        l_scr[...] = jnp.zeros(l_scr.shape, f32)
        acc_scr[...] = jnp.zeros(acc_scr.shape, f32)

    def update(s, vs):
        m_old = m_scr[...]
        m_new = jnp.maximum(m_old, jnp.max(s, -1, keepdims=True))
        p = jnp.exp(s - m_new)
        alpha = jnp.exp(m_old - m_new)
        l_scr[...] = alpha * l_scr[...] + jnp.sum(p, -1, keepdims=True)
        pb = p.astype(bf16)
        pv = jnp.concatenate([_dot(pb[g * rows:(g + 1) * rows], vs[g]) for g in range(KV_HEADS)], axis=0)
        acc_scr[...] = alpha * acc_scr[...] + pv
        m_scr[...] = m_new

    def scores(ks):
        qp = qp_scr[...].astype(bf16)
        return jnp.concatenate([_dot_nt(qp[g * rows:(g + 1) * rows], ks[g]) for g in range(KV_HEADS)], axis=0)

    def head_rows(refs, g):
        return jnp.concatenate([r[0, 0, pl.ds(g, PAGE_SIZE, stride=KV_HEADS), :] for r in refs],
                               axis=0).astype(bf16)

    last = s_id == n_steps - 1
    s = scores([head_rows(k_refs, g) for g in range(KV_HEADS)])
    tail = s[:, (npg - 1) * PAGE_SIZE:] + jnp.where(last, bias_last_ref[...], 0.0)
    s = jnp.concatenate([s[:, :(npg - 1) * PAGE_SIZE], tail], axis=1)
    update(s, [head_rows(v_refs, g) for g in range(KV_HEADS)])

    @pl.when(last)
    def _():
        pad = jnp.zeros((PAGE_SIZE - T, HEAD_DIM), f32)
        kn = [jnp.concatenate([kn_ref[:, g * HEAD_DIM:(g + 1) * HEAD_DIM], pad], axis=0).astype(bf16)
              for g in range(KV_HEADS)]
        vn = [jnp.concatenate([vn_ref[:, g * HEAD_DIM:(g + 1) * HEAD_DIM], pad], axis=0).astype(bf16)
              for g in range(KV_HEADS)]
        update(scores(kn) + bias_new_ref[...], vn)
        acc = acc_scr[...] * (1.0 / l_scr[...])
        lam = lam_ref[0]
        for h in range(N_HEADS):
            o = _diff_finish(acc[(2 * h) * T:(2 * h + 1) * T], acc[(2 * h + 1) * T:(2 * h + 2) * T],
                             lam, gain_ref[...], lam_init, -1)
            o_ref[:, h * HEAD_DIM:(h + 1) * HEAD_DIM] = o


def diff_sample(p, row_block0, DB, T, cache_k, cache_v, li, page_table, bias_last, bias_new, lam, gain, lam_init):
    n_pages = PAST_LEN // PAGE_SIZE
    npg = PAGES_PER_STEP
    n_steps = n_pages // npg
    rows = 2 * N_HEADS * T
    kc, vc = K_COL // KV_HEADS, V_COL // KV_HEADS

    def page(j):
        return lambda n, s, pt: (li, pt[n * n_pages + s * npg + j], 0, 0)

    page_block = (1, 1, PAGE_SIZE * KV_HEADS, HEAD_DIM)
    const2 = lambda n, s, pt: (0, 0)
    in_specs = ([pl.BlockSpec((T, Q_W), lambda n, s, pt: (row_block0 + n, 0)),
                 pl.BlockSpec((T, KV_W), lambda n, s, pt: (row_block0 + n, kc)),
                 pl.BlockSpec((T, KV_W), lambda n, s, pt: (row_block0 + n, vc))]
                + [pl.BlockSpec(page_block, page(j)) for j in range(npg)]
                + [pl.BlockSpec(page_block, page(j)) for j in range(npg)]
                + [pl.BlockSpec((rows, PAGE_SIZE), const2),
                   pl.BlockSpec((rows, PAGE_SIZE), const2),
                   pl.BlockSpec(memory_space=pltpu.SMEM),
                   pl.BlockSpec((1, HEAD_DIM), const2)])
    return pl.pallas_call(
        functools.partial(_diff_sample_kernel, T=T, lam_init=lam_init, n_steps=n_steps),
        grid_spec=pltpu.PrefetchScalarGridSpec(
            num_scalar_prefetch=1,
            grid=(DB, n_steps),
            in_specs=in_specs,
            out_specs=pl.BlockSpec((T, Q_W), lambda n, s, pt: (n, 0)),
            scratch_shapes=[pltpu.VMEM((rows, HEAD_DIM), f32),
                            pltpu.VMEM((rows, 1), f32),
                            pltpu.VMEM((rows, 1), f32),
                            pltpu.VMEM((rows, HEAD_DIM), f32)],
        ),
        out_shape=jax.ShapeDtypeStruct((DB * T, Q_W), f32),
        compiler_params=_params(2),
        name="diff_sample",
    )(page_table, p, p, p, *([cache_k] * npg), *([cache_v] * npg), bias_last, bias_new, lam, gain)


def _row_tile(total, candidates):
    for t in candidates:
        if total % t == 0:
            return t
    raise ValueError(f"no row tile for {total} rows")


def kernel(x_prompt, x_sample, cache_win_k, cache_win_v, cache_diff_k, cache_diff_v, cache_mem_k, cache_mem_v,
           page_table, mem_prompt, w_in, w_mem_kv, w_o, rel_bias, sinks, lam_q1, lam_k1, lam_q2, lam_k2, subln_g,
           ln1_g, ln1_b, ln2_g, ln2_b, w_router, router_bias, w_gate_up, w_down):
    B, S, _ = x_prompt.shape
    DB, T, _ = x_sample.shape
    TP, TS = B * S, DB * T
    TT = TP + TS
    assert S % DIFF_TQ == 0 and S % WINDOW == 0 and TP % T == 0 and T % 8 == 0
    n_swa, n_pool = cache_win_k.shape[0], cache_diff_k.shape[1]
    sample_block0 = TP // T

    x = jnp.concatenate([x_prompt.reshape(TP, D_MODEL), x_sample.reshape(TS, D_MODEL)], axis=0)
    xb = x.astype(bf16)
    w_in_b = w_in.astype(bf16)
    w_o_b = w_o.astype(bf16)
    w_mem_b = w_mem_kv.astype(bf16)
    mem_b = mem_prompt.reshape(B * MEM_LEN, D_MODEL).astype(bf16)
    w_router_t = w_router.T
    router_bias_b = jnp.broadcast_to(router_bias.astype(f32)[:, None], (N_EXPERTS, 128))
    win_k = cache_win_k.reshape(n_swa, DB, WINDOW, KV_W)
    win_v = cache_win_v.reshape(n_swa, DB, WINDOW, KV_W)
    pool_k = cache_diff_k.reshape(-1, n_pool, PAGE_SIZE * KV_HEADS, HEAD_DIM)
    pool_v = cache_diff_v.reshape(-1, n_pool, PAGE_SIZE * KV_HEADS, HEAD_DIM)
    cmem_k = cache_mem_k.reshape(DEPTH, DB, MEM_LEN * MEM_HEADS, HEAD_DIM)
    cmem_v = cache_mem_v.reshape(DEPTH, DB, MEM_LEN * MEM_HEADS, HEAD_DIM)
    pt_flat = page_table.reshape(-1).astype(i32)

    far = rel_bias[NUM_BUCKETS - 1].astype(f32)
    qo = jnp.arange(WINDOW)
    rel = (jnp.arange(2 * WINDOW) - WINDOW)[None, :] - qo[:, None]
    bias_swa_p = jnp.swapaxes(_group_rows(_bias_tile(rel_bias, rel, (rel <= 0) & (rel > -WINDOW)), 1), 1, 2)
    tt = jnp.arange(T)
    kpos = jnp.concatenate([jnp.arange(WINDOW) - WINDOW, tt, jnp.full((WINDOW - T,), T)])
    rel = kpos[None, :] - tt[:, None]
    bias_swa_s = _group_rows(_bias_tile(rel_bias, rel, (rel <= 0) & (rel > -WINDOW)), 1)
    qo = jnp.arange(DIFF_TQ)
    rel = (jnp.arange(2 * DIFF_TQ) - DIFF_TQ)[None, :] - qo[:, None]
    bias_diff_p = jnp.swapaxes(_group_rows(_bias_tile(rel_bias, rel, rel <= 0, far), 2), 1, 2) * LOG2_E
    rel = (jnp.arange(PAGE_SIZE) - PAGE_SIZE)[None, :] - tt[:, None]
    bias_diff_last = _group_rows(_bias_tile(rel_bias, rel, rel <= 0, far), 2).reshape(-1, PAGE_SIZE)
    kpos = jnp.concatenate([tt, jnp.full((PAGE_SIZE - T,), T)])
    rel = kpos[None, :] - tt[:, None]
    bias_diff_new = _group_rows(_bias_tile(rel_bias, rel, rel <= 0, far), 2).reshape(-1, PAGE_SIZE)

    tm_mm = _row_tile(TT, (1056, 1024, 512, 256))
    tm_ln = _row_tile(TT, (352, 256, 128))
    tm_wo = _row_tile(math.gcd(TP, TS), (256, 128))
    tm_rt = _row_tile(TT, (1408, 1024, 512, 256, 128))
    tq_mem = _row_tile(S, (512, 256, 128))

    win_k_p, win_v_p, win_k_s, win_v_s = [], [], [], []
    diff_k_p, diff_v_p, diff_k_s, diff_v_s = [], [], [], []
    mem_k_p, mem_v_p = [], []
    for l in range(DEPTH):
        i = l // 2
        p, k_rows, v_rows = in_proj(xb, w_in_b[l], tm_mm)
        k_p = k_rows[:TP * KV_HEADS].reshape(B, S, KV_HEADS, HEAD_DIM)
        v_p = v_rows[:TP * KV_HEADS].reshape(B, S, KV_HEADS, HEAD_DIM)
        if l % 2 == 0:
            sink = sinks[i].astype(f32)
            self_p = swa_prompt(p, B, S, bias_swa_p, sink)
            self_s, nk, nv = swa_sample(p, sample_block0, DB, T, win_k, win_v, i, bias_swa_s, sink)
            win_k_p.append(k_p[:, S - WINDOW:])
            win_v_p.append(v_p[:, S - WINDOW:])
            win_k_s.append(nk.reshape(DB, WINDOW, KV_HEADS, HEAD_DIM))
            win_v_s.append(nv.reshape(DB, WINDOW, KV_HEADS, HEAD_DIM))
        else:
            lam_init = 0.8 - 0.6 * math.exp(-0.3 * l)
            lam = (jnp.exp(jnp.sum(lam_q1[i].astype(f32) * lam_k1[i].astype(f32)))
                   - jnp.exp(jnp.sum(lam_q2[i].astype(f32) * lam_k2[i].astype(f32))) + lam_init).reshape(1)
            gain = subln_g[i].astype(f32).reshape(1, HEAD_DIM)
            self_p = diff_prompt(p, B, S, bias_diff_p, lam, gain.reshape(HEAD_DIM, 1), lam_init)
            self_s = diff_sample(p, sample_block0, DB, T, pool_k, pool_v, i, pt_flat,
                                 bias_diff_last, bias_diff_new, lam, gain, lam_init)
            diff_k_p.append(k_p)
            diff_v_p.append(v_p)
            diff_k_s.append(k_rows[TP * KV_HEADS:].reshape(DB, T, KV_HEADS, HEAD_DIM))
            diff_v_s.append(v_rows[TP * KV_HEADS:].reshape(DB, T, KV_HEADS, HEAD_DIM))
        mkv = matmul(mem_b, w_mem_b[l], B * MEM_LEN, 512).reshape(B, MEM_LEN, 2 * MEM_W)
        mem_k_p.append(mkv[:, :, :MEM_W].reshape(B, MEM_LEN, MEM_HEADS, HEAD_DIM))
        mem_v_p.append(mkv[:, :, MEM_W:].reshape(B, MEM_LEN, MEM_HEADS, HEAD_DIM))
        cross_p = mem_attend(p, 0, B, tq_mem, S // tq_mem, mkv, mkv,
                             lambda n, t: (n, 0, 0), lambda n, t: (n, 0, 1), (1, MEM_LEN, MEM_W))
        cross_s = mem_attend(p, sample_block0, DB, T, 1, cmem_k, cmem_v,
                             lambda n, t: (l, n, 0, 0), lambda n, t: (l, n, 0, 0),
                             (1, 1, MEM_LEN * MEM_HEADS, HEAD_DIM))
        x, x_packed = wo_ln(self_p, cross_p, self_s, cross_s, x, w_o_b[l, :Q_W], w_o_b[l, Q_W:],
                            ln1_g[l].reshape(1, D_MODEL), ln1_b[l].reshape(1, D_MODEL), tm_wo)
        e_idx, gate = router(x, w_router_t, router_bias_b, tm_rt)
        dest, row_tok, blk_e, n_used = moe_dispatch(e_idx)
        xs = gather_rows(x_packed, row_tok, n_used, MOE_TM, jnp.uint32)
        yb = moe_ffn_sorted(xs, blk_e, n_used, w_gate_up, w_down, l)
        x, xb = combine_ln(x, yb, dest, gate.T,
                           ln2_g[l].reshape(1, D_MODEL), ln2_b[l].reshape(1, D_MODEL), tm_ln)
    return (x[:TP].reshape(B, S, D_MODEL), x[TP:].reshape(DB, T, D_MODEL),
            jnp.stack(win_k_p), jnp.stack(win_v_p), jnp.stack(win_k_s), jnp.stack(win_v_s),
            jnp.stack(diff_k_p), jnp.stack(diff_v_p), jnp.stack(diff_k_s), jnp.stack(diff_v_s),
            jnp.stack(mem_k_p), jnp.stack(mem_v_p))
```

```python
import functools
import math

import jax
import jax.numpy as jnp
from jax import lax
from jax.experimental import pallas as pl
from jax.experimental.pallas import tpu as pltpu

f32 = jnp.float32
bf16 = jnp.bfloat16
i32 = jnp.int32

D_MODEL = 2048
DEPTH = 4
PAST_LEN = 16384
PAGE_SIZE = 128
HEAD_DIM = 128
N_HEADS = 12
KV_HEADS = 4
REP = N_HEADS // KV_HEADS
DIFF_DIM = HEAD_DIM // 2
MEM_HEADS = 4
MEM_LEN = 256
WINDOW = 128
NUM_BUCKETS = 32
MAX_DISTANCE = 128
N_EXPERTS = 16
N_GROUPS = 4
EXPERTS_PER_GROUP = N_EXPERTS // N_GROUPS
TOP_K = 2
D_EXPERT = D_MODEL // 2
Q_W = N_HEADS * HEAD_DIM
KV_W = KV_HEADS * HEAD_DIM
MEM_W = MEM_HEADS * HEAD_DIM
IN_W = Q_W + 2 * KV_W + MEM_W
ALPHA = (2.0 * DEPTH) ** 0.25
LN_EPS = 1e-5
RMS_EPS = 1e-5
NEG_INF = -1e30
LOG2_E = math.log2(math.e)

VMEM_LIMIT_BYTES = 56 * 1024 * 1024

K_COL = Q_W // HEAD_DIM
V_COL = (Q_W + KV_W) // HEAD_DIM
QM_COL = (Q_W + 2 * KV_W) // HEAD_DIM

DIFF_TQ = 256
DIFF_TK = 256
DIFF_FAR_CHUNKS = 2
PAGES_PER_STEP = 16
MOE_TM = 512


def _params(n_axes):
    return pltpu.CompilerParams(dimension_semantics=("arbitrary",) * n_axes,
                                vmem_limit_bytes=VMEM_LIMIT_BYTES)


def _dot_nt(a, b):
    return lax.dot_general(a, b, (((1,), (1,)), ((), ())), preferred_element_type=f32)


def _dot(a, b):
    return jnp.dot(a, b, preferred_element_type=f32)


def _mm_kernel(a_ref, b_ref, o_ref):
    o_ref[...] = _dot(a_ref[...], b_ref[...])


def matmul(a, b, tm, tn):
    M, K = a.shape
    N = b.shape[1]
    return pl.pallas_call(
        _mm_kernel,
        grid=(M // tm, N // tn),
        in_specs=[pl.BlockSpec((tm, K), lambda i, j: (i, 0)),
                  pl.BlockSpec((K, tn), lambda i, j: (0, j))],
        out_specs=pl.BlockSpec((tm, tn), lambda i, j: (i, j)),
        out_shape=jax.ShapeDtypeStruct((M, N), f32),
        compiler_params=_params(2),
        name="matmul",
    )(a, b)


def _in_proj_kernel(a_ref, b_ref, o_ref, k_ref, v_ref):
    acc = _dot(a_ref[...], b_ref[0].astype(bf16))
    o_ref[...] = acc
    tm = acc.shape[0]
    j = pl.program_id(1)
    for tile, ref in ((Q_W // KV_W, k_ref), ((Q_W + KV_W) // KV_W, v_ref)):
        @pl.when(j == tile)
        def _():
            for h in range(KV_HEADS):
                ref[pl.ds(h, tm, stride=KV_HEADS), :] = acc[:, h * HEAD_DIM:(h + 1) * HEAD_DIM]


def in_proj(a, w, l, tm):
    M, K = a.shape
    kv_shape = jax.ShapeDtypeStruct((M * KV_HEADS, HEAD_DIM), f32)
    kv_spec = pl.BlockSpec((tm * KV_HEADS, HEAD_DIM), lambda i, j: (i, 0))
    return pl.pallas_call(
        _in_proj_kernel,
        grid=(M // tm, IN_W // KV_W),
        in_specs=[pl.BlockSpec((tm, K), lambda i, j: (i, 0)),
                  pl.BlockSpec((1, K, KV_W), lambda i, j: (l, 0, j))],
        out_specs=[pl.BlockSpec((tm, KV_W), lambda i, j: (i, j)), kv_spec, kv_spec],
        out_shape=[jax.ShapeDtypeStruct((M, IN_W), f32), kv_shape, kv_shape],
        compiler_params=_params(2),
        name="in_proj",
    )(a, w)


def _layer_norm(y, g, b):
    mu = jnp.mean(y, -1, keepdims=True)
    yc = y - mu
    var = jnp.mean(yc * yc, -1, keepdims=True)
    return yc * lax.rsqrt(var + LN_EPS) * g + b


def _wo_ln_kernel(msp_ref, mcp_ref, mss_ref, mcs_ref, x_ref, ws_ref, wc_ref, g_ref, b_ref, o_ref,
                  *, n_prompt_tiles):
    def body(ms, mc):
        f = _dot(ms, ws_ref[...]) + _dot(mc, wc_ref[...])
        o_ref[...] = _layer_norm(ALPHA * x_ref[...] + f, g_ref[...], b_ref[...])

    is_prompt = pl.program_id(0) < n_prompt_tiles

    @pl.when(is_prompt)
    def _():
        body(msp_ref[...], mcp_ref[...])

    @pl.when(jnp.logical_not(is_prompt))
    def _():
        body(mss_ref[...].astype(bf16), mcs_ref[...].astype(bf16))


def wo_ln(self_p, cross_p, self_s, cross_s, x, w_self, w_cross, g, b, tm):
    M = x.shape[0]
    n_p = self_p.shape[0] // tm
    assert self_p.shape[0] % tm == 0 and self_s.shape[0] % tm == 0
    row = lambda i: (i, 0)
    prow = lambda i: (jnp.minimum(i, n_p - 1), 0)
    srow = lambda i: (jnp.maximum(i - n_p, 0), 0)
    const = lambda i: (0, 0)
    return pl.pallas_call(
        functools.partial(_wo_ln_kernel, n_prompt_tiles=n_p),
        grid=(M // tm,),
        in_specs=[pl.BlockSpec((tm, Q_W), prow), pl.BlockSpec((tm, MEM_W), prow),
                  pl.BlockSpec((tm, Q_W), srow), pl.BlockSpec((tm, MEM_W), srow),
                  pl.BlockSpec((tm, D_MODEL), row),
                  pl.BlockSpec((Q_W, D_MODEL), const), pl.BlockSpec((MEM_W, D_MODEL), const),
                  pl.BlockSpec((1, D_MODEL), const), pl.BlockSpec((1, D_MODEL), const)],
        out_specs=pl.BlockSpec((tm, D_MODEL), row),
        out_shape=jax.ShapeDtypeStruct((M, D_MODEL), f32),
        compiler_params=_params(1),
        name="wo_ln",
    )(self_p, cross_p, self_s, cross_s, x, w_self, w_cross, g, b)


def _pipelined_row_gather(i, n_used, idx_first_ref, idx_next_ref, src_ref, buf, sem, rows):
    def row_copy(slot, r, src_row):
        return pltpu.make_async_copy(src_ref.at[pl.ds(src_row, 1)], buf.at[slot, pl.ds(r, 1)], sem.at[slot])

    def start_block(slot, idx_ref):
        def body(h, carry):
            for q in range(2):
                r = 2 * h + q
                row_copy(slot, r, idx_ref[0, 0, r]).start(priority=q)
            return carry
        lax.fori_loop(0, rows // 2, body, 0, unroll=4)

    @pl.when(i == 0)
    def _():
        start_block(0, idx_first_ref)

    @pl.when(i + 1 < n_used)
    def _():
        start_block((i + 1) % 2, idx_next_ref)

    @pl.when(i < n_used)
    def _():
        def body(r, carry):
            row_copy(i % 2, r, 0).wait()
            return carry
        lax.fori_loop(0, rows, body, 0, unroll=8)


def _gather_idx_specs(rows, n_blocks):
    first = lambda i, *_: (0, 0, 0)
    nxt = lambda i, *_: (jnp.minimum(i + 1, n_blocks - 1), 0, 0)
    return [pl.BlockSpec((1, 1, rows), first, memory_space=pltpu.SMEM),
            pl.BlockSpec((1, 1, rows), nxt, memory_space=pltpu.SMEM)]


def _gather_rows_kernel(nu_ref, idx_first_ref, idx_next_ref, src_ref, o_ref, buf, sem, *, rows):
    i = pl.program_id(0)
    used = i < nu_ref[0]
    _pipelined_row_gather(i, nu_ref[0], idx_first_ref, idx_next_ref, src_ref, buf, sem, rows)

    @pl.when(used)
    def _():
        o_ref[...] = buf[i % 2].astype(o_ref.dtype)

    @pl.when(jnp.logical_not(used))
    def _():
        o_ref[...] = jnp.zeros(o_ref.shape, o_ref.dtype)


def gather_rows(src, idx, n_used, rows, out_dtype):
    n_blocks = idx.shape[0] // rows
    D = src.shape[1]
    idx3 = idx.reshape(n_blocks, 1, rows)
    return pl.pallas_call(
        functools.partial(_gather_rows_kernel, rows=rows),
        grid_spec=pltpu.PrefetchScalarGridSpec(
            num_scalar_prefetch=1,
            grid=(n_blocks,),
            in_specs=_gather_idx_specs(rows, n_blocks) + [pl.BlockSpec(memory_space=pl.ANY)],
            out_specs=pl.BlockSpec((rows, D), lambda i, nu: (i, 0)),
            scratch_shapes=[pltpu.VMEM((2, rows, D), src.dtype), pltpu.SemaphoreType.DMA((2,))],
        ),
        out_shape=jax.ShapeDtypeStruct((n_blocks * rows, D), out_dtype),
        compiler_params=_params(1),
        name="gather_rows",
    )(n_used, idx3, idx3, src)


def _combine_ln_kernel(idx_first_ref, idx_next_ref, y_ref, x_ref, gate_ref, g_ref, b_ref, o_ref, ob_ref,
                       buf, sem, *, tm):
    i = pl.program_id(0)
    _pipelined_row_gather(i, pl.num_programs(0), idx_first_ref, idx_next_ref, y_ref, buf, sem, TOP_K * tm)
    gate = gate_ref[...]
    y = buf[i % 2]
    f = y[:tm] * gate[:, 0:1] + y[tm:] * gate[:, 1:2]
    out = _layer_norm(ALPHA * x_ref[...] + f, g_ref[...], b_ref[...])
    o_ref[...] = out
    ob_ref[...] = out.astype(bf16)


def combine_ln(x, yb, dest, gate_t, g, b, tm):
    M = x.shape[0]
    n_tiles = M // tm
    idx3 = jnp.swapaxes(dest.reshape(n_tiles, tm, TOP_K), 1, 2).reshape(n_tiles, 1, TOP_K * tm)
    row = lambda i: (i, 0)
    const = lambda i: (0, 0)
    return pl.pallas_call(
        functools.partial(_combine_ln_kernel, tm=tm),
        grid=(n_tiles,),
        in_specs=_gather_idx_specs(TOP_K * tm, n_tiles) + [
            pl.BlockSpec(memory_space=pl.ANY),
            pl.BlockSpec((tm, D_MODEL), row), pl.BlockSpec((tm, TOP_K), row),
            pl.BlockSpec((1, D_MODEL), const), pl.BlockSpec((1, D_MODEL), const)],
        out_specs=[pl.BlockSpec((tm, D_MODEL), row), pl.BlockSpec((tm, D_MODEL), row)],
        out_shape=[jax.ShapeDtypeStruct((M, D_MODEL), f32),
                   jax.ShapeDtypeStruct((M, D_MODEL), bf16)],
        scratch_shapes=[pltpu.VMEM((2, TOP_K * tm, D_MODEL), f32), pltpu.SemaphoreType.DMA((2,))],
        compiler_params=_params(1),
        name="combine_ln",
    )(idx3, idx3, yb, x, gate_t, g, b)


def _router_kernel(x_ref, wh_ref, wl_ref, bias_ref, e_ref, gate_ref):
    x = x_ref[...]
    xh = x.astype(bf16)
    xl = (x - xh.astype(f32)).astype(bf16)
    wh, wl = wh_ref[...], wl_ref[...]
    logits = _dot_nt(wh, xh) + (_dot_nt(wh, xl) + _dot_nt(wl, xh))
    scores = jax.nn.sigmoid(logits)
    sel = scores + bias_ref[...][:, 0:1]
    n = EXPERTS_PER_GROUP
    rows = [sel[e:e + 1, :] for e in range(N_EXPERTS)]
    srows = [scores[e:e + 1, :] for e in range(N_EXPERTS)]
    gscore = []
    for g in range(N_GROUPS):
        v = rows[g * n:(g + 1) * n]
        best = None
        for a in range(n):
            for b in range(a + 1, n):
                pair = v[a] + v[b]
                best = pair if best is None else jnp.maximum(best, pair)
        gscore.append(best)
    g_idx = jnp.zeros_like(gscore[0], dtype=i32)
    best = gscore[0]
    for g in range(1, N_GROUPS):
        take = gscore[g] > best
        g_idx = jnp.where(take, g, g_idx)
        best = jnp.where(take, gscore[g], best)
    ing, sg = [], []
    for a in range(n):
        va, sa = rows[a], srows[a]
        for g in range(1, N_GROUPS):
            va = jnp.where(g_idx == g, rows[g * n + a], va)
            sa = jnp.where(g_idx == g, srows[g * n + a], sa)
        ing.append(va)
        sg.append(sa)
    l_idx = [jnp.zeros_like(g_idx), jnp.zeros_like(g_idx)]
    w = [jnp.zeros_like(best), jnp.zeros_like(best)]
    for a in range(n):
        rank = jnp.zeros_like(g_idx)
        for b in range(n):
            if b == a:
                continue
            ahead = (ing[b] > ing[a]) | ((ing[b] == ing[a]) & (b < a))
            rank = rank + ahead.astype(i32)
        for k in range(TOP_K):
            hit = rank == k
            l_idx[k] = jnp.where(hit, a, l_idx[k])
            w[k] = jnp.where(hit, sg[a], w[k])
    tot = w[0] + w[1]
    e_ref[...] = jnp.concatenate([g_idx * n + l_idx[0], g_idx * n + l_idx[1]], axis=0)
    gate_ref[...] = jnp.concatenate([w[0] / tot, w[1] / tot], axis=0)


def router(x, w_router_t, router_bias, tm):
    M = x.shape[0]
    w_hi = w_router_t.astype(bf16)
    w_lo = (w_router_t - w_hi.astype(f32)).astype(bf16)
    return pl.pallas_call(
        _router_kernel,
        grid=(M // tm,),
        in_specs=[pl.BlockSpec((tm, D_MODEL), lambda i: (i, 0)),
                  pl.BlockSpec((N_EXPERTS, D_MODEL), lambda i: (0, 0)),
                  pl.BlockSpec((N_EXPERTS, D_MODEL), lambda i: (0, 0)),
                  pl.BlockSpec((N_EXPERTS, 128), lambda i: (0, 0))],
        out_specs=[pl.BlockSpec((TOP_K, tm), lambda i: (0, i)),
                   pl.BlockSpec((TOP_K, tm), lambda i: (0, i))],
        out_shape=[jax.ShapeDtypeStruct((TOP_K, M), i32),
                   jax.ShapeDtypeStruct((TOP_K, M), f32)],
        compiler_params=_params(1),
        name="router",
    )(x, w_hi, w_lo, router_bias)


def _moe_up_kernel(be_ref, nu_ref, x_ref, wg_ref, wu_ref, h_ref):
    used = pl.program_id(1) < nu_ref[0]

    @pl.when(used)
    def _():
        x = x_ref[...]
        g = _dot(x, wg_ref[0, 0].astype(bf16))
        u = _dot(x, wu_ref[0, 0].astype(bf16))
        h_ref[...] = (g * jax.nn.sigmoid(g) * u).astype(bf16)

    @pl.when(jnp.logical_not(used))
    def _():
        h_ref[...] = jnp.zeros(h_ref.shape, h_ref.dtype)


def _moe_down_kernel(be_ref, nu_ref, h_ref, wd_ref, y_ref):
    used = pl.program_id(1) < nu_ref[0]

    @pl.when(used)
    def _():
        y_ref[...] = _dot(h_ref[...], wd_ref[0, 0].astype(bf16))

    @pl.when(jnp.logical_not(used))
    def _():
        y_ref[...] = jnp.zeros(y_ref.shape, y_ref.dtype)


def moe_ffn_sorted(xs, blk_e, n_used, w_gu, w_dn, l, tn_up=1024, tn_dn=2048):
    R = xs.shape[0]
    n_blocks = R // MOE_TM
    nj = D_EXPERT // tn_up
    blk = lambda j, i, be, nu: jnp.minimum(i, nu[0] - 1)
    h = pl.pallas_call(
        _moe_up_kernel,
        grid_spec=pltpu.PrefetchScalarGridSpec(
            num_scalar_prefetch=2,
            grid=(nj, n_blocks),
            in_specs=[pl.BlockSpec((MOE_TM, D_MODEL), lambda j, i, be, nu: (blk(j, i, be, nu), 0)),
                      pl.BlockSpec((1, 1, D_MODEL, tn_up), lambda j, i, be, nu: (l, be[i], 0, j)),
                      pl.BlockSpec((1, 1, D_MODEL, tn_up), lambda j, i, be, nu: (l, be[i], 0, nj + j))],
            out_specs=pl.BlockSpec((MOE_TM, tn_up), lambda j, i, be, nu: (i, j)),
        ),
        out_shape=jax.ShapeDtypeStruct((R, D_EXPERT), bf16),
        compiler_params=_params(2),
        name="moe_up",
    )(blk_e, n_used, xs, w_gu, w_gu)
    nj2 = D_MODEL // tn_dn
    return pl.pallas_call(
        _moe_down_kernel,
        grid_spec=pltpu.PrefetchScalarGridSpec(
            num_scalar_prefetch=2,
            grid=(nj2, n_blocks),
            in_specs=[pl.BlockSpec((MOE_TM, D_EXPERT), lambda j, i, be, nu: (blk(j, i, be, nu), 0)),
                      pl.BlockSpec((1, 1, D_EXPERT, tn_dn), lambda j, i, be, nu: (l, be[i], 0, j))],
            out_specs=pl.BlockSpec((MOE_TM, tn_dn), lambda j, i, be, nu: (i, j)),
        ),
        out_shape=jax.ShapeDtypeStruct((R, D_MODEL), f32),
        compiler_params=_params(2),
        name="moe_down",
    )(blk_e, n_used, h, w_dn)


def moe_dispatch(e_idx):
    T = e_idx.shape[1]
    A = T * TOP_K
    flat_e = e_idx.T.reshape(A)
    onehot = (flat_e[:, None] == jnp.arange(N_EXPERTS, dtype=i32)[None, :]).astype(i32)
    csum = jnp.cumsum(onehot, axis=0)
    rank = jnp.take_along_axis(csum, flat_e[:, None], axis=1)[:, 0] - 1
    counts = csum[-1]
    padded = (counts + MOE_TM - 1) // MOE_TM * MOE_TM
    pad_ends = jnp.cumsum(padded)
    pad_starts = pad_ends - padded
    dest = pad_starts[flat_e] + rank
    n_blocks = -(-A // MOE_TM) + N_EXPERTS
    tok = jnp.arange(A, dtype=i32) // TOP_K
    row_tok = (jnp.arange(n_blocks * MOE_TM, dtype=i32) % T).at[dest].set(tok)
    n_used = (pad_ends[-1] // MOE_TM).astype(i32)
    blk_start = jnp.minimum(jnp.arange(n_blocks, dtype=i32), n_used - 1) * MOE_TM
    blk_e = jnp.minimum(jnp.searchsorted(pad_ends, blk_start, side='right'), N_EXPERTS - 1).astype(i32)
    return dest.reshape(T, TOP_K), row_tok, blk_e, n_used.reshape(1)


def _t5_bucket(rel):
    n = jnp.maximum(-rel, 0)
    max_exact = NUM_BUCKETS // 2
    nf = jnp.maximum(n, 1).astype(f32)
    large = max_exact + (jnp.log(nf / max_exact) / math.log(MAX_DISTANCE / max_exact)
                         * (NUM_BUCKETS - max_exact)).astype(i32)
    large = jnp.minimum(large, NUM_BUCKETS - 1)
    return jnp.where(n < max_exact, n, large)


def _bias_tile(table, rel, mask, shift=None):
    t = table.astype(f32)
    if shift is not None:
        t = t - shift[None, :]
    bucket = jnp.where(mask, _t5_bucket(rel), -1)[None]
    b = jnp.full((table.shape[1],) + rel.shape, NEG_INF, f32)
    for i in range(NUM_BUCKETS):
        b = jnp.where(bucket == i, t[i][:, None, None], b)
    return b


def _group_rows(b, comps):
    H, Tq, Tk = b.shape
    b = b.reshape(KV_HEADS, REP, 1, Tq, Tk)
    b = jnp.broadcast_to(b, (KV_HEADS, REP, comps, Tq, Tk))
    return b.reshape(KV_HEADS, REP * comps * Tq, Tk)


def _mem_attn_kernel(q_ref, k_ref, v_ref, o_ref, *, interleaved):
    scale = HEAD_DIM ** -0.5
    for h in range(MEM_HEADS):
        sl = slice(h * HEAD_DIM, (h + 1) * HEAD_DIM)
        if interleaved:
            k = k_ref[0, 0, pl.ds(h, MEM_LEN, stride=MEM_HEADS), :]
            v = v_ref[0, 0, pl.ds(h, MEM_LEN, stride=MEM_HEADS), :]
        else:
            k = k_ref[0, :, sl]
            v = v_ref[0, :, sl]
        s = _dot_nt(q_ref[:, sl].astype(bf16), k.astype(bf16)) * scale
        m = jnp.max(s, -1, keepdims=True)
        e = jnp.exp(s - m)
        p = e * (1.0 / jnp.sum(e, -1, keepdims=True))
        o_ref[:, sl] = _dot(p.astype(bf16), v.astype(bf16)).astype(o_ref.dtype)


def mem_attend(p, row_block0, n_seq, tq, tiles_per_seq, mem_k, mem_v, k_map, v_map, kv_block):
    rows = n_seq * tiles_per_seq * tq
    return pl.pallas_call(
        functools.partial(_mem_attn_kernel, interleaved=len(kv_block) == 4),
        grid=(n_seq, tiles_per_seq),
        in_specs=[pl.BlockSpec((tq, MEM_W), lambda n, i: (row_block0 + n * tiles_per_seq + i, QM_COL // MEM_HEADS)),
                  pl.BlockSpec(kv_block, k_map), pl.BlockSpec(kv_block, v_map)],
        out_specs=pl.BlockSpec((tq, MEM_W), lambda n, i: (n * tiles_per_seq + i, 0)),
        out_shape=jax.ShapeDtypeStruct((rows, MEM_W), f32 if tq < 16 else bf16),
        compiler_params=_params(2),
        name="mem_attn",
    )(p, mem_k, mem_v)


def _softmax_sink_pv(s, sink_col, v):
    m = jnp.maximum(jnp.max(s, -1, keepdims=True), sink_col)
    e = jnp.exp(s - m)
    den = jnp.sum(e, -1, keepdims=True) + jnp.exp(sink_col - m)
    p = e * (1.0 / den)
    return _dot(p.astype(bf16), v)


def _sink_col(sink_ref, g, tq):
    return jnp.concatenate([jnp.full((tq, 1), sink_ref[REP * g + r], f32) for r in range(REP)], axis=0)


def _swa_prompt_kernel(q_ref, kc_ref, kp_ref, vc_ref, vp_ref, bias_ref, sink_ref, o_ref):
    i = pl.program_id(1)
    scale = HEAD_DIM ** -0.5
    for g in range(KV_HEADS):
        sl = slice(g * HEAD_DIM, (g + 1) * HEAD_DIM)
        heads = [slice((REP * g + r) * HEAD_DIM, (REP * g + r + 1) * HEAD_DIM) for r in range(REP)]
        kk = jnp.concatenate([kp_ref[:, sl], kc_ref[:, sl]], axis=0).astype(bf16)
        vt = jnp.concatenate([vp_ref[:, sl], vc_ref[:, sl]], axis=0).T.astype(bf16)
        qt = jnp.concatenate([q_ref[:, h].T for h in heads], axis=1).astype(bf16)
        s = _dot(kk, qt) * scale + bias_ref[g]
        key = lax.broadcasted_iota(i32, s.shape, 0)
        s = jnp.where((key >= WINDOW) | (i > 0), s, NEG_INF)
        sink = jnp.concatenate([jnp.full((1, WINDOW), sink_ref[REP * g + r], f32) for r in range(REP)], axis=1)
        m = jnp.maximum(jnp.max(s, axis=0, keepdims=True), sink)
        e = jnp.exp(s - m)
        den = jnp.sum(e, axis=0, keepdims=True) + jnp.exp(sink - m)
        o = _dot(vt, (e * (1.0 / den)).astype(bf16))
        for r, h in enumerate(heads):
            o_ref[:, h] = o[:, r * WINDOW:(r + 1) * WINDOW].T.astype(o_ref.dtype)


def swa_prompt(p, B, S, bias, sink):
    nb = S // WINDOW
    cur = lambda c: (lambda b, i: (b * nb + i, c))
    prev = lambda c: (lambda b, i: (b * nb + jnp.maximum(i - 1, 0), c))
    kc, vc = K_COL // KV_HEADS, V_COL // KV_HEADS
    return pl.pallas_call(
        _swa_prompt_kernel,
        grid=(B, nb),
        in_specs=[pl.BlockSpec((WINDOW, Q_W), cur(0)),
                  pl.BlockSpec((WINDOW, KV_W), cur(kc)), pl.BlockSpec((WINDOW, KV_W), prev(kc)),
                  pl.BlockSpec((WINDOW, KV_W), cur(vc)), pl.BlockSpec((WINDOW, KV_W), prev(vc)),
                  pl.BlockSpec((KV_HEADS, 2 * WINDOW, REP * WINDOW), lambda b, i: (0, 0, 0)),
                  pl.BlockSpec(memory_space=pltpu.SMEM)],
        out_specs=pl.BlockSpec((WINDOW, Q_W), lambda b, i: (b * nb + i, 0)),
        out_shape=jax.ShapeDtypeStruct((B * S, Q_W), bf16),
        compiler_params=_params(2),
        name="swa_prompt",
    )(p, p, p, p, p, bias, sink)


def _swa_sample_kernel(q_ref, kn_ref, vn_ref, wk_ref, wv_ref, bias_ref, sink_ref, o_ref, nk_ref, nv_ref, *, T):
    scale = HEAD_DIM ** -0.5
    wk = wk_ref[...].reshape(WINDOW, KV_W)
    wv = wv_ref[...].reshape(WINDOW, KV_W)
    kn = kn_ref[...]
    vn = vn_ref[...]
    pad = jnp.zeros((WINDOW - T, HEAD_DIM), f32)
    for g in range(KV_HEADS):
        sl = slice(g * HEAD_DIM, (g + 1) * HEAD_DIM)
        kk = jnp.concatenate([wk[:, sl], kn[:, sl], pad], axis=0).astype(bf16)
        vv = jnp.concatenate([wv[:, sl], vn[:, sl], pad], axis=0).astype(bf16)
        q3 = jnp.concatenate([q_ref[:, (REP * g + r) * HEAD_DIM:(REP * g + r + 1) * HEAD_DIM]
                              for r in range(REP)], axis=0).astype(bf16)
        s = _dot_nt(q3, kk) * scale + bias_ref[g]
        o = _softmax_sink_pv(s, _sink_col(sink_ref, g, T), vv)
        for r in range(REP):
            o_ref[:, (REP * g + r) * HEAD_DIM:(REP * g + r + 1) * HEAD_DIM] = o[r * T:(r + 1) * T]
    nk_ref[0, 0:WINDOW - T, :] = wk[T:, :]
    nk_ref[0, WINDOW - T:, :] = kn
    nv_ref[0, 0:WINDOW - T, :] = wv[T:, :]
    nv_ref[0, WINDOW - T:, :] = vn


def swa_sample(p, row_block0, DB, T, win_k, win_v, li, bias, sink):
    kc, vc = K_COL // KV_HEADS, V_COL // KV_HEADS
    win = lambda n: (li, n, 0, 0)
    return pl.pallas_call(
        functools.partial(_swa_sample_kernel, T=T),
        grid=(DB,),
        in_specs=[pl.BlockSpec((T, Q_W), lambda n: (row_block0 + n, 0)),
                  pl.BlockSpec((T, KV_W), lambda n: (row_block0 + n, kc)),
                  pl.BlockSpec((T, KV_W), lambda n: (row_block0 + n, vc)),
                  pl.BlockSpec((1, 1, WINDOW, KV_W), win), pl.BlockSpec((1, 1, WINDOW, KV_W), win),
                  pl.BlockSpec((KV_HEADS, REP * T, 2 * WINDOW), lambda n: (0, 0, 0)),
                  pl.BlockSpec(memory_space=pltpu.SMEM)],
        out_specs=[pl.BlockSpec((T, Q_W), lambda n: (n, 0)),
                   pl.BlockSpec((1, WINDOW, KV_W), lambda n: (n, 0, 0)),
                   pl.BlockSpec((1, WINDOW, KV_W), lambda n: (n, 0, 0))],
        out_shape=[jax.ShapeDtypeStruct((DB * T, Q_W), f32),
                   jax.ShapeDtypeStruct((DB, WINDOW, KV_W), f32),
                   jax.ShapeDtypeStruct((DB, WINDOW, KV_W), f32)],
        compiler_params=_params(1),
        name="swa_sample",
    )(p, p, p, win_k, win_v, bias, sink)


def _split_components(q, scale):
    lane = lax.broadcasted_iota(i32, q.shape, 1)
    qs = q * scale
    return [jnp.where(lane < DIFF_DIM, qs, 0.0), jnp.where(lane >= DIFF_DIM, qs, 0.0)]


def _diff_finish(o0, o1, lam, gain, lam_init, axis):
    o = o0 - lam * o1
    return o * lax.rsqrt(jnp.mean(o * o, axis, keepdims=True) + RMS_EPS) * gain * (1.0 - lam_init)


def _diff_prompt_kernel(q_ref, k_ref, v_ref, bias_ref, lam_ref, gain_ref, o_ref,
                        kb_scr, vt_scr, qpt_scr, m_scr, l_scr, acc_scr, *, lam_init):
    i = pl.program_id(2)
    tq, tk = DIFF_TQ, DIFF_TK
    n_chunks = k_ref.shape[0] // tk

    @pl.when(i == 0)
    def _():
        for c in range(n_chunks):
            kb_scr[c] = k_ref[c * tk:(c + 1) * tk, :].astype(bf16)
            vt_scr[c] = v_ref[c * tk:(c + 1) * tk, :].T.astype(bf16)

    pieces = []
    for r in range(REP):
        for part in _split_components(q_ref[:, r * HEAD_DIM:(r + 1) * HEAD_DIM], DIFF_DIM ** -0.5 * LOG2_E):
            pieces.append(part.T)
    qpt_scr[...] = jnp.concatenate(pieces, axis=1).astype(bf16)
    m_scr[...] = jnp.full(m_scr.shape, NEG_INF, f32)
    l_scr[...] = jnp.zeros(l_scr.shape, f32)
    acc_scr[...] = jnp.zeros(acc_scr.shape, f32)

    def chunk(c, n, bias):
        if n == 1:
            kb, vt = kb_scr[c], vt_scr[c]
        else:
            kb = kb_scr[pl.ds(c, n)].reshape(n * tk, HEAD_DIM)
            vt = jnp.concatenate([vt_scr[c + d] for d in range(n)], axis=1)
        s = _dot(kb, qpt_scr[...])
        if bias is not None:
            s = s + bias
        m_old = m_scr[...]
        m_new = jnp.maximum(m_old, jnp.max(s, axis=0, keepdims=True))
        p = jnp.exp2(s - m_new)
        alpha = jnp.exp2(m_old - m_new)
        l_scr[...] = alpha * l_scr[...] + jnp.sum(p, axis=0, keepdims=True)
        acc_scr[...] = alpha * acc_scr[...] + _dot(vt, p.astype(bf16))
        m_scr[...] = m_new

    step = tq // tk
    for d in range(step):
        chunk(i * step + d, 1, bias_ref[0, tq + d * tk:tq + (d + 1) * tk, :])

    @pl.when(i > 0)
    def _():
        for d in range(step):
            chunk((i - 1) * step + d, 1, bias_ref[0, d * tk:(d + 1) * tk, :])

    n_far = jnp.maximum(i - 1, 0) * step
    n_group = n_far // DIFF_FAR_CHUNKS

    def far_group(j, carry):
        chunk(j * DIFF_FAR_CHUNKS, DIFF_FAR_CHUNKS, None)
        return carry

    def far_single(c, carry):
        chunk(c, 1, None)
        return carry

    lax.fori_loop(0, n_group, far_group, 0)
    lax.fori_loop(n_group * DIFF_FAR_CHUNKS, n_far, far_single, 0)

    acc = acc_scr[...] * (1.0 / l_scr[...])
    lam = lam_ref[0]
    for r in range(REP):
        o = _diff_finish(acc[:, (2 * r) * tq:(2 * r + 1) * tq], acc[:, (2 * r + 1) * tq:(2 * r + 2) * tq],
                         lam, gain_ref[...], lam_init, 0)
        o_ref[:, r * HEAD_DIM:(r + 1) * HEAD_DIM] = o.T.astype(o_ref.dtype)


def diff_prompt(p, B, S, bias_t, lam, gain_col, lam_init):
    nq = S // DIFF_TQ
    cols = 2 * REP * DIFF_TQ
    n_chunks = S // DIFF_TK
    return pl.pallas_call(
        functools.partial(_diff_prompt_kernel, lam_init=lam_init),
        grid=(B, KV_HEADS, nq),
        in_specs=[pl.BlockSpec((DIFF_TQ, REP * HEAD_DIM), lambda b, g, i: (b * nq + i, g)),
                  pl.BlockSpec((S, HEAD_DIM), lambda b, g, i: (b, K_COL + g)),
                  pl.BlockSpec((S, HEAD_DIM), lambda b, g, i: (b, V_COL + g)),
                  pl.BlockSpec((1, 2 * DIFF_TQ, cols), lambda b, g, i: (g, 0, 0)),
                  pl.BlockSpec(memory_space=pltpu.SMEM),
                  pl.BlockSpec((HEAD_DIM, 1), lambda b, g, i: (0, 0))],
        out_specs=pl.BlockSpec((DIFF_TQ, REP * HEAD_DIM), lambda b, g, i: (b * nq + i, g)),
        out_shape=jax.ShapeDtypeStruct((B * S, Q_W), bf16),
        scratch_shapes=[pltpu.VMEM((n_chunks, DIFF_TK, HEAD_DIM), bf16),
                        pltpu.VMEM((n_chunks, HEAD_DIM, DIFF_TK), bf16),
                        pltpu.VMEM((HEAD_DIM, cols), bf16),
                        pltpu.VMEM((1, cols), f32),
                        pltpu.VMEM((1, cols), f32),
                        pltpu.VMEM((HEAD_DIM, cols), f32)],
        compiler_params=_params(3),
        name="diff_prompt",
    )(p, p, p, bias_t, lam, gain_col)


def _diff_sample_kernel(pt_ref, q_ref, kn_ref, vn_ref, *rest, T, lam_init, n_steps):
    npg = PAGES_PER_STEP
    k_refs, v_refs = rest[:npg], rest[npg:2 * npg]
    bias_last_ref, bias_new_ref, lam_ref, gain_ref, o_ref, qp_scr, m_scr, l_scr, acc_scr = rest[2 * npg:]
    s_id = pl.program_id(1)
    rows = 2 * REP * T

    @pl.when(s_id == 0)
    def _():
        pieces = []
        for h in range(N_HEADS):
            pieces += _split_components(q_ref[:, h * HEAD_DIM:(h + 1) * HEAD_DIM], DIFF_DIM ** -0.5)
        qp_scr[...] = jnp.concatenate(pieces, axis=0)
        m_scr[...] = jnp.full(m_scr.shape, NEG_INF, f32)
        l_scr[...] = jnp.zeros(l_scr.shape, f32)
        acc_scr[...] = jnp.zeros(acc_scr.shape, f32)

    def update(s, vs):
        m_old = m_scr[...]
        m_new = jnp.maximum(m_old, jnp.max(s, -1, keepdims=True))
        p = jnp.exp(s - m_new)
        alpha = jnp.exp(m_old - m_new)
        l_scr[...] = alpha * l_scr[...] + jnp.sum(p, -1, keepdims=True)
        pb = p.astype(bf16)
        pv = jnp.concatenate([_dot(pb[g * rows:(g + 1) * rows], vs[g]) for g in range(KV_HEADS)], axis=0)
        acc_scr[...] = alpha * acc_scr[...] + pv
        m_scr[...] = m_new

    def scores(ks):
        qp = qp_scr[...].astype(bf16)
        return jnp.concatenate([_dot_nt(qp[g * rows:(g + 1) * rows], ks[g]) for g in range(KV_HEADS)], axis=0)

    def head_rows(refs, g):
        return jnp.concatenate([r[0, 0, pl.ds(g, PAGE_SIZE, stride=KV_HEADS), :] for r in refs],
                               axis=0).astype(bf16)

    last = s_id == n_steps - 1
    s = scores([head_rows(k_refs, g) for g in range(KV_HEADS)])
    tail = s[:, (npg - 1) * PAGE_SIZE:] + jnp.where(last, bias_last_ref[...], 0.0)
    s = jnp.concatenate([s[:, :(npg - 1) * PAGE_SIZE], tail], axis=1)
    update(s, [head_rows(v_refs, g) for g in range(KV_HEADS)])

    @pl.when(last)
    def _():
        pad = jnp.zeros((PAGE_SIZE - T, HEAD_DIM), f32)
        kn = [jnp.concatenate([kn_ref[:, g * HEAD_DIM:(g + 1) * HEAD_DIM], pad], axis=0).astype(bf16)
              for g in range(KV_HEADS)]
        vn = [jnp.concatenate([vn_ref[:, g * HEAD_DIM:(g + 1) * HEAD_DIM], pad], axis=0).astype(bf16)
              for g in range(KV_HEADS)]
        update(scores(kn) + bias_new_ref[...], vn)
        acc = acc_scr[...] * (1.0 / l_scr[...])
        lam = lam_ref[0]
        for h in range(N_HEADS):
            o = _diff_finish(acc[(2 * h) * T:(2 * h + 1) * T], acc[(2 * h + 1) * T:(2 * h + 2) * T],
                             lam, gain_ref[...], lam_init, -1)
            o_ref[:, h * HEAD_DIM:(h + 1) * HEAD_DIM] = o


def diff_sample(p, row_block0, DB, T, cache_k, cache_v, li, page_table, bias_last, bias_new, lam, gain, lam_init):
    n_pages = PAST_LEN // PAGE_SIZE
    npg = PAGES_PER_STEP
    n_steps = n_pages // npg
    rows = 2 * N_HEADS * T
    kc, vc = K_COL // KV_HEADS, V_COL // KV_HEADS

    def page(j):
        return lambda n, s, pt: (li, pt[n * n_pages + s * npg + j], 0, 0)

    page_block = (1, 1, PAGE_SIZE * KV_HEADS, HEAD_DIM)
    const2 = lambda n, s, pt: (0, 0)
    in_specs = ([pl.BlockSpec((T, Q_W), lambda n, s, pt: (row_block0 + n, 0)),
                 pl.BlockSpec((T, KV_W), lambda n, s, pt: (row_block0 + n, kc)),
                 pl.BlockSpec((T, KV_W), lambda n, s, pt: (row_block0 + n, vc))]
                + [pl.BlockSpec(page_block, page(j)) for j in range(npg)]
                + [pl.BlockSpec(page_block, page(j)) for j in range(npg)]
                + [pl.BlockSpec((rows, PAGE_SIZE), const2),
                   pl.BlockSpec((rows, PAGE_SIZE), const2),
                   pl.BlockSpec(memory_space=pltpu.SMEM),
                   pl.BlockSpec((1, HEAD_DIM), const2)])
    return pl.pallas_call(
        functools.partial(_diff_sample_kernel, T=T, lam_init=lam_init, n_steps=n_steps),
        grid_spec=pltpu.PrefetchScalarGridSpec(
            num_scalar_prefetch=1,
            grid=(DB, n_steps),
            in_specs=in_specs,
            out_specs=pl.BlockSpec((T, Q_W), lambda n, s, pt: (n, 0)),
            scratch_shapes=[pltpu.VMEM((rows, HEAD_DIM), f32),
                            pltpu.VMEM((rows, 1), f32),
                            pltpu.VMEM((rows, 1), f32),
                            pltpu.VMEM((rows, HEAD_DIM), f32)],
        ),
        out_shape=jax.ShapeDtypeStruct((DB * T, Q_W), f32),
        compiler_params=_params(2),
        name="diff_sample",
    )(page_table, p, p, p, *([cache_k] * npg), *([cache_v] * npg), bias_last, bias_new, lam, gain)


def _row_tile(total, candidates):
    for t in candidates:
        if total % t == 0:
            return t
    raise ValueError(f"no row tile for {total} rows")


def kernel(x_prompt, x_sample, cache_win_k, cache_win_v, cache_diff_k, cache_diff_v, cache_mem_k, cache_mem_v,
           page_table, mem_prompt, w_in, w_mem_kv, w_o, rel_bias, sinks, lam_q1, lam_k1, lam_q2, lam_k2, subln_g,
           ln1_g, ln1_b, ln2_g, ln2_b, w_router, router_bias, w_gate_up, w_down):
    B, S, _ = x_prompt.shape
    DB, T, _ = x_sample.shape
    TP, TS = B * S, DB * T
    TT = TP + TS
    assert S % DIFF_TQ == 0 and S % WINDOW == 0 and TP % T == 0 and T % 8 == 0
    n_swa, n_pool = cache_win_k.shape[0], cache_diff_k.shape[1]
    sample_block0 = TP // T

    x = jnp.concatenate([x_prompt.reshape(TP, D_MODEL), x_sample.reshape(TS, D_MODEL)], axis=0)
    xb = x.astype(bf16)
    w_o_b = w_o.astype(bf16)
    w_mem_b = w_mem_kv.astype(bf16)
    mem_b = mem_prompt.reshape(B * MEM_LEN, D_MODEL).astype(bf16)
    w_router_t = w_router.T
    router_bias_b = jnp.broadcast_to(router_bias.astype(f32)[:, None], (N_EXPERTS, 128))
    win_k = cache_win_k.reshape(n_swa, DB, WINDOW, KV_W)
    win_v = cache_win_v.reshape(n_swa, DB, WINDOW, KV_W)
    pool_k = cache_diff_k.reshape(-1, n_pool, PAGE_SIZE * KV_HEADS, HEAD_DIM)
    pool_v = cache_diff_v.reshape(-1, n_pool, PAGE_SIZE * KV_HEADS, HEAD_DIM)
    cmem_k = cache_mem_k.reshape(DEPTH, DB, MEM_LEN * MEM_HEADS, HEAD_DIM)
    cmem_v = cache_mem_v.reshape(DEPTH, DB, MEM_LEN * MEM_HEADS, HEAD_DIM)
    pt_flat = page_table.reshape(-1).astype(i32)

    far = rel_bias[NUM_BUCKETS - 1].astype(f32)
    qo = jnp.arange(WINDOW)
    rel = (jnp.arange(2 * WINDOW) - WINDOW)[None, :] - qo[:, None]
    bias_swa_p = jnp.swapaxes(_group_rows(_bias_tile(rel_bias, rel, (rel <= 0) & (rel > -WINDOW)), 1), 1, 2)
    tt = jnp.arange(T)
    kpos = jnp.concatenate([jnp.arange(WINDOW) - WINDOW, tt, jnp.full((WINDOW - T,), T)])
    rel = kpos[None, :] - tt[:, None]
    bias_swa_s = _group_rows(_bias_tile(rel_bias, rel, (rel <= 0) & (rel > -WINDOW)), 1)
    qo = jnp.arange(DIFF_TQ)
    rel = (jnp.arange(2 * DIFF_TQ) - DIFF_TQ)[None, :] - qo[:, None]
    bias_diff_p = jnp.swapaxes(_group_rows(_bias_tile(rel_bias, rel, rel <= 0, far), 2), 1, 2) * LOG2_E
    rel = (jnp.arange(PAGE_SIZE) - PAGE_SIZE)[None, :] - tt[:, None]
    bias_diff_last = _group_rows(_bias_tile(rel_bias, rel, rel <= 0, far), 2).reshape(-1, PAGE_SIZE)
    kpos = jnp.concatenate([tt, jnp.full((PAGE_SIZE - T,), T)])
    rel = kpos[None, :] - tt[:, None]
    bias_diff_new = _group_rows(_bias_tile(rel_bias, rel, rel <= 0, far), 2).reshape(-1, PAGE_SIZE)

    tm_mm = _row_tile(TT, (1056, 1024, 512, 256))
    tm_ln = _row_tile(TT, (352, 256, 128))
    tm_wo = _row_tile(math.gcd(TP, TS), (256, 128))
    tm_rt = _row_tile(TT, (1408, 1024, 512, 256, 128))
    tq_mem = _row_tile(S, (512, 256, 128))

    win_k_p, win_v_p, win_k_s, win_v_s = [], [], [], []
    diff_k_p, diff_v_p, diff_k_s, diff_v_s = [], [], [], []
    mem_k_p, mem_v_p = [], []
    for l in range(DEPTH):
        i = l // 2
        p, k_rows, v_rows = in_proj(xb, w_in, l, tm_mm)
        k_p = k_rows[:TP * KV_HEADS].reshape(B, S, KV_HEADS, HEAD_DIM)
        v_p = v_rows[:TP * KV_HEADS].reshape(B, S, KV_HEADS, HEAD_DIM)
        if l % 2 == 0:
            sink = sinks[i].astype(f32)
            self_p = swa_prompt(p, B, S, bias_swa_p, sink)
            self_s, nk, nv = swa_sample(p, sample_block0, DB, T, win_k, win_v, i, bias_swa_s, sink)
            win_k_p.append(k_p[:, S - WINDOW:])
            win_v_p.append(v_p[:, S - WINDOW:])
            win_k_s.append(nk.reshape(DB, WINDOW, KV_HEADS, HEAD_DIM))
            win_v_s.append(nv.reshape(DB, WINDOW, KV_HEADS, HEAD_DIM))
        else:
            lam_init = 0.8 - 0.6 * math.exp(-0.3 * l)
            lam = (jnp.exp(jnp.sum(lam_q1[i].astype(f32) * lam_k1[i].astype(f32)))
                   - jnp.exp(jnp.sum(lam_q2[i].astype(f32) * lam_k2[i].astype(f32))) + lam_init).reshape(1)
            gain = subln_g[i].astype(f32).reshape(1, HEAD_DIM)
            self_p = diff_prompt(p, B, S, bias_diff_p, lam, gain.reshape(HEAD_DIM, 1), lam_init)
            self_s = diff_sample(p, sample_block0, DB, T, pool_k, pool_v, i, pt_flat,
                                 bias_diff_last, bias_diff_new, lam, gain, lam_init)
            diff_k_p.append(k_p)
            diff_v_p.append(v_p)
            diff_k_s.append(k_rows[TP * KV_HEADS:].reshape(DB, T, KV_HEADS, HEAD_DIM))
            diff_v_s.append(v_rows[TP * KV_HEADS:].reshape(DB, T, KV_HEADS, HEAD_DIM))
        mkv = matmul(mem_b, w_mem_b[l], B * MEM_LEN, 512).reshape(B, MEM_LEN, 2 * MEM_W)
        mem_k_p.append(mkv[:, :, :MEM_W].reshape(B, MEM_LEN, MEM_HEADS, HEAD_DIM))
        mem_v_p.append(mkv[:, :, MEM_W:].reshape(B, MEM_LEN, MEM_HEADS, HEAD_DIM))
        cross_p = mem_attend(p, 0, B, tq_mem, S // tq_mem, mkv, mkv,
                             lambda n, t: (n, 0, 0), lambda n, t: (n, 0, 1), (1, MEM_LEN, MEM_W))
        cross_s = mem_attend(p, sample_block0, DB, T, 1, cmem_k, cmem_v,
                             lambda n, t: (l, n, 0, 0), lambda n, t: (l, n, 0, 0),
                             (1, 1, MEM_LEN * MEM_HEADS, HEAD_DIM))
        x = wo_ln(self_p, cross_p, self_s, cross_s, x, w_o_b[l, :Q_W], w_o_b[l, Q_W:],
                  ln1_g[l].reshape(1, D_MODEL), ln1_b[l].reshape(1, D_MODEL), tm_wo)
        e_idx, gate = router(x, w_router_t, router_bias_b, tm_rt)
        dest, row_tok, blk_e, n_used = moe_dispatch(e_idx)
        xs = gather_rows(x, row_tok, n_used, MOE_TM, bf16)
        yb = moe_ffn_sorted(xs, blk_e, n_used, w_gate_up, w_down, l)
        x, xb = combine_ln(x, yb, dest, gate.T,
                           ln2_g[l].reshape(1, D_MODEL), ln2_b[l].reshape(1, D_MODEL), tm_ln)
    return (x[:TP].reshape(B, S, D_MODEL), x[TP:].reshape(DB, T, D_MODEL),
            jnp.stack(win_k_p), jnp.stack(win_v_p), jnp.stack(win_k_s), jnp.stack(win_v_s),
            jnp.stack(diff_k_p), jnp.stack(diff_v_p), jnp.stack(diff_k_s), jnp.stack(diff_v_s),
            jnp.stack(mem_k_p), jnp.stack(mem_v_p))
```

```python
import functools
import math

import jax
import jax.numpy as jnp
from jax import lax
from jax.experimental import pallas as pl
from jax.experimental.pallas import tpu as pltpu

f32 = jnp.float32
bf16 = jnp.bfloat16
i32 = jnp.int32

D_MODEL = 2048
DEPTH = 4
PAST_LEN = 16384
PAGE_SIZE = 128
HEAD_DIM = 128
N_HEADS = 12
KV_HEADS = 4
REP = N_HEADS // KV_HEADS
DIFF_DIM = HEAD_DIM // 2
MEM_HEADS = 4
MEM_LEN = 256
WINDOW = 128
NUM_BUCKETS = 32
MAX_DISTANCE = 128
N_EXPERTS = 16
N_GROUPS = 4
EXPERTS_PER_GROUP = N_EXPERTS // N_GROUPS
TOP_K = 2
D_EXPERT = D_MODEL // 2
Q_W = N_HEADS * HEAD_DIM
KV_W = KV_HEADS * HEAD_DIM
MEM_W = MEM_HEADS * HEAD_DIM
IN_W = Q_W + 2 * KV_W + MEM_W
ALPHA = (2.0 * DEPTH) ** 0.25
LN_EPS = 1e-5
RMS_EPS = 1e-5
NEG_INF = -1e30
LOG2_E = math.log2(math.e)

VMEM_LIMIT_BYTES = 56 * 1024 * 1024

K_COL = Q_W // HEAD_DIM
V_COL = (Q_W + KV_W) // HEAD_DIM
QM_COL = (Q_W + 2 * KV_W) // HEAD_DIM

DIFF_TQ = 256
DIFF_TK = 256
DIFF_FAR_CHUNKS = 2
PAGES_PER_STEP = 16
MOE_TM = 512


def _params(n_axes):
    return pltpu.CompilerParams(dimension_semantics=("arbitrary",) * n_axes,
                                vmem_limit_bytes=VMEM_LIMIT_BYTES)


def _dot_nt(a, b):
    return lax.dot_general(a, b, (((1,), (1,)), ((), ())), preferred_element_type=f32)


def _dot(a, b):
    return jnp.dot(a, b, preferred_element_type=f32)


def _mm_kernel(a_ref, b_ref, o_ref):
    o_ref[...] = _dot(a_ref[...], b_ref[...])


def matmul(a, b, tm, tn):
    M, K = a.shape
    N = b.shape[1]
    return pl.pallas_call(
        _mm_kernel,
        grid=(M // tm, N // tn),
        in_specs=[pl.BlockSpec((tm, K), lambda i, j: (i, 0)),
                  pl.BlockSpec((K, tn), lambda i, j: (0, j))],
        out_specs=pl.BlockSpec((tm, tn), lambda i, j: (i, j)),
        out_shape=jax.ShapeDtypeStruct((M, N), f32),
        compiler_params=_params(2),
        name="matmul",
    )(a, b)


def _in_proj_kernel(a_ref, b_ref, o_ref, k_ref, v_ref):
    acc = _dot(a_ref[...], b_ref[0].astype(bf16))
    o_ref[...] = acc
    tm = acc.shape[0]
    j = pl.program_id(1)
    for tile, ref in ((Q_W // KV_W, k_ref), ((Q_W + KV_W) // KV_W, v_ref)):
        @pl.when(j == tile)
        def _():
            for h in range(KV_HEADS):
                ref[pl.ds(h, tm, stride=KV_HEADS), :] = acc[:, h * HEAD_DIM:(h + 1) * HEAD_DIM]


def in_proj(a, w, l, tm):
    M, K = a.shape
    kv_shape = jax.ShapeDtypeStruct((M * KV_HEADS, HEAD_DIM), f32)
    kv_spec = pl.BlockSpec((tm * KV_HEADS, HEAD_DIM), lambda i, j: (i, 0))
    return pl.pallas_call(
        _in_proj_kernel,
        grid=(M // tm, IN_W // KV_W),
        in_specs=[pl.BlockSpec((tm, K), lambda i, j: (i, 0)),
                  pl.BlockSpec((1, K, KV_W), lambda i, j: (l, 0, j))],
        out_specs=[pl.BlockSpec((tm, KV_W), lambda i, j: (i, j)), kv_spec, kv_spec],
        out_shape=[jax.ShapeDtypeStruct((M, IN_W), f32), kv_shape, kv_shape],
        compiler_params=_params(2),
        name="in_proj",
    )(a, w)


def _layer_norm(y, g, b):
    mu = jnp.mean(y, -1, keepdims=True)
    yc = y - mu
    var = jnp.mean(yc * yc, -1, keepdims=True)
    return yc * lax.rsqrt(var + LN_EPS) * g + b


def _wo_ln_kernel(msp_ref, mcp_ref, mss_ref, mcs_ref, x_ref, ws_ref, wc_ref, g_ref, b_ref, o_ref,
                  *, n_prompt_tiles):
    def body(ms, mc):
        f = _dot(ms, ws_ref[...]) + _dot(mc, wc_ref[...])
        o_ref[...] = _layer_norm(ALPHA * x_ref[...] + f, g_ref[...], b_ref[...])

    is_prompt = pl.program_id(0) < n_prompt_tiles

    @pl.when(is_prompt)
    def _():
        body(msp_ref[...], mcp_ref[...])

    @pl.when(jnp.logical_not(is_prompt))
    def _():
        body(mss_ref[...].astype(bf16), mcs_ref[...].astype(bf16))


def wo_ln(self_p, cross_p, self_s, cross_s, x, w_self, w_cross, g, b, tm):
    M = x.shape[0]
    n_p = self_p.shape[0] // tm
    assert self_p.shape[0] % tm == 0 and self_s.shape[0] % tm == 0
    row = lambda i: (i, 0)
    prow = lambda i: (jnp.minimum(i, n_p - 1), 0)
    srow = lambda i: (jnp.maximum(i - n_p, 0), 0)
    const = lambda i: (0, 0)
    return pl.pallas_call(
        functools.partial(_wo_ln_kernel, n_prompt_tiles=n_p),
        grid=(M // tm,),
        in_specs=[pl.BlockSpec((tm, Q_W), prow), pl.BlockSpec((tm, MEM_W), prow),
                  pl.BlockSpec((tm, Q_W), srow), pl.BlockSpec((tm, MEM_W), srow),
                  pl.BlockSpec((tm, D_MODEL), row),
                  pl.BlockSpec((Q_W, D_MODEL), const), pl.BlockSpec((MEM_W, D_MODEL), const),
                  pl.BlockSpec((1, D_MODEL), const), pl.BlockSpec((1, D_MODEL), const)],
        out_specs=pl.BlockSpec((tm, D_MODEL), row),
        out_shape=jax.ShapeDtypeStruct((M, D_MODEL), f32),
        compiler_params=_params(1),
        name="wo_ln",
    )(self_p, cross_p, self_s, cross_s, x, w_self, w_cross, g, b)


def _pipelined_row_gather(i, n_used, idx_first_ref, idx_next_ref, src_ref, buf, sem, rows):
    def row_copy(slot, r, src_row):
        return pltpu.make_async_copy(src_ref.at[pl.ds(src_row, 1)], buf.at[slot, pl.ds(r, 1)], sem.at[slot])

    def start_block(slot, idx_ref):
        def body(h, carry):
            for q in range(2):
                r = 2 * h + q
                row_copy(slot, r, idx_ref[0, 0, r]).start(priority=q)
            return carry
        lax.fori_loop(0, rows // 2, body, 0, unroll=4)

    @pl.when(i == 0)
    def _():
        start_block(0, idx_first_ref)

    @pl.when(i + 1 < n_used)
    def _():
        start_block((i + 1) % 2, idx_next_ref)

    @pl.when(i < n_used)
    def _():
        def body(r, carry):
            row_copy(i % 2, r, 0).wait()
            return carry
        lax.fori_loop(0, rows, body, 0, unroll=8)


def _gather_idx_specs(rows, n_blocks):
    first = lambda i, *_: (0, 0, 0)
    nxt = lambda i, *_: (jnp.minimum(i + 1, n_blocks - 1), 0, 0)
    return [pl.BlockSpec((1, 1, rows), first, memory_space=pltpu.SMEM),
            pl.BlockSpec((1, 1, rows), nxt, memory_space=pltpu.SMEM)]


def _gather_rows_kernel(nu_ref, idx_first_ref, idx_next_ref, src_ref, o_ref, buf, sem, *, rows):
    i = pl.program_id(0)
    used = i < nu_ref[0]
    _pipelined_row_gather(i, nu_ref[0], idx_first_ref, idx_next_ref, src_ref, buf, sem, rows)

    @pl.when(used)
    def _():
        o_ref[...] = buf[i % 2].astype(o_ref.dtype)

    @pl.when(jnp.logical_not(used))
    def _():
        o_ref[...] = jnp.zeros(o_ref.shape, o_ref.dtype)


def gather_rows(src, idx, n_used, rows, out_dtype):
    n_blocks = idx.shape[0] // rows
    D = src.shape[1]
    idx3 = idx.reshape(n_blocks, 1, rows)
    return pl.pallas_call(
        functools.partial(_gather_rows_kernel, rows=rows),
        grid_spec=pltpu.PrefetchScalarGridSpec(
            num_scalar_prefetch=1,
            grid=(n_blocks,),
            in_specs=_gather_idx_specs(rows, n_blocks) + [pl.BlockSpec(memory_space=pl.ANY)],
            out_specs=pl.BlockSpec((rows, D), lambda i, nu: (i, 0)),
            scratch_shapes=[pltpu.VMEM((2, rows, D), src.dtype), pltpu.SemaphoreType.DMA((2,))],
        ),
        out_shape=jax.ShapeDtypeStruct((n_blocks * rows, D), out_dtype),
        compiler_params=_params(1),
        name="gather_rows",
    )(n_used, idx3, idx3, src)


def _combine_ln_kernel(idx_first_ref, idx_next_ref, y_ref, x_ref, gate_ref, g_ref, b_ref, o_ref, ob_ref,
                       buf, sem, *, tm):
    i = pl.program_id(0)
    _pipelined_row_gather(i, pl.num_programs(0), idx_first_ref, idx_next_ref, y_ref, buf, sem, TOP_K * tm)
    gate = gate_ref[...]
    y = buf[i % 2]
    f = y[:tm] * gate[:, 0:1] + y[tm:] * gate[:, 1:2]
    out = _layer_norm(ALPHA * x_ref[...] + f, g_ref[...], b_ref[...])
    o_ref[...] = out
    ob_ref[...] = out.astype(bf16)


def combine_ln(x, yb, dest, gate_t, g, b, tm):
    M = x.shape[0]
    n_tiles = M // tm
    idx3 = jnp.swapaxes(dest.reshape(n_tiles, tm, TOP_K), 1, 2).reshape(n_tiles, 1, TOP_K * tm)
    row = lambda i: (i, 0)
    const = lambda i: (0, 0)
    return pl.pallas_call(
        functools.partial(_combine_ln_kernel, tm=tm),
        grid=(n_tiles,),
        in_specs=_gather_idx_specs(TOP_K * tm, n_tiles) + [
            pl.BlockSpec(memory_space=pl.ANY),
            pl.BlockSpec((tm, D_MODEL), row), pl.BlockSpec((tm, TOP_K), row),
            pl.BlockSpec((1, D_MODEL), const), pl.BlockSpec((1, D_MODEL), const)],
        out_specs=[pl.BlockSpec((tm, D_MODEL), row), pl.BlockSpec((tm, D_MODEL), row)],
        out_shape=[jax.ShapeDtypeStruct((M, D_MODEL), f32),
                   jax.ShapeDtypeStruct((M, D_MODEL), bf16)],
        scratch_shapes=[pltpu.VMEM((2, TOP_K * tm, D_MODEL), f32), pltpu.SemaphoreType.DMA((2,))],
        compiler_params=_params(1),
        name="combine_ln",
    )(idx3, idx3, yb, x, gate_t, g, b)


def _router_kernel(x_ref, wh_ref, wl_ref, bias_ref, e_ref, gate_ref):
    x = x_ref[...]
    xh = x.astype(bf16)
    xl = (x - xh.astype(f32)).astype(bf16)
    wh, wl = wh_ref[...], wl_ref[...]
    logits = _dot_nt(wh, xh) + (_dot_nt(wh, xl) + _dot_nt(wl, xh))
    scores = jax.nn.sigmoid(logits)
    sel = scores + bias_ref[...][:, 0:1]
    n = EXPERTS_PER_GROUP
    rows = [sel[e:e + 1, :] for e in range(N_EXPERTS)]
    srows = [scores[e:e + 1, :] for e in range(N_EXPERTS)]
    gscore = []
    for g in range(N_GROUPS):
        v = rows[g * n:(g + 1) * n]
        best = None
        for a in range(n):
            for b in range(a + 1, n):
                pair = v[a] + v[b]
                best = pair if best is None else jnp.maximum(best, pair)
        gscore.append(best)
    g_idx = jnp.zeros_like(gscore[0], dtype=i32)
    best = gscore[0]
    for g in range(1, N_GROUPS):
        take = gscore[g] > best
        g_idx = jnp.where(take, g, g_idx)
        best = jnp.where(take, gscore[g], best)
    ing, sg = [], []
    for a in range(n):
        va, sa = rows[a], srows[a]
        for g in range(1, N_GROUPS):
            va = jnp.where(g_idx == g, rows[g * n + a], va)
            sa = jnp.where(g_idx == g, srows[g * n + a], sa)
        ing.append(va)
        sg.append(sa)
    l_idx = [jnp.zeros_like(g_idx), jnp.zeros_like(g_idx)]
    w = [jnp.zeros_like(best), jnp.zeros_like(best)]
    for a in range(n):
        rank = jnp.zeros_like(g_idx)
        for b in range(n):
            if b == a:
                continue
            ahead = (ing[b] > ing[a]) | ((ing[b] == ing[a]) & (b < a))
            rank = rank + ahead.astype(i32)
        for k in range(TOP_K):
            hit = rank == k
            l_idx[k] = jnp.where(hit, a, l_idx[k])
            w[k] = jnp.where(hit, sg[a], w[k])
    tot = w[0] + w[1]
    e_ref[...] = jnp.concatenate([g_idx * n + l_idx[0], g_idx * n + l_idx[1]], axis=0)
    gate_ref[...] = jnp.concatenate([w[0] / tot, w[1] / tot], axis=0)


def router(x, w_router_t, router_bias, tm):
    M = x.shape[0]
    w_hi = w_router_t.astype(bf16)
    w_lo = (w_router_t - w_hi.astype(f32)).astype(bf16)
    return pl.pallas_call(
        _router_kernel,
        grid=(M // tm,),
        in_specs=[pl.BlockSpec((tm, D_MODEL), lambda i: (i, 0)),
                  pl.BlockSpec((N_EXPERTS, D_MODEL), lambda i: (0, 0)),
                  pl.BlockSpec((N_EXPERTS, D_MODEL), lambda i: (0, 0)),
                  pl.BlockSpec((N_EXPERTS, 128), lambda i: (0, 0))],
        out_specs=[pl.BlockSpec((TOP_K, tm), lambda i: (0, i)),
                   pl.BlockSpec((TOP_K, tm), lambda i: (0, i))],
        out_shape=[jax.ShapeDtypeStruct((TOP_K, M), i32),
                   jax.ShapeDtypeStruct((TOP_K, M), f32)],
        compiler_params=_params(1),
        name="router",
    )(x, w_hi, w_lo, router_bias)


def _moe_up_kernel(be_ref, nu_ref, x_ref, wg_ref, wu_ref, h_ref):
    used = pl.program_id(1) < nu_ref[0]

    @pl.when(used)
    def _():
        x = x_ref[...]
        g = _dot(x, wg_ref[0, 0].astype(bf16))
        u = _dot(x, wu_ref[0, 0].astype(bf16))
        h_ref[...] = (g * jax.nn.sigmoid(g) * u).astype(bf16)

    @pl.when(jnp.logical_not(used))
    def _():
        h_ref[...] = jnp.zeros(h_ref.shape, h_ref.dtype)


def _moe_down_kernel(be_ref, nu_ref, h_ref, wd_ref, y_ref):
    used = pl.program_id(1) < nu_ref[0]

    @pl.when(used)
    def _():
        y_ref[...] = _dot(h_ref[...], wd_ref[0, 0].astype(bf16))

    @pl.when(jnp.logical_not(used))
    def _():
        y_ref[...] = jnp.zeros(y_ref.shape, y_ref.dtype)


def moe_ffn_sorted(xs, blk_e, n_used, w_gu, w_dn, l, tn_up=1024, tn_dn=2048):
    R = xs.shape[0]
    n_blocks = R // MOE_TM
    nj = D_EXPERT // tn_up
    blk = lambda j, i, be, nu: jnp.minimum(i, nu[0] - 1)
    h = pl.pallas_call(
        _moe_up_kernel,
        grid_spec=pltpu.PrefetchScalarGridSpec(
            num_scalar_prefetch=2,
            grid=(nj, n_blocks),
            in_specs=[pl.BlockSpec((MOE_TM, D_MODEL), lambda j, i, be, nu: (blk(j, i, be, nu), 0)),
                      pl.BlockSpec((1, 1, D_MODEL, tn_up), lambda j, i, be, nu: (l, be[i], 0, j)),
                      pl.BlockSpec((1, 1, D_MODEL, tn_up), lambda j, i, be, nu: (l, be[i], 0, nj + j))],
            out_specs=pl.BlockSpec((MOE_TM, tn_up), lambda j, i, be, nu: (i, j)),
        ),
        out_shape=jax.ShapeDtypeStruct((R, D_EXPERT), bf16),
        compiler_params=_params(2),
        name="moe_up",
    )(blk_e, n_used, xs, w_gu, w_gu)
    nj2 = D_MODEL // tn_dn
    return pl.pallas_call(
        _moe_down_kernel,
        grid_spec=pltpu.PrefetchScalarGridSpec(
            num_scalar_prefetch=2,
            grid=(nj2, n_blocks),
            in_specs=[pl.BlockSpec((MOE_TM, D_EXPERT), lambda j, i, be, nu: (blk(j, i, be, nu), 0)),
                      pl.BlockSpec((1, 1, D_EXPERT, tn_dn), lambda j, i, be, nu: (l, be[i], 0, j))],
            out_specs=pl.BlockSpec((MOE_TM, tn_dn), lambda j, i, be, nu: (i, j)),
        ),
        out_shape=jax.ShapeDtypeStruct((R, D_MODEL), f32),
        compiler_params=_params(2),
        name="moe_down",
    )(blk_e, n_used, h, w_dn)


def moe_dispatch(e_idx):
    T = e_idx.shape[1]
    A = T * TOP_K
    flat_e = e_idx.T.reshape(A)
    onehot = (flat_e[:, None] == jnp.arange(N_EXPERTS, dtype=i32)[None, :]).astype(i32)
    csum = jnp.cumsum(onehot, axis=0)
    rank = jnp.take_along_axis(csum, flat_e[:, None], axis=1)[:, 0] - 1
    counts = csum[-1]
    padded = (counts + MOE_TM - 1) // MOE_TM * MOE_TM
    pad_ends = jnp.cumsum(padded)
    pad_starts = pad_ends - padded
    dest = pad_starts[flat_e] + rank
    n_blocks = -(-A // MOE_TM) + N_EXPERTS
    tok = jnp.arange(A, dtype=i32) // TOP_K
    row_tok = (jnp.arange(n_blocks * MOE_TM, dtype=i32) % T).at[dest].set(tok)
    n_used = (pad_ends[-1] // MOE_TM).astype(i32)
    blk_start = jnp.minimum(jnp.arange(n_blocks, dtype=i32), n_used - 1) * MOE_TM
    blk_e = jnp.minimum(jnp.searchsorted(pad_ends, blk_start, side='right'), N_EXPERTS - 1).astype(i32)
    return dest.reshape(T, TOP_K), row_tok, blk_e, n_used.reshape(1)


def _t5_bucket(rel):
    n = jnp.maximum(-rel, 0)
    max_exact = NUM_BUCKETS // 2
    nf = jnp.maximum(n, 1).astype(f32)
    large = max_exact + (jnp.log(nf / max_exact) / math.log(MAX_DISTANCE / max_exact)
                         * (NUM_BUCKETS - max_exact)).astype(i32)
    large = jnp.minimum(large, NUM_BUCKETS - 1)
    return jnp.where(n < max_exact, n, large)


def _bias_tile(table, rel, mask, shift=None):
    t = table.astype(f32)
    if shift is not None:
        t = t - shift[None, :]
    bucket = jnp.where(mask, _t5_bucket(rel), -1)[None]
    b = jnp.full((table.shape[1],) + rel.shape, NEG_INF, f32)
    for i in range(NUM_BUCKETS):
        b = jnp.where(bucket == i, t[i][:, None, None], b)
    return b


def _group_rows(b, comps):
    H, Tq, Tk = b.shape
    b = b.reshape(KV_HEADS, REP, 1, Tq, Tk)
    b = jnp.broadcast_to(b, (KV_HEADS, REP, comps, Tq, Tk))
    return b.reshape(KV_HEADS, REP * comps * Tq, Tk)


def _mem_attn_kernel(q_ref, k_ref, v_ref, o_ref, *, interleaved):
    scale = HEAD_DIM ** -0.5
    for h in range(MEM_HEADS):
        sl = slice(h * HEAD_DIM, (h + 1) * HEAD_DIM)
        if interleaved:
            k = k_ref[0, 0, pl.ds(h, MEM_LEN, stride=MEM_HEADS), :]
            v = v_ref[0, 0, pl.ds(h, MEM_LEN, stride=MEM_HEADS), :]
        else:
            k = k_ref[0, :, sl]
            v = v_ref[0, :, sl]
        s = _dot_nt(q_ref[:, sl].astype(bf16), k.astype(bf16)) * scale
        m = jnp.max(s, -1, keepdims=True)
        e = jnp.exp(s - m)
        p = e * (1.0 / jnp.sum(e, -1, keepdims=True))
        o_ref[:, sl] = _dot(p.astype(bf16), v.astype(bf16)).astype(o_ref.dtype)


def mem_attend(p, row_block0, n_seq, tq, tiles_per_seq, mem_k, mem_v, k_map, v_map, kv_block):
    rows = n_seq * tiles_per_seq * tq
    return pl.pallas_call(
        functools.partial(_mem_attn_kernel, interleaved=len(kv_block) == 4),
        grid=(n_seq, tiles_per_seq),
        in_specs=[pl.BlockSpec((tq, MEM_W), lambda n, i: (row_block0 + n * tiles_per_seq + i, QM_COL // MEM_HEADS)),
                  pl.BlockSpec(kv_block, k_map), pl.BlockSpec(kv_block, v_map)],
        out_specs=pl.BlockSpec((tq, MEM_W), lambda n, i: (n * tiles_per_seq + i, 0)),
        out_shape=jax.ShapeDtypeStruct((rows, MEM_W), f32 if tq < 16 else bf16),
        compiler_params=_params(2),
        name="mem_attn",
    )(p, mem_k, mem_v)


def _softmax_sink_pv(s, sink_col, v):
    m = jnp.maximum(jnp.max(s, -1, keepdims=True), sink_col)
    e = jnp.exp(s - m)
    den = jnp.sum(e, -1, keepdims=True) + jnp.exp(sink_col - m)
    p = e * (1.0 / den)
    return _dot(p.astype(bf16), v)


def _sink_col(sink_ref, g, tq):
    return jnp.concatenate([jnp.full((tq, 1), sink_ref[REP * g + r], f32) for r in range(REP)], axis=0)


def _swa_prompt_kernel(q_ref, kc_ref, kp_ref, vc_ref, vp_ref, bias_ref, sink_ref, o_ref):
    i = pl.program_id(1)
    scale = HEAD_DIM ** -0.5
    for g in range(KV_HEADS):
        sl = slice(g * HEAD_DIM, (g + 1) * HEAD_DIM)
        heads = [slice((REP * g + r) * HEAD_DIM, (REP * g + r + 1) * HEAD_DIM) for r in range(REP)]
        kk = jnp.concatenate([kp_ref[:, sl], kc_ref[:, sl]], axis=0).astype(bf16)
        vt = jnp.concatenate([vp_ref[:, sl], vc_ref[:, sl]], axis=0).T.astype(bf16)
        qt = jnp.concatenate([q_ref[:, h].T for h in heads], axis=1).astype(bf16)
        s = _dot(kk, qt) * scale + bias_ref[g]
        key = lax.broadcasted_iota(i32, s.shape, 0)
        s = jnp.where((key >= WINDOW) | (i > 0), s, NEG_INF)
        sink = jnp.concatenate([jnp.full((1, WINDOW), sink_ref[REP * g + r], f32) for r in range(REP)], axis=1)
        m = jnp.maximum(jnp.max(s, axis=0, keepdims=True), sink)
        e = jnp.exp(s - m)
        den = jnp.sum(e, axis=0, keepdims=True) + jnp.exp(sink - m)
        o = _dot(vt, (e * (1.0 / den)).astype(bf16))
        for r, h in enumerate(heads):
            o_ref[:, h] = o[:, r * WINDOW:(r + 1) * WINDOW].T.astype(o_ref.dtype)


def swa_prompt(p, B, S, bias, sink):
    nb = S // WINDOW
    cur = lambda c: (lambda b, i: (b * nb + i, c))
    prev = lambda c: (lambda b, i: (b * nb + jnp.maximum(i - 1, 0), c))
    kc, vc = K_COL // KV_HEADS, V_COL // KV_HEADS
    return pl.pallas_call(
        _swa_prompt_kernel,
        grid=(B, nb),
        in_specs=[pl.BlockSpec((WINDOW, Q_W), cur(0)),
                  pl.BlockSpec((WINDOW, KV_W), cur(kc)), pl.BlockSpec((WINDOW, KV_W), prev(kc)),
                  pl.BlockSpec((WINDOW, KV_W), cur(vc)), pl.BlockSpec((WINDOW, KV_W), prev(vc)),
                  pl.BlockSpec((KV_HEADS, 2 * WINDOW, REP * WINDOW), lambda b, i: (0, 0, 0)),
                  pl.BlockSpec(memory_space=pltpu.SMEM)],
        out_specs=pl.BlockSpec((WINDOW, Q_W), lambda b, i: (b * nb + i, 0)),
        out_shape=jax.ShapeDtypeStruct((B * S, Q_W), bf16),
        compiler_params=_params(2),
        name="swa_prompt",
    )(p, p, p, p, p, bias, sink)


def _swa_sample_kernel(q_ref, kn_ref, vn_ref, wk_ref, wv_ref, bias_ref, sink_ref, o_ref, nk_ref, nv_ref, *, T):
    scale = HEAD_DIM ** -0.5
    kn = kn_ref[...]
    vn = vn_ref[...]
    pad = jnp.zeros((WINDOW - T, HEAD_DIM), f32)
    for g in range(KV_HEADS):
        sl = slice(g * HEAD_DIM, (g + 1) * HEAD_DIM)
        wk = wk_ref[0, 0, pl.ds(g, WINDOW, stride=KV_HEADS), :]
        wv = wv_ref[0, 0, pl.ds(g, WINDOW, stride=KV_HEADS), :]
        kk = jnp.concatenate([wk, kn[:, sl], pad], axis=0).astype(bf16)
        vv = jnp.concatenate([wv, vn[:, sl], pad], axis=0).astype(bf16)
        q3 = jnp.concatenate([q_ref[:, (REP * g + r) * HEAD_DIM:(REP * g + r + 1) * HEAD_DIM]
                              for r in range(REP)], axis=0).astype(bf16)
        s = _dot_nt(q3, kk) * scale + bias_ref[g]
        o = _softmax_sink_pv(s, _sink_col(sink_ref, g, T), vv)
        for r in range(REP):
            o_ref[:, (REP * g + r) * HEAD_DIM:(REP * g + r + 1) * HEAD_DIM] = o[r * T:(r + 1) * T]
    kept = (WINDOW - T) * KV_HEADS
    nk_ref[0, 0:kept, :] = wk_ref[0, 0, T * KV_HEADS:, :]
    nv_ref[0, 0:kept, :] = wv_ref[0, 0, T * KV_HEADS:, :]
    for g in range(KV_HEADS):
        sl = slice(g * HEAD_DIM, (g + 1) * HEAD_DIM)
        nk_ref[0, pl.ds(kept + g, T, stride=KV_HEADS), :] = kn[:, sl]
        nv_ref[0, pl.ds(kept + g, T, stride=KV_HEADS), :] = vn[:, sl]


def swa_sample(p, row_block0, DB, T, win_k, win_v, li, bias, sink):
    kc, vc = K_COL // KV_HEADS, V_COL // KV_HEADS
    win = lambda n: (li, n, 0, 0)
    return pl.pallas_call(
        functools.partial(_swa_sample_kernel, T=T),
        grid=(DB,),
        in_specs=[pl.BlockSpec((T, Q_W), lambda n: (row_block0 + n, 0)),
                  pl.BlockSpec((T, KV_W), lambda n: (row_block0 + n, kc)),
                  pl.BlockSpec((T, KV_W), lambda n: (row_block0 + n, vc)),
                  pl.BlockSpec((1, 1, WINDOW * KV_HEADS, HEAD_DIM), win),
                  pl.BlockSpec((1, 1, WINDOW * KV_HEADS, HEAD_DIM), win),
                  pl.BlockSpec((KV_HEADS, REP * T, 2 * WINDOW), lambda n: (0, 0, 0)),
                  pl.BlockSpec(memory_space=pltpu.SMEM)],
        out_specs=[pl.BlockSpec((T, Q_W), lambda n: (n, 0)),
                   pl.BlockSpec((1, WINDOW * KV_HEADS, HEAD_DIM), lambda n: (n, 0, 0)),
                   pl.BlockSpec((1, WINDOW * KV_HEADS, HEAD_DIM), lambda n: (n, 0, 0))],
        out_shape=[jax.ShapeDtypeStruct((DB * T, Q_W), f32),
                   jax.ShapeDtypeStruct((DB, WINDOW * KV_HEADS, HEAD_DIM), f32),
                   jax.ShapeDtypeStruct((DB, WINDOW * KV_HEADS, HEAD_DIM), f32)],
        compiler_params=_params(1),
        name="swa_sample",
    )(p, p, p, win_k, win_v, bias, sink)


def _split_components(q, scale):
    lane = lax.broadcasted_iota(i32, q.shape, 1)
    qs = q * scale
    return [jnp.where(lane < DIFF_DIM, qs, 0.0), jnp.where(lane >= DIFF_DIM, qs, 0.0)]


def _diff_finish(o0, o1, lam, gain, lam_init, axis):
    o = o0 - lam * o1
    return o * lax.rsqrt(jnp.mean(o * o, axis, keepdims=True) + RMS_EPS) * gain * (1.0 - lam_init)


def _diff_prompt_kernel(q_ref, k_ref, v_ref, bias_ref, lam_ref, gain_ref, o_ref,
                        kb_scr, vt_scr, qpt_scr, m_scr, l_scr, acc_scr, *, lam_init):
    i = pl.program_id(2)
    tq, tk = DIFF_TQ, DIFF_TK
    n_chunks = k_ref.shape[0] // tk

    @pl.when(i == 0)
    def _():
        for c in range(n_chunks):
            kb_scr[c] = k_ref[c * tk:(c + 1) * tk, :].astype(bf16)
            vt_scr[c] = v_ref[c * tk:(c + 1) * tk, :].T.astype(bf16)

    pieces = []
    for r in range(REP):
        for part in _split_components(q_ref[:, r * HEAD_DIM:(r + 1) * HEAD_DIM], DIFF_DIM ** -0.5 * LOG2_E):
            pieces.append(part.T)
    qpt_scr[...] = jnp.concatenate(pieces, axis=1).astype(bf16)
    m_scr[...] = jnp.full(m_scr.shape, NEG_INF, f32)
    l_scr[...] = jnp.zeros(l_scr.shape, f32)
    acc_scr[...] = jnp.zeros(acc_scr.shape, f32)

    def chunk(c, n, bias):
        if n == 1:
            kb, vt = kb_scr[c], vt_scr[c]
        else:
            kb = kb_scr[pl.ds(c, n)].reshape(n * tk, HEAD_DIM)
            vt = jnp.concatenate([vt_scr[c + d] for d in range(n)], axis=1)
        s = _dot(kb, qpt_scr[...])
        if bias is not None:
            s = s + bias
        m_old = m_scr[...]
        m_new = jnp.maximum(m_old, jnp.max(s, axis=0, keepdims=True))
        p = jnp.exp2(s - m_new)
        alpha = jnp.exp2(m_old - m_new)
        l_scr[...] = alpha * l_scr[...] + jnp.sum(p, axis=0, keepdims=True)
        acc_scr[...] = alpha * acc_scr[...] + _dot(vt, p.astype(bf16))
        m_scr[...] = m_new

    step = tq // tk
    for d in range(step):
        chunk(i * step + d, 1, bias_ref[0, tq + d * tk:tq + (d + 1) * tk, :])

    @pl.when(i > 0)
    def _():
        for d in range(step):
            chunk((i - 1) * step + d, 1, bias_ref[0, d * tk:(d + 1) * tk, :])

    n_far = jnp.maximum(i - 1, 0) * step
    n_group = n_far // DIFF_FAR_CHUNKS

    def far_group(j, carry):
        chunk(j * DIFF_FAR_CHUNKS, DIFF_FAR_CHUNKS, None)
        return carry

    def far_single(c, carry):
        chunk(c, 1, None)
        return carry

    lax.fori_loop(0, n_group, far_group, 0)
    lax.fori_loop(n_group * DIFF_FAR_CHUNKS, n_far, far_single, 0)

    acc = acc_scr[...] * (1.0 / l_scr[...])
    lam = lam_ref[0]
    for r in range(REP):
        o = _diff_finish(acc[:, (2 * r) * tq:(2 * r + 1) * tq], acc[:, (2 * r + 1) * tq:(2 * r + 2) * tq],
                         lam, gain_ref[...], lam_init, 0)
        o_ref[:, r * HEAD_DIM:(r + 1) * HEAD_DIM] = o.T.astype(o_ref.dtype)


def diff_prompt(p, B, S, bias_t, lam, gain_col, lam_init):
    nq = S // DIFF_TQ
    cols = 2 * REP * DIFF_TQ
    n_chunks = S // DIFF_TK
    return pl.pallas_call(
        functools.partial(_diff_prompt_kernel, lam_init=lam_init),
        grid=(B, KV_HEADS, nq),
        in_specs=[pl.BlockSpec((DIFF_TQ, REP * HEAD_DIM), lambda b, g, i: (b * nq + i, g)),
                  pl.BlockSpec((S, HEAD_DIM), lambda b, g, i: (b, K_COL + g)),
                  pl.BlockSpec((S, HEAD_DIM), lambda b, g, i: (b, V_COL + g)),
                  pl.BlockSpec((1, 2 * DIFF_TQ, cols), lambda b, g, i: (g, 0, 0)),
                  pl.BlockSpec(memory_space=pltpu.SMEM),
                  pl.BlockSpec((HEAD_DIM, 1), lambda b, g, i: (0, 0))],
        out_specs=pl.BlockSpec((DIFF_TQ, REP * HEAD_DIM), lambda b, g, i: (b * nq + i, g)),
        out_shape=jax.ShapeDtypeStruct((B * S, Q_W), bf16),
        scratch_shapes=[pltpu.VMEM((n_chunks, DIFF_TK, HEAD_DIM), bf16),
                        pltpu.VMEM((n_chunks, HEAD_DIM, DIFF_TK), bf16),
                        pltpu.VMEM((HEAD_DIM, cols), bf16),
                        pltpu.VMEM((1, cols), f32),
                        pltpu.VMEM((1, cols), f32),
                        pltpu.VMEM((HEAD_DIM, cols), f32)],
        compiler_params=_params(3),
        name="diff_prompt",
    )(p, p, p, bias_t, lam, gain_col)


def _diff_sample_kernel(pt_ref, q_ref, kn_ref, vn_ref, *rest, T, lam_init, n_steps):
    npg = PAGES_PER_STEP
    k_refs, v_refs = rest[:npg], rest[npg:2 * npg]
    bias_last_ref, bias_new_ref, lam_ref, gain_ref, o_ref, qp_scr, m_scr, l_scr, acc_scr = rest[2 * npg:]
    s_id = pl.program_id(1)
    rows = 2 * REP * T

    @pl.when(s_id == 0)
    def _():
        pieces = []
        for h in range(N_HEADS):
            pieces += _split_components(q_ref[:, h * HEAD_DIM:(h + 1) * HEAD_DIM], DIFF_DIM ** -0.5)
        qp_scr[...] = jnp.concatenate(pieces, axis=0)
        m_scr[...] = jnp.full(m_scr.shape, NEG_INF, f32)
        l_scr[...] = jnp.zeros(l_scr.shape, f32)
        acc_scr[...] = jnp.zeros(acc_scr.shape, f32)

    def update(s, vs):
        m_old = m_scr[...]
        m_new = jnp.maximum(m_old, jnp.max(s, -1, keepdims=True))
        p = jnp.exp(s - m_new)
        alpha = jnp.exp(m_old - m_new)
        l_scr[...] = alpha * l_scr[...] + jnp.sum(p, -1, keepdims=True)
        pb = p.astype(bf16)
        pv = jnp.concatenate([_dot(pb[g * rows:(g + 1) * rows], vs[g]) for g in range(KV_HEADS)], axis=0)
        acc_scr[...] = alpha * acc_scr[...] + pv
        m_scr[...] = m_new

    def scores(ks):
        qp = qp_scr[...].astype(bf16)
        return jnp.concatenate([_dot_nt(qp[g * rows:(g + 1) * rows], ks[g]) for g in range(KV_HEADS)], axis=0)

    def head_rows(refs, g):
        return jnp.concatenate([r[0, 0, pl.ds(g, PAGE_SIZE, stride=KV_HEADS), :] for r in refs],
                               axis=0).astype(bf16)

    last = s_id == n_steps - 1
    s = scores([head_rows(k_refs, g) for g in range(KV_HEADS)])
    tail = s[:, (npg - 1) * PAGE_SIZE:] + jnp.where(last, bias_last_ref[...], 0.0)
    s = jnp.concatenate([s[:, :(npg - 1) * PAGE_SIZE], tail], axis=1)
    update(s, [head_rows(v_refs, g) for g in range(KV_HEADS)])

    @pl.when(last)
    def _():
        pad = jnp.zeros((PAGE_SIZE - T, HEAD_DIM), f32)
        kn = [jnp.concatenate([kn_ref[:, g * HEAD_DIM:(g + 1) * HEAD_DIM], pad], axis=0).astype(bf16)
              for g in range(KV_HEADS)]
        vn = [jnp.concatenate([vn_ref[:, g * HEAD_DIM:(g + 1) * HEAD_DIM], pad], axis=0).astype(bf16)
              for g in range(KV_HEADS)]
        update(scores(kn) + bias_new_ref[...], vn)
        acc = acc_scr[...] * (1.0 / l_scr[...])
        lam = lam_ref[0]
        for h in range(N_HEADS):
            o = _diff_finish(acc[(2 * h) * T:(2 * h + 1) * T], acc[(2 * h + 1) * T:(2 * h + 2) * T],
                             lam, gain_ref[...], lam_init, -1)
            o_ref[:, h * HEAD_DIM:(h + 1) * HEAD_DIM] = o


def diff_sample(p, row_block0, DB, T, cache_k, cache_v, li, page_table, bias_last, bias_new, lam, gain, lam_init):
    n_pages = PAST_LEN // PAGE_SIZE
    npg = PAGES_PER_STEP
    n_steps = n_pages // npg
    rows = 2 * N_HEADS * T
    kc, vc = K_COL // KV_HEADS, V_COL // KV_HEADS

    def page(j):
        return lambda n, s, pt: (li, pt[n * n_pages + s * npg + j], 0, 0)

    page_block = (1, 1, PAGE_SIZE * KV_HEADS, HEAD_DIM)
    const2 = lambda n, s, pt: (0, 0)
    in_specs = ([pl.BlockSpec((T, Q_W), lambda n, s, pt: (row_block0 + n, 0)),
                 pl.BlockSpec((T, KV_W), lambda n, s, pt: (row_block0 + n, kc)),
                 pl.BlockSpec((T, KV_W), lambda n, s, pt: (row_block0 + n, vc))]
                + [pl.BlockSpec(page_block, page(j)) for j in range(npg)]
                + [pl.BlockSpec(page_block, page(j)) for j in range(npg)]
                + [pl.BlockSpec((rows, PAGE_SIZE), const2),
                   pl.BlockSpec((rows, PAGE_SIZE), const2),
                   pl.BlockSpec(memory_space=pltpu.SMEM),
                   pl.BlockSpec((1, HEAD_DIM), const2)])
    return pl.pallas_call(
        functools.partial(_diff_sample_kernel, T=T, lam_init=lam_init, n_steps=n_steps),
        grid_spec=pltpu.PrefetchScalarGridSpec(
            num_scalar_prefetch=1,
            grid=(DB, n_steps),
            in_specs=in_specs,
            out_specs=pl.BlockSpec((T, Q_W), lambda n, s, pt: (n, 0)),
            scratch_shapes=[pltpu.VMEM((rows, HEAD_DIM), f32),
                            pltpu.VMEM((rows, 1), f32),
                            pltpu.VMEM((rows, 1), f32),
                            pltpu.VMEM((rows, HEAD_DIM), f32)],
        ),
        out_shape=jax.ShapeDtypeStruct((DB * T, Q_W), f32),
        compiler_params=_params(2),
        name="diff_sample",
    )(page_table, p, p, p, *([cache_k] * npg), *([cache_v] * npg), bias_last, bias_new, lam, gain)


def _row_tile(total, candidates):
    for t in candidates:
        if total % t == 0:
            return t
    raise ValueError(f"no row tile for {total} rows")


def kernel(x_prompt, x_sample, cache_win_k, cache_win_v, cache_diff_k, cache_diff_v, cache_mem_k, cache_mem_v,
           page_table, mem_prompt, w_in, w_mem_kv, w_o, rel_bias, sinks, lam_q1, lam_k1, lam_q2, lam_k2, subln_g,
           ln1_g, ln1_b, ln2_g, ln2_b, w_router, router_bias, w_gate_up, w_down):
    B, S, _ = x_prompt.shape
    DB, T, _ = x_sample.shape
    TP, TS = B * S, DB * T
    TT = TP + TS
    assert S % DIFF_TQ == 0 and S % WINDOW == 0 and TP % T == 0 and T % 8 == 0
    n_swa, n_pool = cache_win_k.shape[0], cache_diff_k.shape[1]
    sample_block0 = TP // T

    x = jnp.concatenate([x_prompt.reshape(TP, D_MODEL), x_sample.reshape(TS, D_MODEL)], axis=0)
    xb = x.astype(bf16)
    w_o_b = w_o.astype(bf16)
    w_mem_b = w_mem_kv.astype(bf16)
    mem_b = mem_prompt.reshape(B * MEM_LEN, D_MODEL).astype(bf16)
    w_router_t = w_router.T
    router_bias_b = jnp.broadcast_to(router_bias.astype(f32)[:, None], (N_EXPERTS, 128))
    win_k = cache_win_k.reshape(n_swa, DB, WINDOW * KV_HEADS, HEAD_DIM)
    win_v = cache_win_v.reshape(n_swa, DB, WINDOW * KV_HEADS, HEAD_DIM)
    pool_k = cache_diff_k.reshape(-1, n_pool, PAGE_SIZE * KV_HEADS, HEAD_DIM)
    pool_v = cache_diff_v.reshape(-1, n_pool, PAGE_SIZE * KV_HEADS, HEAD_DIM)
    cmem_k = cache_mem_k.reshape(DEPTH, DB, MEM_LEN * MEM_HEADS, HEAD_DIM)
    cmem_v = cache_mem_v.reshape(DEPTH, DB, MEM_LEN * MEM_HEADS, HEAD_DIM)
    pt_flat = page_table.reshape(-1).astype(i32)

    far = rel_bias[NUM_BUCKETS - 1].astype(f32)
    qo = jnp.arange(WINDOW)
    rel = (jnp.arange(2 * WINDOW) - WINDOW)[None, :] - qo[:, None]
    bias_swa_p = jnp.swapaxes(_group_rows(_bias_tile(rel_bias, rel, (rel <= 0) & (rel > -WINDOW)), 1), 1, 2)
    tt = jnp.arange(T)
    kpos = jnp.concatenate([jnp.arange(WINDOW) - WINDOW, tt, jnp.full((WINDOW - T,), T)])
    rel = kpos[None, :] - tt[:, None]
    bias_swa_s = _group_rows(_bias_tile(rel_bias, rel, (rel <= 0) & (rel > -WINDOW)), 1)
    qo = jnp.arange(DIFF_TQ)
    rel = (jnp.arange(2 * DIFF_TQ) - DIFF_TQ)[None, :] - qo[:, None]
    bias_diff_p = jnp.swapaxes(_group_rows(_bias_tile(rel_bias, rel, rel <= 0, far), 2), 1, 2) * LOG2_E
    rel = (jnp.arange(PAGE_SIZE) - PAGE_SIZE)[None, :] - tt[:, None]
    bias_diff_last = _group_rows(_bias_tile(rel_bias, rel, rel <= 0, far), 2).reshape(-1, PAGE_SIZE)
    kpos = jnp.concatenate([tt, jnp.full((PAGE_SIZE - T,), T)])
    rel = kpos[None, :] - tt[:, None]
    bias_diff_new = _group_rows(_bias_tile(rel_bias, rel, rel <= 0, far), 2).reshape(-1, PAGE_SIZE)

    tm_mm = _row_tile(TT, (1056, 1024, 512, 256))
    tm_ln = _row_tile(TT, (352, 256, 128))
    tm_wo = _row_tile(math.gcd(TP, TS), (256, 128))
    tm_rt = _row_tile(TT, (1408, 1024, 512, 256, 128))
    tq_mem = _row_tile(S, (512, 256, 128))

    win_k_p, win_v_p, win_k_s, win_v_s = [], [], [], []
    diff_k_p, diff_v_p, diff_k_s, diff_v_s = [], [], [], []
    mem_k_p, mem_v_p = [], []
    for l in range(DEPTH):
        i = l // 2
        p, k_rows, v_rows = in_proj(xb, w_in, l, tm_mm)
        k_p = k_rows[:TP * KV_HEADS].reshape(B, S, KV_HEADS, HEAD_DIM)
        v_p = v_rows[:TP * KV_HEADS].reshape(B, S, KV_HEADS, HEAD_DIM)
        if l % 2 == 0:
            sink = sinks[i].astype(f32)
            self_p = swa_prompt(p, B, S, bias_swa_p, sink)
            self_s, nk, nv = swa_sample(p, sample_block0, DB, T, win_k, win_v, i, bias_swa_s, sink)
            win_k_p.append(k_p[:, S - WINDOW:])
            win_v_p.append(v_p[:, S - WINDOW:])
            win_k_s.append(nk.reshape(DB, WINDOW, KV_HEADS, HEAD_DIM))
            win_v_s.append(nv.reshape(DB, WINDOW, KV_HEADS, HEAD_DIM))
        else:
            lam_init = 0.8 - 0.6 * math.exp(-0.3 * l)
            lam = (jnp.exp(jnp.sum(lam_q1[i].astype(f32) * lam_k1[i].astype(f32)))
                   - jnp.exp(jnp.sum(lam_q2[i].astype(f32) * lam_k2[i].astype(f32))) + lam_init).reshape(1)
            gain = subln_g[i].astype(f32).reshape(1, HEAD_DIM)
            self_p = diff_prompt(p, B, S, bias_diff_p, lam, gain.reshape(HEAD_DIM, 1), lam_init)
            self_s = diff_sample(p, sample_block0, DB, T, pool_k, pool_v, i, pt_flat,
                                 bias_diff_last, bias_diff_new, lam, gain, lam_init)
            diff_k_p.append(k_p)
            diff_v_p.append(v_p)
            diff_k_s.append(k_rows[TP * KV_HEADS:].reshape(DB, T, KV_HEADS, HEAD_DIM))
            diff_v_s.append(v_rows[TP * KV_HEADS:].reshape(DB, T, KV_HEADS, HEAD_DIM))
        mkv = matmul(mem_b, w_mem_b[l], B * MEM_LEN, 512).reshape(B, MEM_LEN, 2 * MEM_W)
        mem_k_p.append(mkv[:, :, :MEM_W].reshape(B, MEM_LEN, MEM_HEADS, HEAD_DIM))
        mem_v_p.append(mkv[:, :, MEM_W:].reshape(B, MEM_LEN, MEM_HEADS, HEAD_DIM))
        cross_p = mem_attend(p, 0, B, tq_mem, S // tq_mem, mkv, mkv,
                             lambda n, t: (n, 0, 0), lambda n, t: (n, 0, 1), (1, MEM_LEN, MEM_W))
        cross_s = mem_attend(p, sample_block0, DB, T, 1, cmem_k, cmem_v,
                             lambda n, t: (l, n, 0, 0), lambda n, t: (l, n, 0, 0),
                             (1, 1, MEM_LEN * MEM_HEADS, HEAD_DIM))
        x = wo_ln(self_p, cross_p, self_s, cross_s, x, w_o_b[l, :Q_W], w_o_b[l, Q_W:],
                  ln1_g[l].reshape(1, D_MODEL), ln1_b[l].reshape(1, D_MODEL), tm_wo)
        e_idx, gate = router(x, w_router_t, router_bias_b, tm_rt)
        dest, row_tok, blk_e, n_used = moe_dispatch(e_idx)
        xs = gather_rows(x, row_tok, n_used, MOE_TM, bf16)
        yb = moe_ffn_sorted(xs, blk_e, n_used, w_gate_up, w_down, l)
        x, xb = combine_ln(x, yb, dest, gate.T,
                           ln2_g[l].reshape(1, D_MODEL), ln2_b[l].reshape(1, D_MODEL), tm_ln)
    return (x[:TP].reshape(B, S, D_MODEL), x[TP:].reshape(DB, T, D_MODEL),
            jnp.stack(win_k_p), jnp.stack(win_v_p), jnp.stack(win_k_s), jnp.stack(win_v_s),
            jnp.stack(diff_k_p), jnp.stack(diff_v_p), jnp.stack(diff_k_s), jnp.stack(diff_v_s),
            jnp.stack(mem_k_p), jnp.stack(mem_v_p))
```
